```python
import math
import jax, jax.numpy as jnp
from jax import lax
import numpy as np


D_MODEL = 1024
BATCH = 2
SEQ = 8192
DEPTH = 2

GRID_W = 64
MEM_LEN = 256
HEAD_DIM = 64
NA_HEADS = 6
NA_WIDTH = NA_HEADS * HEAD_DIM
HY_WIDTH = 256
DIL_HEADS = 6
DIL_WIDTH = DIL_HEADS * HEAD_DIM
MIX_WIDTH = NA_WIDTH + HY_WIDTH + DIL_WIDTH
IN_COLS = 3 * MIX_WIDTH
NA_ROWS = 8
NA_COLS = 16
HY_SHORT = 3
HY_BANDS = 16
HY_EMB = 2 * HY_BANDS + 1
HY_FILTER_W = 64
HY_SHORT_DECAY_PCT = 0.3
HY_LONG_DECAY_PCT = 1.5
HY_TARGET = 1e-2
DIL_CONFIGS = ((128, 1), (512, 4), (2048, 16))
T5_BUCKETS = 32
T5_MAX_DIST = 1024
XA_HEADS = 4
XA_HEAD_DIM = D_MODEL // XA_HEADS
FFN_HIDDEN = 2816
RMS_EPS = 1e-6
NEG_INF = -1e30
FFN1_PRE, FFN1_POST, MIX_PRE, MIX_GROUP, MIX_POST, XA_PRE, XA_MEM, XA_POST, FFN2_PRE, FFN2_POST = range(10)
N_NORMS = 10

kernel_name = "hybrid_natten_hyena_dilated_encoder"


def rms_norm(x, g):
    xf = x.astype(jnp.float32)
    y = xf * lax.rsqrt(jnp.mean(xf * xf, axis=-1, keepdims=True) + RMS_EPS)
    return (y * g.astype(jnp.float32)).astype(x.dtype)


def swiglu(x, w_in, w_out):
    gate, up = jnp.split(x @ w_in, 2, axis=-1)
    return (jax.nn.silu(gate) * up) @ w_out


def t5_bucket(rel):
    half = T5_BUCKETS // 2
    exact = half // 2
    n = np.abs(rel)
    far = exact + (np.log(np.maximum(n, 1) / exact) / math.log(T5_MAX_DIST / exact) * (half - exact)).astype(np.int32)
    far = np.minimum(far, half - 1)
    return (np.where(rel > 0, half, 0) + np.where(n < exact, n, far)).astype(np.int32)


def neighbourhood_attn(q, k, v, rpb):
    B, L, H, hd = q.shape
    rows = L // GRID_W
    wr = min(NA_ROWS, rows)
    qg = q.reshape(B, rows, GRID_W, H, hd)
    kg = k.reshape(B, rows, GRID_W, H, hd)
    vg = v.reshape(B, rows, GRID_W, H, hd)
    col = np.arange(GRID_W)
    cs = np.clip(col - NA_COLS // 2, 0, GRID_W - NA_COLS)
    col_idx = cs[:, None] + np.arange(NA_COLS)[None, :]
    col_off = col_idx - col[:, None] + NA_COLS - 1

    def row_step(r):
        rs = jnp.clip(r - wr // 2, 0, rows - wr)
        kr = lax.dynamic_slice_in_dim(kg, rs, wr, axis=1)
        vr = lax.dynamic_slice_in_dim(vg, rs, wr, axis=1)
        kn = kr[:, :, col_idx]
        vn = vr[:, :, col_idx]
        qr = lax.dynamic_index_in_dim(qg, r, axis=1, keepdims=False)
        s = jnp.einsum('bqhd,bwqkhd->bhqwk', qr, kn).astype(jnp.float32)
        row_off = rs + jnp.arange(wr) - r + NA_ROWS - 1
        bias = rpb[:, row_off[None, :, None], col_off[:, None, :]]
        s = s + bias.astype(jnp.float32)[None]
        p = jax.nn.softmax(s.reshape(B, H, GRID_W, wr * NA_COLS), axis=-1)
        p = p.reshape(B, H, GRID_W, wr, NA_COLS).astype(v.dtype)
        return jnp.einsum('bhqwk,bwqkhd->bqhd', p, vn)

    out = lax.map(row_step, jnp.arange(rows))
    return out.transpose(1, 0, 2, 3, 4).reshape(B, L, H, hd)


def dilated_attn(q, k, v, dil, half, t5_table):
    B, L, H, hd = q.shape
    M = L // dil
    blk = half
    nb = -(-M // blk)
    Mp = nb * blk

    def classes(t):
        return t.reshape(B, M, dil, H, hd).transpose(0, 2, 1, 3, 4)

    qb = jnp.pad(classes(q), ((0, 0), (0, 0), (0, Mp - M), (0, 0), (0, 0))).reshape(B, dil, nb, blk, H, hd)

    def key_windows(t):
        tp = jnp.pad(classes(t), ((0, 0), (0, 0), (blk, Mp - M + blk), (0, 0), (0, 0)))
        tp = tp.reshape(B, dil, nb + 2, blk, H, hd)
        return jnp.concatenate([tp[:, :, :-2], tp[:, :, 1:-1], tp[:, :, 2:]], axis=3)

    kw = key_windows(k)
    vw = key_windows(v)
    rel = np.arange(3 * blk)[None, :] - blk - np.arange(blk)[:, None]
    bias = t5_table[t5_bucket(dil * rel)].transpose(2, 0, 1).astype(jnp.float32)
    key_idx = np.arange(nb)[:, None] * blk - blk + np.arange(3 * blk)[None, :]
    valid = (np.abs(rel) <= half)[None] & ((key_idx >= 0) & (key_idx < M))[:, None, :]
    s = jnp.einsum('bgnqhd,bgnkhd->bghnqk', qb, kw).astype(jnp.float32) + bias[None, None, :, None]
    s = jnp.where(valid[None, None, None], s, NEG_INF)
    m = jnp.max(s, axis=-1, keepdims=True)
    p = jnp.exp(s - m)
    den = jnp.sum(p, axis=-1, keepdims=True)
    o = jnp.einsum('bghnqk,bgnkhd->bgnqhd', (p / den).astype(v.dtype), vw)
    lse = (m + jnp.log(den))[..., 0]
    o = o.reshape(B, dil, Mp, H, hd)[:, :, :M].transpose(0, 2, 1, 3, 4).reshape(B, L, H, hd)
    lse = lse.transpose(0, 1, 3, 4, 2).reshape(B, dil, Mp, H)[:, :, :M].transpose(0, 2, 1, 3).reshape(B, L, H)
    return o, lse


def dilated_mixer(q, k, v, t5_table):
    outs, lses = [], []
    for window, dil in DIL_CONFIGS:
        o, lse = dilated_attn(q, k, v, dil, window // (2 * dil), t5_table)
        outs.append(o)
        lses.append(lse)
    alpha = jax.nn.softmax(jnp.stack(lses), axis=0)
    return jnp.einsum('cblh,cblhd->blhd', alpha.astype(q.dtype), jnp.stack(outs))


def short_conv(u, w, b):
    up = jnp.pad(u, ((0, 0), (1, 1), (0, 0)))
    return up[:, :-2] * w[0] + up[:, 1:-1] * w[1] + up[:, 2:] * w[2] + b


def hyena_filters(L, w1, b1, w2, b2, w3, b3, w4, freq, decay):
    f32 = jnp.float32
    t = jnp.linspace(0.0, 1.0, L, dtype=f32)[:, None]
    bands = jnp.linspace(1e-4, HY_BANDS - 1, HY_BANDS, dtype=f32)[None, :]
    ang = (2.0 * math.pi / L) * jnp.arange(L, dtype=f32)[:, None] * bands
    z = jnp.concatenate([t, jnp.cos(ang), -jnp.sin(ang)], axis=-1)
    fr = freq.astype(f32)
    h = jnp.sin(fr * (z @ w1.astype(f32) + b1.astype(f32)))
    h = jnp.sin(fr * (h @ w2.astype(f32) + b2.astype(f32)))
    h = jnp.sin(fr * (h @ w3.astype(f32) + b3.astype(f32)))
    h = h @ w4.astype(f32)
    return h * jnp.exp(-t * jnp.abs(decay.astype(f32)))


def bidir_long_conv(z, h_fwd, h_bwd):
    L, C = h_fwd.shape
    filt = jnp.concatenate([h_fwd, jnp.zeros((1, C), h_fwd.dtype), h_bwd[:0:-1]], axis=0)
    zf = jnp.fft.rfft(z, n=2 * L, axis=1)
    ff = jnp.fft.rfft(filt, axis=0)
    return jnp.fft.irfft(zf * ff[None], n=2 * L, axis=1)[:, :L]


def hyena_mixer(p, conv_w, conv_b, w1, b1, w2, b2, w3, b3, w4, freq, decay, skip):
    L = p.shape[1]
    uc = short_conv(p, conv_w, conv_b)
    v, x1, x0 = jnp.split(uc, 3, axis=-1)
    h = hyena_filters(L, w1, b1, w2, b2, w3, b3, w4, freq, decay)
    z = (x1 * v).astype(jnp.float32)
    y = bidir_long_conv(z, h[:, :HY_WIDTH], h[:, HY_WIDTH:]) + skip.astype(jnp.float32) * z
    return (x0.astype(jnp.float32) * y).astype(p.dtype)


def memory_cross_attn(h, m, wq, wkv, wo):
    B, L, _ = h.shape
    S = m.shape[1]
    q = (h @ wq).reshape(B, L, XA_HEADS, XA_HEAD_DIM) * XA_HEAD_DIM ** -0.5
    kv = (m @ wkv).reshape(B, S, 2, XA_HEADS, XA_HEAD_DIM)
    s = jnp.einsum('blhd,bmhd->bhlm', q, kv[:, :, 0]).astype(jnp.float32)
    p = jax.nn.softmax(s, axis=-1).astype(h.dtype)
    o = jnp.einsum('bhlm,bmhd->blhd', p, kv[:, :, 1]).reshape(B, L, D_MODEL)
    return o @ wo


def setup_inputs(seed: int = 0) -> dict:
    key = jax.random.key(seed)
    ks = jax.random.split(key, 26)

    def nrm(k, shape, scale):
        return scale * jax.random.normal(k, shape, jnp.float32)

    base = jnp.abs(jnp.linspace(math.log(HY_TARGET) / HY_LONG_DECAY_PCT, math.log(HY_TARGET) / HY_SHORT_DECAY_PCT, HY_WIDTH, dtype=jnp.float32))
    decay = jnp.concatenate([base, base])[None, :] * (1.0 + nrm(ks[17], (DEPTH, 2 * HY_WIDTH), 0.05))
    return {
        'x': nrm(ks[0], (BATCH, SEQ, D_MODEL), 1.0),
        'mem': nrm(ks[1], (BATCH, MEM_LEN, D_MODEL), 1.0),
        'norm_g': 1.0 + nrm(ks[2], (DEPTH, N_NORMS, D_MODEL), 0.05),
        'w_in': nrm(ks[3], (DEPTH, D_MODEL, IN_COLS), D_MODEL ** -0.5),
        'w_out': nrm(ks[4], (DEPTH, MIX_WIDTH, D_MODEL), MIX_WIDTH ** -0.5),
        'na_rpb': nrm(ks[5], (DEPTH, NA_HEADS, 2 * NA_ROWS - 1, 2 * NA_COLS - 1), 0.1),
        't5_table': nrm(ks[6], (T5_BUCKETS, DIL_HEADS), 0.1),
        'hy_conv_w': nrm(ks[7], (DEPTH, HY_SHORT, 3 * HY_WIDTH), HY_SHORT ** -0.5),
        'hy_conv_b': nrm(ks[8], (DEPTH, 3 * HY_WIDTH), 0.01),
        'hy_f_w1': nrm(ks[9], (DEPTH, HY_EMB, HY_FILTER_W), HY_EMB ** -0.5),
        'hy_f_b1': nrm(ks[10], (DEPTH, HY_FILTER_W), 0.1),
        'hy_f_w2': nrm(ks[11], (DEPTH, HY_FILTER_W, HY_FILTER_W), HY_FILTER_W ** -0.5),
        'hy_f_b2': nrm(ks[12], (DEPTH, HY_FILTER_W), 0.1),
        'hy_f_w3': nrm(ks[13], (DEPTH, HY_FILTER_W, HY_FILTER_W), HY_FILTER_W ** -0.5),
        'hy_f_b3': nrm(ks[14], (DEPTH, HY_FILTER_W), 0.1),
        'hy_f_w4': nrm(ks[15], (DEPTH, HY_FILTER_W, 2 * HY_WIDTH), HY_FILTER_W ** -0.5),
        'hy_freq': 1.0 + nrm(ks[16], (DEPTH, HY_FILTER_W), 0.01),
        'hy_decay': decay,
        'hy_skip': nrm(ks[18], (DEPTH, HY_WIDTH), 1.0),
        'xa_wq': nrm(ks[19], (DEPTH, D_MODEL, D_MODEL), D_MODEL ** -0.5),
        'xa_wkv': nrm(ks[20], (DEPTH, D_MODEL, 2 * D_MODEL), D_MODEL ** -0.5),
        'xa_wo': nrm(ks[21], (DEPTH, D_MODEL, D_MODEL), D_MODEL ** -0.5),
        'ffn_w_in': nrm(ks[22], (DEPTH, 2, D_MODEL, 2 * FFN_HIDDEN), D_MODEL ** -0.5),
        'ffn_w_out': nrm(ks[23], (DEPTH, 2, FFN_HIDDEN, D_MODEL), FFN_HIDDEN ** -0.5),
    }


def reference(x, mem, norm_g, w_in, w_out, na_rpb, t5_table, hy_conv_w, hy_conv_b,
              hy_f_w1, hy_f_b1, hy_f_w2, hy_f_b2, hy_f_w3, hy_f_b3, hy_f_w4,
              hy_freq, hy_decay, hy_skip, xa_wq, xa_wkv, xa_wo, ffn_w_in, ffn_w_out):
    B, L, _ = x.shape
    scale = HEAD_DIM ** -0.5
    split_b = 3 * NA_WIDTH
    split_c = 3 * NA_WIDTH + 3 * HY_WIDTH
    for l in range(DEPTH):
        g = norm_g[l]
        x = x + 0.5 * rms_norm(swiglu(rms_norm(x, g[FFN1_PRE]), ffn_w_in[l, 0], ffn_w_out[l, 0]), g[FFN1_POST])

        u = rms_norm(x, g[MIX_PRE])
        proj = u @ w_in[l]
        pa, pb, pc = jnp.split(proj, [split_b, split_c], axis=-1)
        qa, ka, va = [t.reshape(B, L, NA_HEADS, HEAD_DIM) for t in jnp.split(pa, 3, axis=-1)]
        ya = neighbourhood_attn(qa * scale, ka, va, na_rpb[l]).reshape(B, L, NA_WIDTH)
        yb = hyena_mixer(pb, hy_conv_w[l], hy_conv_b[l], hy_f_w1[l], hy_f_b1[l], hy_f_w2[l], hy_f_b2[l],
                         hy_f_w3[l], hy_f_b3[l], hy_f_w4[l], hy_freq[l], hy_decay[l], hy_skip[l])
        qc, kc, vc = [t.reshape(B, L, DIL_HEADS, HEAD_DIM) for t in jnp.split(pc, 3, axis=-1)]
        yc = dilated_mixer(qc * scale, kc, vc, t5_table).reshape(B, L, DIL_WIDTH)
        gg = g[MIX_GROUP]
        y = jnp.concatenate([
            rms_norm(ya, gg[:NA_WIDTH]),
            rms_norm(yb, gg[NA_WIDTH:NA_WIDTH + HY_WIDTH]),
            rms_norm(yc, gg[NA_WIDTH + HY_WIDTH:]),
        ], axis=-1)
        x = x + rms_norm(y @ w_out[l], g[MIX_POST])

        xa = memory_cross_attn(rms_norm(x, g[XA_PRE]), rms_norm(mem, g[XA_MEM]), xa_wq[l], xa_wkv[l], xa_wo[l])
        x = x + rms_norm(xa, g[XA_POST])

        x = x + 0.5 * rms_norm(swiglu(rms_norm(x, g[FFN2_PRE]), ffn_w_in[l, 1], ffn_w_out[l, 1]), g[FFN2_POST])
    return x
```

```python
import functools
import math

import numpy as np
import jax
import jax.numpy as jnp
from jax import lax
from jax.experimental import pallas as pl
from jax.experimental.pallas import tpu as pltpu

F32 = jnp.float32
BF16 = jnp.bfloat16

D_MODEL = 1024
SEQ = 8192
GRID_W = 64
HEAD_DIM = 64
NA_WIDTH = 384
HY_WIDTH = 256
DIL_WIDTH = 384
NA_ROWS = 8
NA_COLS = 16
HY_BANDS = 16
DIL_CONFIGS = ((128, 1), (512, 4), (2048, 16))
T5_BUCKETS = 32
T5_MAX_DIST = 1024
XA_HEADS = 4
XA_HEAD_DIM = 256
FFN_HIDDEN = 2816
RMS_EPS = 1e-6
NEG_INF = -1e30

LANES = 128
MIB = 1024 * 1024

NA_QROWS = 8
NA_KROWS = 16
NA_BQ = NA_QROWS * GRID_W
NA_BK = NA_KROWS * GRID_W
DIL_HALF = 64
DIL_BQ = 256
DIL_BK = DIL_BQ + 2 * DIL_HALF
DFT_R = 128
DFT_HALF = 64
DFT_N = DFT_R * DFT_R
DFT_PITCH = 264


def _cparams(sem, vmem_mib):
    return pltpu.CompilerParams(dimension_semantics=sem, vmem_limit_bytes=vmem_mib * MIB)


def _rms(x, g):
    return x * lax.rsqrt(jnp.mean(x * x, axis=-1, keepdims=True) + RMS_EPS) * g


FFN_TM = 1024
FFN_TH = 256


def _ffn_kernel(x_ref, gpre_ref, wg_ref, wu_ref, wo_ref, gpost_ref, o_ref, h_scr, acc_scr):
    j = pl.program_id(1)

    @pl.when(j == 0)
    def _():
        h_scr[...] = _rms(x_ref[...], gpre_ref[...]).astype(BF16)
        acc_scr[...] = jnp.zeros_like(acc_scr)

    h = h_scr[...]
    gate = jnp.dot(h, wg_ref[...], preferred_element_type=F32)
    up = jnp.dot(h, wu_ref[...], preferred_element_type=F32)
    act = (gate * jax.nn.sigmoid(gate) * up).astype(BF16)
    acc_scr[...] += jnp.dot(act, wo_ref[...], preferred_element_type=F32)

    @pl.when(j == pl.num_programs(1) - 1)
    def _():
        o_ref[...] = x_ref[...] + 0.5 * _rms(acc_scr[...], gpost_ref[...])


def _ffn(x, g_pre, w_in, w_out, g_post):
    n = x.shape[0]
    nh = FFN_HIDDEN // FFN_TH
    return pl.pallas_call(
        _ffn_kernel,
        grid=(n // FFN_TM, nh),
        in_specs=[
            pl.BlockSpec((FFN_TM, D_MODEL), lambda i, j: (i, 0)),
            pl.BlockSpec((1, D_MODEL), lambda i, j: (0, 0)),
            pl.BlockSpec((D_MODEL, FFN_TH), lambda i, j: (0, j)),
            pl.BlockSpec((D_MODEL, FFN_TH), lambda i, j: (0, j + nh)),
            pl.BlockSpec((FFN_TH, D_MODEL), lambda i, j: (j, 0)),
            pl.BlockSpec((1, D_MODEL), lambda i, j: (0, 0)),
        ],
        out_specs=pl.BlockSpec((FFN_TM, D_MODEL), lambda i, j: (i, 0)),
        out_shape=jax.ShapeDtypeStruct((n, D_MODEL), F32),
        scratch_shapes=[pltpu.VMEM((FFN_TM, D_MODEL), BF16), pltpu.VMEM((FFN_TM, D_MODEL), F32)],
        compiler_params=_cparams(("parallel", "arbitrary"), 48),
        name="ffn",
    )(x, g_pre, w_in, w_in, w_out, g_post)


PROJ_TM = 512


def _mix_proj_kernel(x_ref, g_ref, w_ref, cs_ref, pa_ref, pb_ref, pc_ref):
    h = _rms(x_ref[...], g_ref[...]).astype(BF16)
    p = jnp.dot(h, w_ref[...], preferred_element_type=F32) * cs_ref[...]
    wa = 3 * NA_WIDTH
    wb = 3 * HY_WIDTH
    pa_ref[...] = p[:, :wa]
    pb_ref[...] = p[:, wa:wa + wb]
    pc_ref[...] = p[:, wa + wb:]


def _mix_proj(x, g, w, colscale):
    n = x.shape[0]
    wa, wb, wc = 3 * NA_WIDTH, 3 * HY_WIDTH, 3 * DIL_WIDTH
    ncol = wa + wb + wc
    return pl.pallas_call(
        _mix_proj_kernel,
        grid=(n // PROJ_TM,),
        in_specs=[
            pl.BlockSpec((PROJ_TM, D_MODEL), lambda i: (i, 0)),
            pl.BlockSpec((1, D_MODEL), lambda i: (0, 0)),
            pl.BlockSpec((D_MODEL, ncol), lambda i: (0, 0)),
            pl.BlockSpec((1, ncol), lambda i: (0, 0)),
        ],
        out_specs=[
            pl.BlockSpec((PROJ_TM, wa), lambda i: (i, 0)),
            pl.BlockSpec((PROJ_TM, wb), lambda i: (i, 0)),
            pl.BlockSpec((PROJ_TM, wc), lambda i: (i, 0)),
        ],
        out_shape=[
            jax.ShapeDtypeStruct((n, wa), F32),
            jax.ShapeDtypeStruct((n, wb), F32),
            jax.ShapeDtypeStruct((n, wc), F32),
        ],
        compiler_params=_cparams(("parallel",), 48),
        name="mix_proj",
    )(x, g, w, colscale)


def _norm_mm_kernel(x_ref, g_ref, w_ref, o_ref):
    h = _rms(x_ref[...], g_ref[...]).astype(BF16)
    o_ref[...] = jnp.dot(h, w_ref[...], preferred_element_type=F32)


def _norm_mm(x, g, w, tm):
    n, ncol = x.shape[0], w.shape[1]
    return pl.pallas_call(
        _norm_mm_kernel,
        grid=(n // tm,),
        in_specs=[
            pl.BlockSpec((tm, D_MODEL), lambda i: (i, 0)),
            pl.BlockSpec((1, D_MODEL), lambda i: (0, 0)),
            pl.BlockSpec((D_MODEL, ncol), lambda i: (0, 0)),
        ],
        out_specs=pl.BlockSpec((tm, ncol), lambda i: (i, 0)),
        out_shape=jax.ShapeDtypeStruct((n, ncol), F32),
        compiler_params=_cparams(("parallel",), 32),
        name="norm_mm",
    )(x, g, w)


def _pair_attention(q, k, v, bias_ref_fn):
    lane = lax.broadcasted_iota(jnp.int32, (1, LANES), 1)
    res = []
    for hh in range(2):
        keep = (lane < HEAD_DIM) if hh == 0 else (lane >= HEAD_DIM)
        qh = jnp.where(keep, q, 0.0).astype(BF16)
        s = lax.dot_general(qh, k, (((1,), (1,)), ((), ())), preferred_element_type=F32)
        s = s + bias_ref_fn(hh)
        m = jnp.max(s, axis=-1, keepdims=True)
        p = jnp.exp(s - m)
        l = jnp.sum(p, axis=-1, keepdims=True)
        o = jnp.dot(p.astype(BF16), v, preferred_element_type=F32)
        res.append((o, m, l))
    return res, lane


def _na_kernel(q_ref, k_ref, v_ref, bias_ref, o_ref):
    j = pl.program_id(2)
    nrows = SEQ // GRID_W
    start_row = jnp.clip(j * NA_QROWS - NA_ROWS // 2, 0, nrows - NA_KROWS)
    start = pl.multiple_of(start_row * GRID_W, GRID_W)
    k = k_ref[0, pl.ds(start, NA_BK), :].astype(BF16)
    v = v_ref[0, pl.ds(start, NA_BK), :].astype(BF16)
    res, lane = _pair_attention(q_ref[0], k, v, lambda hh: bias_ref[0, hh])
    (o0, _, l0), (o1, _, l1) = res
    o_ref[0] = jnp.where(lane < HEAD_DIM, o0 / l0, o1 / l1)


def _na_attention(pa, bias):
    b = pa.shape[0]
    npair = NA_WIDTH // LANES
    nblk = SEQ // NA_BQ

    def case(j):
        return jnp.where(j == 0, 0, jnp.where(j == nblk - 1, 2, 1))

    return pl.pallas_call(
        _na_kernel,
        grid=(b, npair, nblk),
        in_specs=[
            pl.BlockSpec((1, NA_BQ, LANES), lambda bi, p, j: (bi, j, p)),
            pl.BlockSpec((1, SEQ, LANES), lambda bi, p, j: (bi, 0, npair + p)),
            pl.BlockSpec((1, SEQ, LANES), lambda bi, p, j: (bi, 0, 2 * npair + p)),
            pl.BlockSpec((1, 2, NA_BQ, NA_BK), lambda bi, p, j: (case(j), p, 0, 0)),
        ],
        out_specs=pl.BlockSpec((1, NA_BQ, LANES), lambda bi, p, j: (bi, j, p)),
        out_shape=jax.ShapeDtypeStruct((b, SEQ, NA_WIDTH), F32),
        compiler_params=_cparams(("parallel", "parallel", "arbitrary"), 48),
        name="na_attn",
    )(pa, pa, pa, bias)


def _na_bias(rpb):
    nrows = SEQ // GRID_W
    col = np.arange(GRID_W)
    cs = np.clip(col - NA_COLS // 2, 0, GRID_W - NA_COLS)
    kc = np.arange(GRID_W)
    col_ok = (kc[None, :] >= cs[:, None]) & (kc[None, :] < cs[:, None] + NA_COLS)
    col_off = np.clip(kc[None, :] - col[:, None] + NA_COLS - 1, 0, 2 * NA_COLS - 2)
    tz = jnp.where(col_ok[None, None], rpb[:, :, col_off], NEG_INF)
    cases = []
    for j in (0, 1, SEQ // NA_BQ - 1):
        r = j * NA_QROWS + np.arange(NA_QROWS)
        start_row = int(np.clip(j * NA_QROWS - NA_ROWS // 2, 0, nrows - NA_KROWS))
        kr = start_row + np.arange(NA_KROWS)
        rs = np.clip(r - NA_ROWS // 2, 0, nrows - NA_ROWS)
        row_ok = (kr[None, :] >= rs[:, None]) & (kr[None, :] < rs[:, None] + NA_ROWS)
        row_off = np.clip(kr[None, :] - r[:, None] + NA_ROWS - 1, 0, 2 * NA_ROWS - 2)
        t = jnp.where(row_ok[None, :, :, None, None], tz[:, row_off], NEG_INF)
        cases.append(t.transpose(0, 1, 3, 2, 4).reshape(rpb.shape[0], NA_BQ, NA_BK))
    return jnp.stack(cases)


def _dil_kernel(q_ref, k_ref, v_ref, bias_ref, o_ref, m_scr, l_scr):
    lane = lax.broadcasted_iota(jnp.int32, (1, LANES), 1)
    kcol = lax.broadcasted_iota(jnp.int32, (1, DIL_BK), 1)

    for c, (_, dil) in enumerate(DIL_CONFIGS):
        m_len = SEQ // dil
        nblk = m_len // DIL_BQ

        def body(it, carry, c=c, dil=dil, m_len=m_len, nblk=nblk):
            g = it // nblk
            n = it % nblk
            pos = n * DIL_BQ

            def rows(p0, size):
                return pl.ds(g + p0 * dil, size, stride=dil) if dil > 1 else pl.ds(p0, size)

            lo = jnp.maximum(pos - DIL_HALF, 0)
            hi = jnp.minimum(pos + DIL_BQ, m_len - DIL_HALF)
            qrows = rows(pos, DIL_BQ)
            k = jnp.concatenate(
                [k_ref[0, rows(lo, DIL_HALF), :], k_ref[0, qrows, :], k_ref[0, rows(hi, DIL_HALF), :]],
                axis=0).astype(BF16)
            v = jnp.concatenate(
                [v_ref[0, rows(lo, DIL_HALF), :], v_ref[0, qrows, :], v_ref[0, rows(hi, DIL_HALF), :]],
                axis=0).astype(BF16)
            edge = jnp.where(((kcol < DIL_HALF) & (n == 0))
                             | ((kcol >= DIL_HALF + DIL_BQ) & (n == nblk - 1)), NEG_INF, 0.0)
            res, _ = _pair_attention(q_ref[0, qrows, :], k, v, lambda hh: bias_ref[c, 0, hh] + edge)
            (o0, m0, l0), (o1, m1, l1) = res
            o = jnp.where(lane < HEAD_DIM, o0, o1)
            m = jnp.where(lane < HEAD_DIM, m0, m1)
            l = jnp.where(lane < HEAD_DIM, l0, l1)
            if c == 0:
                o_ref[0, qrows, :] = o
                m_scr[qrows, :] = m
                l_scr[qrows, :] = l
            else:
                m_old = m_scr[qrows, :]
                m_new = jnp.maximum(m_old, m)
                a_old = jnp.exp(m_old - m_new)
                a_new = jnp.exp(m - m_new)
                o_ref[0, qrows, :] = o_ref[0, qrows, :] * a_old + o * a_new
                l_scr[qrows, :] = l_scr[qrows, :] * a_old + l * a_new
                m_scr[qrows, :] = m_new
            return carry

        lax.fori_loop(0, SEQ // DIL_BQ, body, 0)

    def finish(i, carry):
        r = pl.ds(pl.multiple_of(i * 512, 512), 512)
        o_ref[0, r, :] = o_ref[0, r, :] / l_scr[r, :]
        return carry

    lax.fori_loop(0, SEQ // 512, finish, 0)


def _dil_attention(pc, bias):
    b = pc.shape[0]
    npair = DIL_WIDTH // LANES
    return pl.pallas_call(
        _dil_kernel,
        grid=(b, npair),
        in_specs=[
            pl.BlockSpec((1, SEQ, LANES), lambda bi, p: (bi, 0, p)),
            pl.BlockSpec((1, SEQ, LANES), lambda bi, p: (bi, 0, npair + p)),
            pl.BlockSpec((1, SEQ, LANES), lambda bi, p: (bi, 0, 2 * npair + p)),
            pl.BlockSpec((len(DIL_CONFIGS), 1, 2, DIL_BQ, DIL_BK), lambda bi, p: (0, p, 0, 0, 0)),
        ],
        out_specs=pl.BlockSpec((1, SEQ, LANES), lambda bi, p: (bi, 0, p)),
        out_shape=jax.ShapeDtypeStruct((b, SEQ, DIL_WIDTH), F32),
        scratch_shapes=[pltpu.VMEM((SEQ, LANES), F32), pltpu.VMEM((SEQ, LANES), F32)],
        compiler_params=_cparams(("parallel", "arbitrary"), 56),
        name="dil_attn",
    )(pc, pc, pc, bias)


def _t5_bucket(rel):
    half = T5_BUCKETS // 2
    exact = half // 2
    n = np.abs(rel)
    far = exact + (np.log(np.maximum(n, 1) / exact) / math.log(T5_MAX_DIST / exact) * (half - exact)).astype(np.int32)
    far = np.minimum(far, half - 1)
    return (np.where(rel > 0, half, 0) + np.where(n < exact, n, far)).astype(np.int32)


def _dil_bias(t5_table):
    rel = np.arange(DIL_BK)[None, :] - DIL_HALF - np.arange(DIL_BQ)[:, None]
    ok = np.abs(rel) <= DIL_HALF
    out = []
    for _, dil in DIL_CONFIGS:
        t = t5_table[_t5_bucket(dil * rel)].transpose(2, 0, 1)
        out.append(jnp.where(ok[None], t, NEG_INF))
    nh = t5_table.shape[1]
    return jnp.stack(out).reshape(len(DIL_CONFIGS), nh // 2, 2, DIL_BQ, DIL_BK)


CONV_TM = 1024


def _conv_gate_kernel(p_ref, prev_ref, next_ref, w_ref, b_ref, z_ref, x0_ref):
    i = pl.program_id(1)
    p = p_ref[0]
    tm = p.shape[0]
    first = jnp.where(i > 0, prev_ref[0, 7:8, :], 0.0)
    last = jnp.where(i < pl.num_programs(1) - 1, next_ref[0, 0:1, :], 0.0)
    row = lax.broadcasted_iota(jnp.int32, (tm, 1), 0)
    up = jnp.where(row == 0, first, pltpu.roll(p, 1, axis=0))
    dn = jnp.where(row == tm - 1, last, pltpu.roll(p, tm - 1, axis=0))
    uc = up * w_ref[0:1, :] + p * w_ref[1:2, :] + dn * w_ref[2:3, :] + b_ref[...]
    z_ref[0] = uc[:, HY_WIDTH:2 * HY_WIDTH] * uc[:, :HY_WIDTH]
    x0_ref[0] = uc[:, 2 * HY_WIDTH:]


def _conv_gate(pb, w, bvec):
    b = pb.shape[0]
    wb = 3 * HY_WIDTH
    nt = SEQ // CONV_TM
    r8 = CONV_TM // 8
    return pl.pallas_call(
        _conv_gate_kernel,
        grid=(b, nt),
        in_specs=[
            pl.BlockSpec((1, CONV_TM, wb), lambda bi, i: (bi, i, 0)),
            pl.BlockSpec((1, 8, wb), lambda bi, i: (bi, jnp.maximum(i * r8 - 1, 0), 0)),
            pl.BlockSpec((1, 8, wb), lambda bi, i: (bi, jnp.minimum((i + 1) * r8, SEQ // 8 - 1), 0)),
            pl.BlockSpec((3, wb), lambda bi, i: (0, 0)),
            pl.BlockSpec((1, wb), lambda bi, i: (0, 0)),
        ],
        out_specs=[
            pl.BlockSpec((1, CONV_TM, HY_WIDTH), lambda bi, i: (bi, i, 0)),
            pl.BlockSpec((1, CONV_TM, HY_WIDTH), lambda bi, i: (bi, i, 0)),
        ],
        out_shape=[jax.ShapeDtypeStruct((b, SEQ, HY_WIDTH), F32)] * 2,
        compiler_params=_cparams(("parallel", "parallel"), 48),
        name="hy_conv_gate",
    )(pb, pb, pb, w, bvec)


FILT_TM = 1024
FILT_KPAD = 128


def _filter_kernel(z_ref, w1_ref, b1_ref, w2_ref, b2_ref, w3_ref, b3_ref, w4_ref, fr_ref, dec_ref, h_ref):
    hp = lax.Precision.HIGHEST
    z = z_ref[...]
    fr = fr_ref[...]
    h = jnp.sin(fr * (jnp.dot(z, w1_ref[...], precision=hp, preferred_element_type=F32) + b1_ref[...]))
    h = jnp.sin(fr * (jnp.dot(h, w2_ref[...], precision=hp, preferred_element_type=F32) + b2_ref[...]))
    h = jnp.sin(fr * (jnp.dot(h, w3_ref[...], precision=hp, preferred_element_type=F32) + b3_ref[...]))
    h = jnp.dot(h, w4_ref[...], precision=hp, preferred_element_type=F32)
    h = h * jnp.exp(-z[:, 0:1] * jnp.abs(dec_ref[...]))
    row = pl.program_id(0) * FILT_TM + lax.broadcasted_iota(jnp.int32, (FILT_TM, 1), 0)
    col = lax.broadcasted_iota(jnp.int32, (1, 2 * HY_WIDTH), 1)
    h_ref[0] = jnp.where((row == 0) & (col >= HY_WIDTH), 0.0, h)


def _filter_features():
    t = jnp.linspace(0.0, 1.0, SEQ, dtype=F32)[:, None]
    bands = jnp.linspace(1e-4, HY_BANDS - 1, HY_BANDS, dtype=F32)[None, :]
    ang = (2.0 * math.pi / SEQ) * jnp.arange(SEQ, dtype=F32)[:, None] * bands
    z = jnp.concatenate([t, jnp.cos(ang), -jnp.sin(ang)], axis=-1)
    return jnp.pad(z, ((0, 0), (0, FILT_KPAD - z.shape[1])))


def _hyena_filters(zfeat, w1, b1, w2, b2, w3, b3, w4, freq, decay):
    fw = w2.shape[0]
    w1p = jnp.pad(w1, ((0, FILT_KPAD - w1.shape[0]), (0, 0)))
    full = lambda a: pl.BlockSpec(a.shape, lambda i: (0,) * a.ndim)
    args = (w1p, b1[None], w2, b2[None], w3, b3[None], w4, freq[None], decay[None])
    del fw
    return pl.pallas_call(
        _filter_kernel,
        grid=(SEQ // FILT_TM,),
        in_specs=[pl.BlockSpec((FILT_TM, FILT_KPAD), lambda i: (i, 0))] + [full(a) for a in args],
        out_specs=pl.BlockSpec((1, FILT_TM, 2 * HY_WIDTH), lambda i: (0, i, 0)),
        out_shape=jax.ShapeDtypeStruct((1, SEQ, 2 * HY_WIDTH), F32),
        compiler_params=_cparams(("parallel",), 32),
        name="hy_filter",
    )(zfeat, *args)


def _dft_tables():
    a = np.arange(DFT_R)
    w = np.exp(-2j * np.pi * np.outer(a, a) / DFT_R)
    tw = np.exp(-2j * np.pi * np.outer(a, a) / DFT_N)
    p = w[None, :, :DFT_HALF] * tw[:, :, None]
    pm = np.concatenate([p.real, p.imag], axis=1)
    qm = np.block([[w.real, -w.imag], [w.imag, w.real]])
    gm = np.block([[w.real, w.imag], [-w.imag, w.real]])
    m = np.conj(w[None, :DFT_HALF, :] * tw[:, None, :]) / DFT_N
    mm = np.concatenate([m.real, -m.imag], axis=2)
    f = lambda x: jnp.asarray(x.astype(np.float32)).astype(BF16)
    return f(pm), f(qm), f(gm), f(mm)


def _dft_fwd_kernel(x_ref, pm_ref, qm_ref, o_ref, s_scr):
    k1 = pl.program_id(2)

    @pl.when(k1 == 0)
    def _():
        def stage1(n2, carry):
            xs = x_ref[0, pl.ds(n2, DFT_HALF, stride=DFT_R), :].astype(BF16)
            a = jnp.dot(pm_ref[n2], xs, preferred_element_type=F32)
            s_scr[pl.ds(pl.multiple_of(n2 * DFT_PITCH, 8), 2 * DFT_R), :] = a
            return carry

        lax.fori_loop(0, DFT_R, stage1, 0)

    re = s_scr[pl.ds(k1, DFT_R, stride=DFT_PITCH), :]
    im = s_scr[pl.ds(DFT_R + k1, DFT_R, stride=DFT_PITCH), :]
    a = jnp.concatenate([re, im], axis=0).astype(BF16)
    o_ref[0, 0] = jnp.dot(qm_ref[...], a, preferred_element_type=F32)


def _dft_fwd(x, pm, qm):
    nb, _, c = x.shape
    return pl.pallas_call(
        _dft_fwd_kernel,
        grid=(nb, c // LANES, DFT_R),
        in_specs=[
            pl.BlockSpec((1, SEQ, LANES), lambda b, ct, k: (b, 0, ct)),
            pl.BlockSpec(pm.shape, lambda b, ct, k: (0, 0, 0)),
            pl.BlockSpec(qm.shape, lambda b, ct, k: (0, 0)),
        ],
        out_specs=pl.BlockSpec((1, 1, 2 * DFT_R, LANES), lambda b, ct, k: (b, k, 0, ct)),
        out_shape=jax.ShapeDtypeStruct((nb, DFT_R, 2 * DFT_R, c), F32),
        scratch_shapes=[pltpu.VMEM((DFT_R * DFT_PITCH, LANES), F32)],
        compiler_params=_cparams(("parallel", "parallel", "arbitrary"), 48),
        name="hy_dft_fwd",
    )(x, pm, qm)


def _dft_inv_kernel(x_ref, hf_ref, hb_ref, gm_ref, mm_ref, o_ref, s_scr):
    k1 = pl.program_id(2)
    xr, xi = x_ref[0, 0, :DFT_R, :], x_ref[0, 0, DFT_R:, :]
    hr = hf_ref[0, 0, :DFT_R, :] + hb_ref[0, 0, :DFT_R, :]
    hi = hf_ref[0, 0, DFT_R:, :] - hb_ref[0, 0, DFT_R:, :]
    y = jnp.concatenate([xr * hr - xi * hi, xr * hi + xi * hr], axis=0).astype(BF16)
    s_scr[pl.ds(pl.multiple_of(k1 * DFT_PITCH, 8), 2 * DFT_R), :] = jnp.dot(
        gm_ref[...], y, preferred_element_type=F32)

    @pl.when(k1 == pl.num_programs(2) - 1)
    def _():
        def stage1(n2, carry):
            re = s_scr[pl.ds(n2, DFT_R, stride=DFT_PITCH), :]
            im = s_scr[pl.ds(DFT_R + n2, DFT_R, stride=DFT_PITCH), :]
            bcat = jnp.concatenate([re, im], axis=0).astype(BF16)
            o_ref[0, pl.ds(n2, DFT_HALF, stride=DFT_R), :] = jnp.dot(
                mm_ref[n2], bcat, preferred_element_type=F32)
            return carry

        lax.fori_loop(0, DFT_R, stage1, 0)


def _dft_conv_inv(xz, hs, gm, mm):
    b, _, _, c = xz.shape
    nct = c // LANES
    spec = lambda f: pl.BlockSpec((1, 1, 2 * DFT_R, LANES), f)
    return pl.pallas_call(
        _dft_inv_kernel,
        grid=(b, nct, DFT_R),
        in_specs=[
            spec(lambda bi, ct, k: (bi, k, 0, ct)),
            spec(lambda bi, ct, k: (0, k, 0, ct)),
            spec(lambda bi, ct, k: (0, k, 0, nct + ct)),
            pl.BlockSpec(gm.shape, lambda bi, ct, k: (0, 0)),
            pl.BlockSpec(mm.shape, lambda bi, ct, k: (0, 0, 0)),
        ],
        out_specs=pl.BlockSpec((1, SEQ, LANES), lambda bi, ct, k: (bi, 0, ct)),
        out_shape=jax.ShapeDtypeStruct((b, SEQ, c), F32),
        scratch_shapes=[pltpu.VMEM((DFT_R * DFT_PITCH, LANES), F32)],
        compiler_params=_cparams(("parallel", "parallel", "arbitrary"), 48),
        name="hy_dft_inv",
    )(xz, hs, hs, gm, mm)


MIXOUT_TM = 512


def _mix_out_kernel(x_ref, ya_ref, yh_ref, z_ref, x0_ref, yc_ref, skip_ref, gg_ref, w_ref, gpost_ref, o_ref):
    a0, b0 = NA_WIDTH, NA_WIDTH + HY_WIDTH
    yb = x0_ref[...] * (yh_ref[...] + skip_ref[...] * z_ref[...])
    acc = jnp.dot(_rms(ya_ref[...], gg_ref[:, :a0]).astype(BF16), w_ref[:a0, :], preferred_element_type=F32)
    acc += jnp.dot(_rms(yb, gg_ref[:, a0:b0]).astype(BF16), w_ref[a0:b0, :], preferred_element_type=F32)
    acc += jnp.dot(_rms(yc_ref[...], gg_ref[:, b0:]).astype(BF16), w_ref[b0:, :], preferred_element_type=F32)
    o_ref[...] = x_ref[...] + _rms(acc, gpost_ref[...])


def _mix_out(x, ya, yh, z, x0, yc, skip, gg, w, gpost):
    n = x.shape[0]
    tok = lambda width: pl.BlockSpec((MIXOUT_TM, width), lambda i: (i, 0))
    full = lambda a: pl.BlockSpec(a.shape, lambda i: (0,) * a.ndim)
    return pl.pallas_call(
        _mix_out_kernel,
        grid=(n // MIXOUT_TM,),
        in_specs=[tok(D_MODEL), tok(NA_WIDTH), tok(HY_WIDTH), tok(HY_WIDTH), tok(HY_WIDTH), tok(DIL_WIDTH),
                  full(skip), full(gg), full(w), full(gpost)],
        out_specs=tok(D_MODEL),
        out_shape=jax.ShapeDtypeStruct((n, D_MODEL), F32),
        compiler_params=_cparams(("parallel",), 40),
        name="mix_out",
    )(x, ya, yh, z, x0, yc, skip, gg, w, gpost)


XA_TM = 512


def _xattn_kernel(x_ref, gpre_ref, wq_ref, kv_ref, wo_ref, gpost_ref, o_ref):
    x = x_ref[...]
    h = _rms(x, gpre_ref[...]).astype(BF16)
    q = jnp.dot(h, wq_ref[...], preferred_element_type=F32) * (XA_HEAD_DIM ** -0.5)
    outs = []
    for hd in range(XA_HEADS):
        c0 = hd * XA_HEAD_DIM
        qh = q[:, c0:c0 + XA_HEAD_DIM].astype(BF16)
        kh = kv_ref[0, :, c0:c0 + XA_HEAD_DIM].astype(BF16)
        vh = kv_ref[0, :, D_MODEL + c0:D_MODEL + c0 + XA_HEAD_DIM].astype(BF16)
        s = lax.dot_general(qh, kh, (((1,), (1,)), ((), ())), preferred_element_type=F32)
        p = jnp.exp(s - jnp.max(s, axis=-1, keepdims=True))
        l = jnp.sum(p, axis=-1, keepdims=True)
        outs.append(jnp.dot(p.astype(BF16), vh, preferred_element_type=F32) / l)
    o = jnp.concatenate(outs, axis=-1).astype(BF16)
    xa = jnp.dot(o, wo_ref[...], preferred_element_type=F32)
    o_ref[...] = x + _rms(xa, gpost_ref[...])


def _xattn(x, gpre, wq, kv, wo, gpost):
    n = x.shape[0]
    per_batch = SEQ // XA_TM
    full = lambda a: pl.BlockSpec(a.shape, lambda i: (0,) * a.ndim)
    return pl.pallas_call(
        _xattn_kernel,
        grid=(n // XA_TM,),
        in_specs=[
            pl.BlockSpec((XA_TM, D_MODEL), lambda i: (i, 0)),
            full(gpre), full(wq),
            pl.BlockSpec((1,) + kv.shape[1:], lambda i: (i // per_batch, 0, 0)),
            full(wo), full(gpost),
        ],
        out_specs=pl.BlockSpec((XA_TM, D_MODEL), lambda i: (i, 0)),
        out_shape=jax.ShapeDtypeStruct((n, D_MODEL), F32),
        compiler_params=_cparams(("parallel",), 40),
        name="xattn",
    )(x, gpre, wq, kv, wo, gpost)


def kernel(x, mem, norm_g, w_in, w_out, na_rpb, t5_table, hy_conv_w, hy_conv_b, hy_f_w1, hy_f_b1, hy_f_w2, hy_f_b2, hy_f_w3, hy_f_b3, hy_f_w4, hy_freq, hy_decay, hy_skip, xa_wq, xa_wkv, xa_wo, ffn_w_in, ffn_w_out):
    bsz, seq, d = x.shape
    assert (seq, d) == (SEQ, D_MODEL)
    depth = norm_g.shape[0]
    mem_len = mem.shape[1]
    n = bsz * seq
    scale = HEAD_DIM ** -0.5
    wa, wb, wc = 3 * NA_WIDTH, 3 * HY_WIDTH, 3 * DIL_WIDTH
    colscale = np.ones((1, wa + wb + wc), np.float32)
    colscale[:, :NA_WIDTH] = scale
    colscale[:, wa + wb:wa + wb + DIL_WIDTH] = scale
    colscale = jnp.asarray(colscale)

    zfeat = _filter_features()
    pm, qm, gm, mm = _dft_tables()
    dil_bias = _dil_bias(t5_table)

    xf = x.reshape(n, d)
    memf = mem.reshape(bsz * mem_len, d)
    for l in range(depth):
        g = norm_g[l][:, None, :]
        xf = _ffn(xf, g[0], ffn_w_in[l, 0].astype(BF16), ffn_w_out[l, 0].astype(BF16), g[1])

        pa, pb, pc = _mix_proj(xf, g[2], w_in[l].astype(BF16), colscale)
        pa = pa.reshape(bsz, seq, wa)
        pb = pb.reshape(bsz, seq, wb)
        pc = pc.reshape(bsz, seq, wc)
        ya = _na_attention(pa, _na_bias(na_rpb[l]))
        z, x0 = _conv_gate(pb, hy_conv_w[l], hy_conv_b[l][None])
        hfilt = _hyena_filters(zfeat, hy_f_w1[l], hy_f_b1[l], hy_f_w2[l], hy_f_b2[l], hy_f_w3[l], hy_f_b3[l],
                               hy_f_w4[l], hy_freq[l], hy_decay[l])
        yh = _dft_conv_inv(_dft_fwd(z, pm, qm), _dft_fwd(hfilt, pm, qm), gm, mm)
        yc = _dil_attention(pc, dil_bias)
        xf = _mix_out(xf, ya.reshape(n, NA_WIDTH), yh.reshape(n, HY_WIDTH), z.reshape(n, HY_WIDTH),
                      x0.reshape(n, HY_WIDTH), yc.reshape(n, DIL_WIDTH), hy_skip[l][None], g[3],
                      w_out[l].astype(BF16), g[4])

        kv = _norm_mm(memf, g[6], xa_wkv[l].astype(BF16), mem_len).reshape(bsz, mem_len, 2 * D_MODEL)
        xf = _xattn(xf, g[5], xa_wq[l].astype(BF16), kv, xa_wo[l].astype(BF16), g[7])

        xf = _ffn(xf, g[8], ffn_w_in[l, 1].astype(BF16), ffn_w_out[l, 1].astype(BF16), g[9])
    return xf.reshape(bsz, seq, d)
```

```python
import functools
import math

import numpy as np
import jax
import jax.numpy as jnp
from jax import lax
from jax.experimental import pallas as pl
from jax.experimental.pallas import tpu as pltpu

F32 = jnp.float32
BF16 = jnp.bfloat16

D_MODEL = 1024
SEQ = 8192
GRID_W = 64
HEAD_DIM = 64
NA_WIDTH = 384
HY_WIDTH = 256
DIL_WIDTH = 384
NA_ROWS = 8
NA_COLS = 16
HY_BANDS = 16
DIL_CONFIGS = ((128, 1), (512, 4), (2048, 16))
T5_BUCKETS = 32
T5_MAX_DIST = 1024
XA_HEADS = 4
XA_HEAD_DIM = 256
FFN_HIDDEN = 2816
RMS_EPS = 1e-6
NEG_INF = -1e30

LANES = 128
MIB = 1024 * 1024

NA_QROWS = 8
NA_KROWS = 16
NA_BQ = NA_QROWS * GRID_W
NA_BK = NA_KROWS * GRID_W
DIL_HALF = 64
DIL_BQ = 256
DIL_BK = DIL_BQ + 2 * DIL_HALF
DFT_R = 128
DFT_HALF = 64
DFT_N = DFT_R * DFT_R
DFT_PITCH = 264


def _cparams(sem, vmem_mib):
    return pltpu.CompilerParams(dimension_semantics=sem, vmem_limit_bytes=vmem_mib * MIB)


def _rms(x, g):
    return x * lax.rsqrt(jnp.mean(x * x, axis=-1, keepdims=True) + RMS_EPS) * g


FFN_TM = 1024
FFN_TH = 256


def _ffn_kernel(x_ref, gpre_ref, wg_ref, wu_ref, wo_ref, gpost_ref, o_ref, h_scr, acc_scr):
    j = pl.program_id(1)

    @pl.when(j == 0)
    def _():
        h_scr[...] = _rms(x_ref[...], gpre_ref[...]).astype(BF16)
        acc_scr[...] = jnp.zeros_like(acc_scr)

    h = h_scr[...]
    gate = jnp.dot(h, wg_ref[...], preferred_element_type=F32)
    up = jnp.dot(h, wu_ref[...], preferred_element_type=F32)
    act = (gate * jax.nn.sigmoid(gate) * up).astype(BF16)
    acc_scr[...] += jnp.dot(act, wo_ref[...], preferred_element_type=F32)

    @pl.when(j == pl.num_programs(1) - 1)
    def _():
        o_ref[...] = x_ref[...] + 0.5 * _rms(acc_scr[...], gpost_ref[...])


def _ffn(x, g_pre, w_in, w_out, g_post):
    n = x.shape[0]
    nh = FFN_HIDDEN // FFN_TH
    return pl.pallas_call(
        _ffn_kernel,
        grid=(n // FFN_TM, nh),
        in_specs=[
            pl.BlockSpec((FFN_TM, D_MODEL), lambda i, j: (i, 0)),
            pl.BlockSpec((1, D_MODEL), lambda i, j: (0, 0)),
            pl.BlockSpec((D_MODEL, FFN_TH), lambda i, j: (0, j)),
            pl.BlockSpec((D_MODEL, FFN_TH), lambda i, j: (0, j + nh)),
            pl.BlockSpec((FFN_TH, D_MODEL), lambda i, j: (j, 0)),
            pl.BlockSpec((1, D_MODEL), lambda i, j: (0, 0)),
        ],
        out_specs=pl.BlockSpec((FFN_TM, D_MODEL), lambda i, j: (i, 0)),
        out_shape=jax.ShapeDtypeStruct((n, D_MODEL), F32),
        scratch_shapes=[pltpu.VMEM((FFN_TM, D_MODEL), BF16), pltpu.VMEM((FFN_TM, D_MODEL), F32)],
        compiler_params=_cparams(("parallel", "arbitrary"), 48),
        name="ffn",
    )(x, g_pre, w_in, w_in, w_out, g_post)


PROJ_TM = 512


def _mix_proj_kernel(x_ref, g_ref, w_ref, cs_ref, pa_ref, pb_ref, pc_ref):
    h = _rms(x_ref[...], g_ref[...]).astype(BF16)
    p = jnp.dot(h, w_ref[...], preferred_element_type=F32) * cs_ref[...]
    wa = 3 * NA_WIDTH
    wb = 3 * HY_WIDTH
    pa_ref[...] = p[:, :wa]
    pb_ref[...] = p[:, wa:wa + wb]
    pc_ref[...] = p[:, wa + wb:]


def _mix_proj(x, g, w, colscale):
    n = x.shape[0]
    wa, wb, wc = 3 * NA_WIDTH, 3 * HY_WIDTH, 3 * DIL_WIDTH
    ncol = wa + wb + wc
    return pl.pallas_call(
        _mix_proj_kernel,
        grid=(n // PROJ_TM,),
        in_specs=[
            pl.BlockSpec((PROJ_TM, D_MODEL), lambda i: (i, 0)),
            pl.BlockSpec((1, D_MODEL), lambda i: (0, 0)),
            pl.BlockSpec((D_MODEL, ncol), lambda i: (0, 0)),
            pl.BlockSpec((1, ncol), lambda i: (0, 0)),
        ],
        out_specs=[
            pl.BlockSpec((PROJ_TM, wa), lambda i: (i, 0)),
            pl.BlockSpec((PROJ_TM, wb), lambda i: (i, 0)),
            pl.BlockSpec((PROJ_TM, wc), lambda i: (i, 0)),
        ],
        out_shape=[
            jax.ShapeDtypeStruct((n, wa), F32),
            jax.ShapeDtypeStruct((n, wb), F32),
            jax.ShapeDtypeStruct((n, wc), F32),
        ],
        compiler_params=_cparams(("parallel",), 48),
        name="mix_proj",
    )(x, g, w, colscale)


def _norm_mm_kernel(x_ref, g_ref, w_ref, o_ref):
    h = _rms(x_ref[...], g_ref[...]).astype(BF16)
    o_ref[...] = jnp.dot(h, w_ref[...], preferred_element_type=F32)


def _norm_mm(x, g, w, tm):
    n, ncol = x.shape[0], w.shape[1]
    return pl.pallas_call(
        _norm_mm_kernel,
        grid=(n // tm,),
        in_specs=[
            pl.BlockSpec((tm, D_MODEL), lambda i: (i, 0)),
            pl.BlockSpec((1, D_MODEL), lambda i: (0, 0)),
            pl.BlockSpec((D_MODEL, ncol), lambda i: (0, 0)),
        ],
        out_specs=pl.BlockSpec((tm, ncol), lambda i: (i, 0)),
        out_shape=jax.ShapeDtypeStruct((n, ncol), F32),
        compiler_params=_cparams(("parallel",), 32),
        name="norm_mm",
    )(x, g, w)


def _pair_attention(q, k, v, bias_ref_fn):
    lane = lax.broadcasted_iota(jnp.int32, (1, LANES), 1)
    res = []
    for hh in range(2):
        keep = (lane < HEAD_DIM) if hh == 0 else (lane >= HEAD_DIM)
        qh = jnp.where(keep, q, 0.0).astype(BF16)
        s = lax.dot_general(qh, k, (((1,), (1,)), ((), ())), preferred_element_type=F32)
        s = s + bias_ref_fn(hh)
        m = jnp.max(s, axis=-1, keepdims=True)
        p = jnp.exp(s - m)
        l = jnp.sum(p, axis=-1, keepdims=True)
        o = jnp.dot(p.astype(BF16), v, preferred_element_type=F32)
        res.append((o, m, l))
    return res, lane


def _na_fill_bias(tz_ref, bias_scr, j):
    nrows = SEQ // GRID_W
    start_row = int(np.clip(j * NA_QROWS - NA_ROWS // 2, 0, nrows - NA_KROWS))
    neg = jnp.full((GRID_W, GRID_W), NEG_INF, F32)
    for hh in range(2):
        for rl in range(NA_QROWS):
            r = j * NA_QROWS + rl
            rs = int(np.clip(r - NA_ROWS // 2, 0, nrows - NA_ROWS))
            for kp in range(NA_KROWS // 2):
                tiles = []
                for kl in (2 * kp, 2 * kp + 1):
                    kr = start_row + kl
                    tiles.append(tz_ref[hh, kr - r + NA_ROWS - 1] if rs <= kr < rs + NA_ROWS else neg)
                bias_scr[hh, rl * GRID_W:(rl + 1) * GRID_W, kp * LANES:(kp + 1) * LANES] = jnp.concatenate(
                    tiles, axis=1)


def _na_kernel(q_ref, k_ref, v_ref, tz_ref, o_ref, bias_scr):
    j = pl.program_id(2)
    nblk = SEQ // NA_BQ
    for case_j in (0, 1, nblk - 1):
        pl.when(j == case_j)(functools.partial(_na_fill_bias, tz_ref, bias_scr, case_j))

    nrows = SEQ // GRID_W
    start_row = jnp.clip(j * NA_QROWS - NA_ROWS // 2, 0, nrows - NA_KROWS)
    start = pl.multiple_of(start_row * GRID_W, GRID_W)
    k = k_ref[0, pl.ds(start, NA_BK), :].astype(BF16)
    v = v_ref[0, pl.ds(start, NA_BK), :].astype(BF16)
    res, lane = _pair_attention(q_ref[0], k, v, lambda hh: bias_scr[hh])
    (o0, _, l0), (o1, _, l1) = res
    o_ref[0] = jnp.where(lane < HEAD_DIM, o0 / l0, o1 / l1)


def _na_attention(pa, tz):
    b = pa.shape[0]
    npair = NA_WIDTH // LANES
    nblk = SEQ // NA_BQ
    return pl.pallas_call(
        _na_kernel,
        grid=(b, npair, nblk),
        in_specs=[
            pl.BlockSpec((1, NA_BQ, LANES), lambda bi, p, j: (bi, j, p)),
            pl.BlockSpec((1, SEQ, LANES), lambda bi, p, j: (bi, 0, npair + p)),
            pl.BlockSpec((1, SEQ, LANES), lambda bi, p, j: (bi, 0, 2 * npair + p)),
            pl.BlockSpec((2,) + tz.shape[1:], lambda bi, p, j: (p, 0, 0, 0)),
        ],
        out_specs=pl.BlockSpec((1, NA_BQ, LANES), lambda bi, p, j: (bi, j, p)),
        out_shape=jax.ShapeDtypeStruct((b, SEQ, NA_WIDTH), F32),
        scratch_shapes=[pltpu.VMEM((2, NA_BQ, NA_BK), F32)],
        compiler_params=_cparams(("parallel", "parallel", "arbitrary"), 48),
        name="na_attn",
    )(pa, pa, pa, tz)


def _na_tz(rpb):
    col = np.arange(GRID_W)
    cs = np.clip(col - NA_COLS // 2, 0, GRID_W - NA_COLS)
    col_ok = (col[None, :] >= cs[:, None]) & (col[None, :] < cs[:, None] + NA_COLS)
    period = GRID_W + 1
    half = NA_COLS - 1
    filler = jnp.zeros(rpb.shape[:2] + (period - (2 * half + 1),), rpb.dtype)
    v = jnp.concatenate([rpb[..., half:], filler, rpb[..., :half]], axis=-1)
    t = jnp.tile(v, (1, 1, GRID_W))[..., :GRID_W * GRID_W].reshape(rpb.shape[:2] + (GRID_W, GRID_W))
    return jnp.where(col_ok[None, None], t, NEG_INF)


def _dil_kernel(q_ref, k_ref, v_ref, bias_ref, o_ref, m_scr, l_scr):
    lane = lax.broadcasted_iota(jnp.int32, (1, LANES), 1)
    kcol = lax.broadcasted_iota(jnp.int32, (1, DIL_BK), 1)

    for c, (_, dil) in enumerate(DIL_CONFIGS):
        m_len = SEQ // dil
        nblk = m_len // DIL_BQ

        def body(it, carry, c=c, dil=dil, m_len=m_len, nblk=nblk):
            g = it // nblk
            n = it % nblk
            pos = n * DIL_BQ

            def rows(p0, size):
                return pl.ds(g + p0 * dil, size, stride=dil) if dil > 1 else pl.ds(p0, size)

            lo = jnp.maximum(pos - DIL_HALF, 0)
            hi = jnp.minimum(pos + DIL_BQ, m_len - DIL_HALF)
            qrows = rows(pos, DIL_BQ)
            k = jnp.concatenate(
                [k_ref[0, rows(lo, DIL_HALF), :], k_ref[0, qrows, :], k_ref[0, rows(hi, DIL_HALF), :]],
                axis=0).astype(BF16)
            v = jnp.concatenate(
                [v_ref[0, rows(lo, DIL_HALF), :], v_ref[0, qrows, :], v_ref[0, rows(hi, DIL_HALF), :]],
                axis=0).astype(BF16)
            edge = jnp.where(((kcol < DIL_HALF) & (n == 0))
                             | ((kcol >= DIL_HALF + DIL_BQ) & (n == nblk - 1)), NEG_INF, 0.0)
            res, _ = _pair_attention(q_ref[0, qrows, :], k, v, lambda hh: bias_ref[c, 0, hh] + edge)
            (o0, m0, l0), (o1, m1, l1) = res
            o = jnp.where(lane < HEAD_DIM, o0, o1)
            m = jnp.where(lane < HEAD_DIM, m0, m1)
            l = jnp.where(lane < HEAD_DIM, l0, l1)
            if c == 0:
                o_ref[0, qrows, :] = o
                m_scr[qrows, :] = m
                l_scr[qrows, :] = l
            else:
                m_old = m_scr[qrows, :]
                m_new = jnp.maximum(m_old, m)
                a_old = jnp.exp(m_old - m_new)
                a_new = jnp.exp(m - m_new)
                o_ref[0, qrows, :] = o_ref[0, qrows, :] * a_old + o * a_new
                l_scr[qrows, :] = l_scr[qrows, :] * a_old + l * a_new
                m_scr[qrows, :] = m_new
            return carry

        lax.fori_loop(0, SEQ // DIL_BQ, body, 0)

    def finish(i, carry):
        r = pl.ds(pl.multiple_of(i * 512, 512), 512)
        o_ref[0, r, :] = o_ref[0, r, :] / l_scr[r, :]
        return carry

    lax.fori_loop(0, SEQ // 512, finish, 0)


def _dil_attention(pc, bias):
    b = pc.shape[0]
    npair = DIL_WIDTH // LANES
    return pl.pallas_call(
        _dil_kernel,
        grid=(b, npair),
        in_specs=[
            pl.BlockSpec((1, SEQ, LANES), lambda bi, p: (bi, 0, p)),
            pl.BlockSpec((1, SEQ, LANES), lambda bi, p: (bi, 0, npair + p)),
            pl.BlockSpec((1, SEQ, LANES), lambda bi, p: (bi, 0, 2 * npair + p)),
            pl.BlockSpec((len(DIL_CONFIGS), 1, 2, DIL_BQ, DIL_BK), lambda bi, p: (0, p, 0, 0, 0)),
        ],
        out_specs=pl.BlockSpec((1, SEQ, LANES), lambda bi, p: (bi, 0, p)),
        out_shape=jax.ShapeDtypeStruct((b, SEQ, DIL_WIDTH), F32),
        scratch_shapes=[pltpu.VMEM((SEQ, LANES), F32), pltpu.VMEM((SEQ, LANES), F32)],
        compiler_params=_cparams(("parallel", "arbitrary"), 56),
        name="dil_attn",
    )(pc, pc, pc, bias)


def _t5_bucket(rel):
    half = T5_BUCKETS // 2
    exact = half // 2
    n = np.abs(rel)
    far = exact + (np.log(np.maximum(n, 1) / exact) / math.log(T5_MAX_DIST / exact) * (half - exact)).astype(np.int32)
    far = np.minimum(far, half - 1)
    return (np.where(rel > 0, half, 0) + np.where(n < exact, n, far)).astype(np.int32)


def _dil_bias(t5_table):
    rel = np.arange(-DIL_HALF, DIL_HALF + 1)
    period = DIL_BK + 1
    nh = t5_table.shape[1]
    out = []
    for _, dil in DIL_CONFIGS:
        vals = t5_table[_t5_bucket(dil * rel)].T
        v = jnp.concatenate([vals, jnp.full((nh, period - vals.shape[1]), NEG_INF, vals.dtype)], axis=1)
        out.append(jnp.tile(v, (1, DIL_BQ))[:, :DIL_BQ * DIL_BK].reshape(nh, DIL_BQ, DIL_BK))
    return jnp.stack(out).reshape(len(DIL_CONFIGS), nh // 2, 2, DIL_BQ, DIL_BK)


CONV_TM = 1024


def _conv_gate_kernel(p_ref, prev_ref, next_ref, w_ref, b_ref, z_ref, x0_ref):
    i = pl.program_id(1)
    p = p_ref[0]
    tm = p.shape[0]
    first = jnp.where(i > 0, prev_ref[0, 7:8, :], 0.0)
    last = jnp.where(i < pl.num_programs(1) - 1, next_ref[0, 0:1, :], 0.0)
    row = lax.broadcasted_iota(jnp.int32, (tm, 1), 0)
    up = jnp.where(row == 0, first, pltpu.roll(p, 1, axis=0))
    dn = jnp.where(row == tm - 1, last, pltpu.roll(p, tm - 1, axis=0))
    uc = up * w_ref[0:1, :] + p * w_ref[1:2, :] + dn * w_ref[2:3, :] + b_ref[...]
    z_ref[0] = uc[:, HY_WIDTH:2 * HY_WIDTH] * uc[:, :HY_WIDTH]
    x0_ref[0] = uc[:, 2 * HY_WIDTH:]


def _conv_gate(pb, w, bvec):
    b = pb.shape[0]
    wb = 3 * HY_WIDTH
    nt = SEQ // CONV_TM
    r8 = CONV_TM // 8
    return pl.pallas_call(
        _conv_gate_kernel,
        grid=(b, nt),
        in_specs=[
            pl.BlockSpec((1, CONV_TM, wb), lambda bi, i: (bi, i, 0)),
            pl.BlockSpec((1, 8, wb), lambda bi, i: (bi, jnp.maximum(i * r8 - 1, 0), 0)),
            pl.BlockSpec((1, 8, wb), lambda bi, i: (bi, jnp.minimum((i + 1) * r8, SEQ // 8 - 1), 0)),
            pl.BlockSpec((3, wb), lambda bi, i: (0, 0)),
            pl.BlockSpec((1, wb), lambda bi, i: (0, 0)),
        ],
        out_specs=[
            pl.BlockSpec((1, CONV_TM, HY_WIDTH), lambda bi, i: (bi, i, 0)),
            pl.BlockSpec((1, CONV_TM, HY_WIDTH), lambda bi, i: (bi, i, 0)),
        ],
        out_shape=[jax.ShapeDtypeStruct((b, SEQ, HY_WIDTH), F32)] * 2,
        compiler_params=_cparams(("parallel", "parallel"), 48),
        name="hy_conv_gate",
    )(pb, pb, pb, w, bvec)


FILT_TM = 1024
FILT_KPAD = 128


def _filter_kernel(z_ref, w1_ref, b1_ref, w2_ref, b2_ref, w3_ref, b3_ref, w4_ref, fr_ref, dec_ref, h_ref):
    hp = lax.Precision.HIGHEST
    z = z_ref[...]
    fr = fr_ref[...]
    h = jnp.sin(fr * (jnp.dot(z, w1_ref[...], precision=hp, preferred_element_type=F32) + b1_ref[...]))
    h = jnp.sin(fr * (jnp.dot(h, w2_ref[...], precision=hp, preferred_element_type=F32) + b2_ref[...]))
    h = jnp.sin(fr * (jnp.dot(h, w3_ref[...], precision=hp, preferred_element_type=F32) + b3_ref[...]))
    h = jnp.dot(h, w4_ref[...], precision=hp, preferred_element_type=F32)
    h = h * jnp.exp(-z[:, 0:1] * jnp.abs(dec_ref[...]))
    row = pl.program_id(0) * FILT_TM + lax.broadcasted_iota(jnp.int32, (FILT_TM, 1), 0)
    col = lax.broadcasted_iota(jnp.int32, (1, 2 * HY_WIDTH), 1)
    h_ref[0] = jnp.where((row == 0) & (col >= HY_WIDTH), 0.0, h)


def _filter_features():
    t = jnp.linspace(0.0, 1.0, SEQ, dtype=F32)[:, None]
    bands = jnp.linspace(1e-4, HY_BANDS - 1, HY_BANDS, dtype=F32)[None, :]
    ang = (2.0 * math.pi / SEQ) * jnp.arange(SEQ, dtype=F32)[:, None] * bands
    z = jnp.concatenate([t, jnp.cos(ang), -jnp.sin(ang)], axis=-1)
    return jnp.pad(z, ((0, 0), (0, FILT_KPAD - z.shape[1])))


def _hyena_filters(zfeat, w1, b1, w2, b2, w3, b3, w4, freq, decay):
    fw = w2.shape[0]
    w1p = jnp.pad(w1, ((0, FILT_KPAD - w1.shape[0]), (0, 0)))
    full = lambda a: pl.BlockSpec(a.shape, lambda i: (0,) * a.ndim)
    args = (w1p, b1[None], w2, b2[None], w3, b3[None], w4, freq[None], decay[None])
    del fw
    return pl.pallas_call(
        _filter_kernel,
        grid=(SEQ // FILT_TM,),
        in_specs=[pl.BlockSpec((FILT_TM, FILT_KPAD), lambda i: (i, 0))] + [full(a) for a in args],
        out_specs=pl.BlockSpec((1, FILT_TM, 2 * HY_WIDTH), lambda i: (0, i, 0)),
        out_shape=jax.ShapeDtypeStruct((1, SEQ, 2 * HY_WIDTH), F32),
        compiler_params=_cparams(("parallel",), 32),
        name="hy_filter",
    )(zfeat, *args)


def _dft_tables():
    a = np.arange(DFT_R)
    w = np.exp(-2j * np.pi * np.outer(a, a) / DFT_R)
    tw = np.exp(-2j * np.pi * np.outer(a, a) / DFT_N)
    f32 = lambda x: jnp.asarray(x.astype(np.float32))
    f1 = np.concatenate([w.real[:, :DFT_HALF], w.imag[:, :DFT_HALF]], axis=0)
    m1 = np.concatenate([w.real[:DFT_HALF], w.imag[:DFT_HALF]], axis=1) / DFT_N
    wr, wi = f32(w.real)[None], f32(w.imag)[None]
    tr, ti = f32(tw.real.T)[:, None, :], f32(tw.imag.T)[:, None, :]
    cr, ci = wr * tr - wi * ti, wr * ti + wi * tr
    blocks = lambda a, b, c, d: jnp.concatenate(
        [jnp.concatenate([a, b], axis=2), jnp.concatenate([c, d], axis=2)], axis=1).astype(BF16)
    qk = blocks(cr, -ci, ci, cr)
    crt, cit = jnp.swapaxes(cr, 1, 2), jnp.swapaxes(ci, 1, 2)
    gk = blocks(crt, cit, -cit, crt)
    return f32(f1).astype(BF16), qk, gk, f32(m1).astype(BF16)


DFT_KB = 8


def _dft_stage1(x_ref, f1_ref, s_scr):
    def body(i, carry):
        n2b = i * DFT_KB
        xs = jnp.concatenate([x_ref[0, pl.ds(n2b + u, DFT_HALF, stride=DFT_R), :] for u in range(DFT_KB)],
                             axis=1).astype(BF16)
        a = jnp.dot(f1_ref[...], xs, preferred_element_type=F32)
        for u in range(DFT_KB):
            s_scr[pl.ds(pl.multiple_of((n2b + u) * DFT_PITCH, 8), 2 * DFT_R), :] = a[:, u * LANES:(u + 1) * LANES]
        return carry

    lax.fori_loop(0, DFT_R // DFT_KB, body, 0)


def _dft_stage2_rows(s_scr, k1):
    re = s_scr[pl.ds(k1, DFT_R, stride=DFT_PITCH), :]
    im = s_scr[pl.ds(DFT_R + k1, DFT_R, stride=DFT_PITCH), :]
    return jnp.concatenate([re, im], axis=0).astype(BF16)


def _dft_fwd_kernel(x_ref, f1_ref, qk_ref, o_ref, s_scr):
    step = pl.program_id(2)
    pl.when(step == 0)(functools.partial(_dft_stage1, x_ref, f1_ref, s_scr))
    for u in range(DFT_KB):
        a = _dft_stage2_rows(s_scr, step * DFT_KB + u)
        o_ref[0, u] = jnp.dot(qk_ref[u], a, preferred_element_type=F32)


def _dft_fwd(x, f1, qk):
    nb, _, c = x.shape
    return pl.pallas_call(
        _dft_fwd_kernel,
        grid=(nb, c // LANES, DFT_R // DFT_KB),
        in_specs=[
            pl.BlockSpec((1, SEQ, LANES), lambda b, ct, s: (b, 0, ct)),
            pl.BlockSpec(f1.shape, lambda b, ct, s: (0, 0)),
            pl.BlockSpec((DFT_KB,) + qk.shape[1:], lambda b, ct, s: (s, 0, 0)),
        ],
        out_specs=pl.BlockSpec((1, DFT_KB, 2 * DFT_R, LANES), lambda b, ct, s: (b, s, 0, ct)),
        out_shape=jax.ShapeDtypeStruct((nb, DFT_R, 2 * DFT_R, c), F32),
        scratch_shapes=[pltpu.VMEM((DFT_R * DFT_PITCH, LANES), F32)],
        compiler_params=_cparams(("parallel", "parallel", "arbitrary"), 40),
        name="hy_dft_fwd",
    )(x, f1, qk)


def _dft_stage1_inv(s_scr, m1_ref, o_ref):
    def body(i, carry):
        n2b = i * DFT_KB
        bc = jnp.concatenate(
            [s_scr[pl.ds(pl.multiple_of((n2b + u) * DFT_PITCH, 8), 2 * DFT_R), :] for u in range(DFT_KB)],
            axis=1).astype(BF16)
        y = jnp.dot(m1_ref[...], bc, preferred_element_type=F32)
        for u in range(DFT_KB):
            o_ref[0, pl.ds(n2b + u, DFT_HALF, stride=DFT_R), :] = y[:, u * LANES:(u + 1) * LANES]
        return carry

    lax.fori_loop(0, DFT_R // DFT_KB, body, 0)


def _dft_conv_kernel(x_ref, f1_ref, qk_ref, hf_ref, hb_ref, gk_ref, m1_ref, o_ref, s_scr):
    step = pl.program_id(2)
    pl.when(step == 0)(functools.partial(_dft_stage1, x_ref, f1_ref, s_scr))
    for u in range(DFT_KB):
        k1 = step * DFT_KB + u
        xk = jnp.dot(qk_ref[u], _dft_stage2_rows(s_scr, k1), preferred_element_type=F32)
        xr, xi = xk[:DFT_R], xk[DFT_R:]
        hr = hf_ref[0, u, :DFT_R, :] + hb_ref[0, u, :DFT_R, :]
        hi = hf_ref[0, u, DFT_R:, :] - hb_ref[0, u, DFT_R:, :]
        y = jnp.concatenate([xr * hr - xi * hi, xr * hi + xi * hr], axis=0).astype(BF16)
        b = jnp.dot(gk_ref[u], y, preferred_element_type=F32)
        s_scr[pl.ds(k1, DFT_R, stride=DFT_PITCH), :] = b[:DFT_R]
        s_scr[pl.ds(DFT_R + k1, DFT_R, stride=DFT_PITCH), :] = b[DFT_R:]
    pl.when(step == pl.num_programs(2) - 1)(functools.partial(_dft_stage1_inv, s_scr, m1_ref, o_ref))


def _dft_conv(x, hs, f1, qk, gk, m1):
    b, _, c = x.shape
    nct = c // LANES
    kblock = lambda a: pl.BlockSpec((DFT_KB,) + a.shape[1:], lambda bi, ct, s: (s, 0, 0))
    return pl.pallas_call(
        _dft_conv_kernel,
        grid=(b, nct, DFT_R // DFT_KB),
        in_specs=[
            pl.BlockSpec((1, SEQ, LANES), lambda bi, ct, s: (bi, 0, ct)),
            pl.BlockSpec(f1.shape, lambda bi, ct, s: (0, 0)),
            kblock(qk),
            pl.BlockSpec((1, DFT_KB, 2 * DFT_R, LANES), lambda bi, ct, s: (0, s, 0, ct)),
            pl.BlockSpec((1, DFT_KB, 2 * DFT_R, LANES), lambda bi, ct, s: (0, s, 0, nct + ct)),
            kblock(gk),
            pl.BlockSpec(m1.shape, lambda bi, ct, s: (0, 0)),
        ],
        out_specs=pl.BlockSpec((1, SEQ, LANES), lambda bi, ct, s: (bi, 0, ct)),
        out_shape=jax.ShapeDtypeStruct((b, SEQ, c), F32),
        scratch_shapes=[pltpu.VMEM((DFT_R * DFT_PITCH, LANES), F32)],
        compiler_params=_cparams(("parallel", "parallel", "arbitrary"), 48),
        name="hy_dft_conv",
    )(x, f1, qk, hs, hs, gk, m1)


MIXOUT_TM = 512


def _mix_out_kernel(x_ref, ya_ref, yh_ref, z_ref, x0_ref, yc_ref, skip_ref, gg_ref, w_ref, gpost_ref, o_ref):
    a0, b0 = NA_WIDTH, NA_WIDTH + HY_WIDTH
    yb = x0_ref[...] * (yh_ref[...] + skip_ref[...] * z_ref[...])
    acc = jnp.dot(_rms(ya_ref[...], gg_ref[:, :a0]).astype(BF16), w_ref[:a0, :], preferred_element_type=F32)
    acc += jnp.dot(_rms(yb, gg_ref[:, a0:b0]).astype(BF16), w_ref[a0:b0, :], preferred_element_type=F32)
    acc += jnp.dot(_rms(yc_ref[...], gg_ref[:, b0:]).astype(BF16), w_ref[b0:, :], preferred_element_type=F32)
    o_ref[...] = x_ref[...] + _rms(acc, gpost_ref[...])


def _mix_out(x, ya, yh, z, x0, yc, skip, gg, w, gpost):
    n = x.shape[0]
    tok = lambda width: pl.BlockSpec((MIXOUT_TM, width), lambda i: (i, 0))
    full = lambda a: pl.BlockSpec(a.shape, lambda i: (0,) * a.ndim)
    return pl.pallas_call(
        _mix_out_kernel,
        grid=(n // MIXOUT_TM,),
        in_specs=[tok(D_MODEL), tok(NA_WIDTH), tok(HY_WIDTH), tok(HY_WIDTH), tok(HY_WIDTH), tok(DIL_WIDTH),
                  full(skip), full(gg), full(w), full(gpost)],
        out_specs=tok(D_MODEL),
        out_shape=jax.ShapeDtypeStruct((n, D_MODEL), F32),
        compiler_params=_cparams(("parallel",), 40),
        name="mix_out",
    )(x, ya, yh, z, x0, yc, skip, gg, w, gpost)


XA_TM = 512


def _xattn_kernel(x_ref, gpre_ref, wq_ref, kv_ref, wo_ref, gpost_ref, o_ref):
    x = x_ref[...]
    h = _rms(x, gpre_ref[...]).astype(BF16)
    q = jnp.dot(h, wq_ref[...], preferred_element_type=F32) * (XA_HEAD_DIM ** -0.5)
    outs = []
    for hd in range(XA_HEADS):
        c0 = hd * XA_HEAD_DIM
        qh = q[:, c0:c0 + XA_HEAD_DIM].astype(BF16)
        kh = kv_ref[0, :, c0:c0 + XA_HEAD_DIM].astype(BF16)
        vh = kv_ref[0, :, D_MODEL + c0:D_MODEL + c0 + XA_HEAD_DIM].astype(BF16)
        s = lax.dot_general(qh, kh, (((1,), (1,)), ((), ())), preferred_element_type=F32)
        p = jnp.exp(s - jnp.max(s, axis=-1, keepdims=True))
        l = jnp.sum(p, axis=-1, keepdims=True)
        outs.append(jnp.dot(p.astype(BF16), vh, preferred_element_type=F32) / l)
    o = jnp.concatenate(outs, axis=-1).astype(BF16)
    xa = jnp.dot(o, wo_ref[...], preferred_element_type=F32)
    o_ref[...] = x + _rms(xa, gpost_ref[...])


def _xattn(x, gpre, wq, kv, wo, gpost):
    n = x.shape[0]
    per_batch = SEQ // XA_TM
    full = lambda a: pl.BlockSpec(a.shape, lambda i: (0,) * a.ndim)
    return pl.pallas_call(
        _xattn_kernel,
        grid=(n // XA_TM,),
        in_specs=[
            pl.BlockSpec((XA_TM, D_MODEL), lambda i: (i, 0)),
            full(gpre), full(wq),
            pl.BlockSpec((1,) + kv.shape[1:], lambda i: (i // per_batch, 0, 0)),
            full(wo), full(gpost),
        ],
        out_specs=pl.BlockSpec((XA_TM, D_MODEL), lambda i: (i, 0)),
        out_shape=jax.ShapeDtypeStruct((n, D_MODEL), F32),
        compiler_params=_cparams(("parallel",), 40),
        name="xattn",
    )(x, gpre, wq, kv, wo, gpost)


def kernel(x, mem, norm_g, w_in, w_out, na_rpb, t5_table, hy_conv_w, hy_conv_b, hy_f_w1, hy_f_b1, hy_f_w2, hy_f_b2, hy_f_w3, hy_f_b3, hy_f_w4, hy_freq, hy_decay, hy_skip, xa_wq, xa_wkv, xa_wo, ffn_w_in, ffn_w_out):
    bsz, seq, d = x.shape
    assert (seq, d) == (SEQ, D_MODEL)
    depth = norm_g.shape[0]
    mem_len = mem.shape[1]
    n = bsz * seq
    scale = HEAD_DIM ** -0.5
    wa, wb, wc = 3 * NA_WIDTH, 3 * HY_WIDTH, 3 * DIL_WIDTH
    colscale = np.ones((1, wa + wb + wc), np.float32)
    colscale[:, :NA_WIDTH] = scale
    colscale[:, wa + wb:wa + wb + DIL_WIDTH] = scale
    colscale = jnp.asarray(colscale)

    zfeat = _filter_features()
    f1, qk, gk, m1 = _dft_tables()
    dil_bias = _dil_bias(t5_table)

    xf = x.reshape(n, d)
    memf = mem.reshape(bsz * mem_len, d)
    for l in range(depth):
        g = norm_g[l][:, None, :]
        xf = _ffn(xf, g[0], ffn_w_in[l, 0].astype(BF16), ffn_w_out[l, 0].astype(BF16), g[1])

        pa, pb, pc = _mix_proj(xf, g[2], w_in[l].astype(BF16), colscale)
        pa = pa.reshape(bsz, seq, wa)
        pb = pb.reshape(bsz, seq, wb)
        pc = pc.reshape(bsz, seq, wc)
        ya = _na_attention(pa, _na_tz(na_rpb[l]))
        z, x0 = _conv_gate(pb, hy_conv_w[l], hy_conv_b[l][None])
        hfilt = _hyena_filters(zfeat, hy_f_w1[l], hy_f_b1[l], hy_f_w2[l], hy_f_b2[l], hy_f_w3[l], hy_f_b3[l],
                               hy_f_w4[l], hy_freq[l], hy_decay[l])
        yh = _dft_conv(z, _dft_fwd(hfilt, f1, qk), f1, qk, gk, m1)
        yc = _dil_attention(pc, dil_bias)
        xf = _mix_out(xf, ya.reshape(n, NA_WIDTH), yh.reshape(n, HY_WIDTH), z.reshape(n, HY_WIDTH),
                      x0.reshape(n, HY_WIDTH), yc.reshape(n, DIL_WIDTH), hy_skip[l][None], g[3],
                      w_out[l].astype(BF16), g[4])

        kv = _norm_mm(memf, g[6], xa_wkv[l].astype(BF16), mem_len).reshape(bsz, mem_len, 2 * D_MODEL)
        xf = _xattn(xf, g[5], xa_wq[l].astype(BF16), kv, xa_wo[l].astype(BF16), g[7])

        xf = _ffn(xf, g[8], ffn_w_in[l, 1].astype(BF16), ffn_w_out[l, 1].astype(BF16), g[9])
    return xf.reshape(bsz, seq, d)
```

```python
import functools
import math

import numpy as np
import jax
import jax.numpy as jnp
from jax import lax
from jax.experimental import pallas as pl
from jax.experimental.pallas import tpu as pltpu

F32 = jnp.float32
BF16 = jnp.bfloat16

D_MODEL = 1024
SEQ = 8192
GRID_W = 64
HEAD_DIM = 64
NA_WIDTH = 384
HY_WIDTH = 256
DIL_WIDTH = 384
NA_ROWS = 8
NA_COLS = 16
HY_BANDS = 16
DIL_CONFIGS = ((128, 1), (512, 4), (2048, 16))
T5_BUCKETS = 32
T5_MAX_DIST = 1024
XA_HEADS = 4
XA_HEAD_DIM = 256
FFN_HIDDEN = 2816
RMS_EPS = 1e-6
NEG_INF = -1e30

LANES = 128
MIB = 1024 * 1024

NA_QROWS = 8
NA_KROWS = 16
NA_BQ = NA_QROWS * GRID_W
NA_BK = NA_KROWS * GRID_W
DIL_HALF = 64
DIL_BQ = 256
DIL_BK = DIL_BQ + 2 * DIL_HALF
DFT_R = 128
DFT_HALF = 64
DFT_N = DFT_R * DFT_R
DFT_PITCH = 264


def _cparams(sem, vmem_mib):
    return pltpu.CompilerParams(dimension_semantics=sem, vmem_limit_bytes=vmem_mib * MIB)


def _rms(x, g):
    return x * lax.rsqrt(jnp.mean(x * x, axis=-1, keepdims=True) + RMS_EPS) * g


FFN_TM = 1024
FFN_TH = 256


def _ffn_kernel(x_ref, gpre_ref, wg_ref, wu_ref, wo_ref, gpost_ref, o_ref, h_scr, acc_scr):
    j = pl.program_id(1)

    @pl.when(j == 0)
    def _():
        h_scr[...] = _rms(x_ref[...], gpre_ref[...]).astype(BF16)
        acc_scr[...] = jnp.zeros_like(acc_scr)

    h = h_scr[...]
    gate = jnp.dot(h, wg_ref[...], preferred_element_type=F32)
    up = jnp.dot(h, wu_ref[...], preferred_element_type=F32)
    act = (gate * jax.nn.sigmoid(gate) * up).astype(BF16)
    acc_scr[...] += jnp.dot(act, wo_ref[...], preferred_element_type=F32)

    @pl.when(j == pl.num_programs(1) - 1)
    def _():
        o_ref[...] = x_ref[...] + 0.5 * _rms(acc_scr[...], gpost_ref[...])


def _ffn(x, g_pre, w_in, w_out, g_post):
    n = x.shape[0]
    nh = FFN_HIDDEN // FFN_TH
    return pl.pallas_call(
        _ffn_kernel,
        grid=(n // FFN_TM, nh),
        in_specs=[
            pl.BlockSpec((FFN_TM, D_MODEL), lambda i, j: (i, 0)),
            pl.BlockSpec((1, D_MODEL), lambda i, j: (0, 0)),
            pl.BlockSpec((D_MODEL, FFN_TH), lambda i, j: (0, j)),
            pl.BlockSpec((D_MODEL, FFN_TH), lambda i, j: (0, j + nh)),
            pl.BlockSpec((FFN_TH, D_MODEL), lambda i, j: (j, 0)),
            pl.BlockSpec((1, D_MODEL), lambda i, j: (0, 0)),
        ],
        out_specs=pl.BlockSpec((FFN_TM, D_MODEL), lambda i, j: (i, 0)),
        out_shape=jax.ShapeDtypeStruct((n, D_MODEL), F32),
        scratch_shapes=[pltpu.VMEM((FFN_TM, D_MODEL), BF16), pltpu.VMEM((FFN_TM, D_MODEL), F32)],
        compiler_params=_cparams(("parallel", "arbitrary"), 48),
        name="ffn",
    )(x, g_pre, w_in, w_in, w_out, g_post)


PROJ_TM = 512


def _mix_proj_kernel(x_ref, g_ref, w_ref, cs_ref, pa_ref, pb_ref, pc_ref):
    h = _rms(x_ref[...], g_ref[...]).astype(BF16)
    p = jnp.dot(h, w_ref[...], preferred_element_type=F32) * cs_ref[...]
    wa = 3 * NA_WIDTH
    wb = 3 * HY_WIDTH
    pa_ref[...] = p[:, :wa]
    pb_ref[...] = p[:, wa:wa + wb]
    pc_ref[...] = p[:, wa + wb:]


def _mix_proj(x, g, w, colscale):
    n = x.shape[0]
    wa, wb, wc = 3 * NA_WIDTH, 3 * HY_WIDTH, 3 * DIL_WIDTH
    ncol = wa + wb + wc
    return pl.pallas_call(
        _mix_proj_kernel,
        grid=(n // PROJ_TM,),
        in_specs=[
            pl.BlockSpec((PROJ_TM, D_MODEL), lambda i: (i, 0)),
            pl.BlockSpec((1, D_MODEL), lambda i: (0, 0)),
            pl.BlockSpec((D_MODEL, ncol), lambda i: (0, 0)),
            pl.BlockSpec((1, ncol), lambda i: (0, 0)),
        ],
        out_specs=[
            pl.BlockSpec((PROJ_TM, wa), lambda i: (i, 0)),
            pl.BlockSpec((PROJ_TM, wb), lambda i: (i, 0)),
            pl.BlockSpec((PROJ_TM, wc), lambda i: (i, 0)),
        ],
        out_shape=[
            jax.ShapeDtypeStruct((n, wa), F32),
            jax.ShapeDtypeStruct((n, wb), F32),
            jax.ShapeDtypeStruct((n, wc), F32),
        ],
        compiler_params=_cparams(("parallel",), 48),
        name="mix_proj",
    )(x, g, w, colscale)


def _norm_mm_kernel(x_ref, g_ref, w_ref, o_ref):
    h = _rms(x_ref[...], g_ref[...]).astype(BF16)
    o_ref[...] = jnp.dot(h, w_ref[...], preferred_element_type=F32)


def _norm_mm(x, g, w, tm):
    n, ncol = x.shape[0], w.shape[1]
    return pl.pallas_call(
        _norm_mm_kernel,
        grid=(n // tm,),
        in_specs=[
            pl.BlockSpec((tm, D_MODEL), lambda i: (i, 0)),
            pl.BlockSpec((1, D_MODEL), lambda i: (0, 0)),
            pl.BlockSpec((D_MODEL, ncol), lambda i: (0, 0)),
        ],
        out_specs=pl.BlockSpec((tm, ncol), lambda i: (i, 0)),
        out_shape=jax.ShapeDtypeStruct((n, ncol), F32),
        compiler_params=_cparams(("parallel",), 32),
        name="norm_mm",
    )(x, g, w)


def _softmax_pv(q, k, v, bias):
    s = lax.dot_general(q, k, (((1,), (1,)), ((), ())), preferred_element_type=F32) + bias
    m = jnp.max(s, axis=-1, keepdims=True)
    p = jnp.exp(s - m)
    l = jnp.sum(p, axis=-1, keepdims=True)
    return jnp.dot(p.astype(BF16), v, preferred_element_type=F32), m, l


def _pair_attention(q, k, v, bias, stack_heads):
    bq = q.shape[0]
    lane = lax.broadcasted_iota(jnp.int32, (1, LANES), 1)
    first = lane < HEAD_DIM
    q0 = jnp.where(first, q, 0.0).astype(BF16)
    q1 = jnp.where(first, 0.0, q).astype(BF16)
    if stack_heads:
        res = _softmax_pv(jnp.concatenate([q0, q1], axis=0), k, v, bias)
        return tuple(jnp.where(first, a[:bq], a[bq:]) for a in res)
    res0 = _softmax_pv(q0, k, v, bias[:bq])
    res1 = _softmax_pv(q1, k, v, bias[bq:])
    return tuple(jnp.where(first, a, b) for a, b in zip(res0, res1))


def _na_fill_bias(tz_ref, bias_scr, j):
    nrows = SEQ // GRID_W
    start_row = int(np.clip(j * NA_QROWS - NA_ROWS // 2, 0, nrows - NA_KROWS))
    neg = jnp.full((GRID_W, GRID_W), NEG_INF, F32)
    for hh in range(2):
        for rl in range(NA_QROWS):
            r = j * NA_QROWS + rl
            rs = int(np.clip(r - NA_ROWS // 2, 0, nrows - NA_ROWS))
            for kp in range(NA_KROWS // 2):
                tiles = []
                for kl in (2 * kp, 2 * kp + 1):
                    kr = start_row + kl
                    tiles.append(tz_ref[hh, kr - r + NA_ROWS - 1] if rs <= kr < rs + NA_ROWS else neg)
                r0 = hh * NA_BQ + rl * GRID_W
                bias_scr[r0:r0 + GRID_W, kp * LANES:(kp + 1) * LANES] = jnp.concatenate(tiles, axis=1)


def _na_kernel(q_ref, k_ref, v_ref, tz_ref, o_ref, bias_scr):
    j = pl.program_id(2)
    nblk = SEQ // NA_BQ
    for case_j in (0, 1, nblk - 1):
        pl.when(j == case_j)(functools.partial(_na_fill_bias, tz_ref, bias_scr, case_j))

    nrows = SEQ // GRID_W
    start_row = jnp.clip(j * NA_QROWS - NA_ROWS // 2, 0, nrows - NA_KROWS)
    start = pl.multiple_of(start_row * GRID_W, GRID_W)
    k = k_ref[0, pl.ds(start, NA_BK), :].astype(BF16)
    v = v_ref[0, pl.ds(start, NA_BK), :].astype(BF16)
    o, _, l = _pair_attention(q_ref[0], k, v, bias_scr, stack_heads=False)
    o_ref[0] = o / l


def _na_attention(pa, tz):
    b = pa.shape[0]
    npair = NA_WIDTH // LANES
    nblk = SEQ // NA_BQ
    return pl.pallas_call(
        _na_kernel,
        grid=(b, npair, nblk),
        in_specs=[
            pl.BlockSpec((1, NA_BQ, LANES), lambda bi, p, j: (bi, j, p)),
            pl.BlockSpec((1, SEQ, LANES), lambda bi, p, j: (bi, 0, npair + p)),
            pl.BlockSpec((1, SEQ, LANES), lambda bi, p, j: (bi, 0, 2 * npair + p)),
            pl.BlockSpec((2,) + tz.shape[1:], lambda bi, p, j: (p, 0, 0, 0)),
        ],
        out_specs=pl.BlockSpec((1, NA_BQ, LANES), lambda bi, p, j: (bi, j, p)),
        out_shape=jax.ShapeDtypeStruct((b, SEQ, NA_WIDTH), F32),
        scratch_shapes=[pltpu.VMEM((2 * NA_BQ, NA_BK), F32)],
        compiler_params=_cparams(("parallel", "parallel", "arbitrary"), 48),
        name="na_attn",
    )(pa, pa, pa, tz)


def _na_tz(rpb):
    col = np.arange(GRID_W)
    cs = np.clip(col - NA_COLS // 2, 0, GRID_W - NA_COLS)
    col_ok = (col[None, :] >= cs[:, None]) & (col[None, :] < cs[:, None] + NA_COLS)
    period = GRID_W + 1
    half = NA_COLS - 1
    filler = jnp.zeros(rpb.shape[:2] + (period - (2 * half + 1),), rpb.dtype)
    v = jnp.concatenate([rpb[..., half:], filler, rpb[..., :half]], axis=-1)
    t = jnp.tile(v, (1, 1, GRID_W))[..., :GRID_W * GRID_W].reshape(rpb.shape[:2] + (GRID_W, GRID_W))
    return jnp.where(col_ok[None, None], t, NEG_INF)


def _dil_kernel(q_ref, k_ref, v_ref, bias_ref, o_ref, m_scr, l_scr):
    kcol = lax.broadcasted_iota(jnp.int32, (1, DIL_BK), 1)

    for c, (_, dil) in enumerate(DIL_CONFIGS):
        m_len = SEQ // dil
        nblk = m_len // DIL_BQ

        def body(it, carry, c=c, dil=dil, m_len=m_len, nblk=nblk):
            g = it // nblk
            n = it % nblk
            pos = n * DIL_BQ

            def rows(p0, size):
                return pl.ds(g + p0 * dil, size, stride=dil) if dil > 1 else pl.ds(p0, size)

            lo = jnp.maximum(pos - DIL_HALF, 0)
            hi = jnp.minimum(pos + DIL_BQ, m_len - DIL_HALF)
            qrows = rows(pos, DIL_BQ)
            k = jnp.concatenate(
                [k_ref[0, rows(lo, DIL_HALF), :], k_ref[0, qrows, :], k_ref[0, rows(hi, DIL_HALF), :]],
                axis=0).astype(BF16)
            v = jnp.concatenate(
                [v_ref[0, rows(lo, DIL_HALF), :], v_ref[0, qrows, :], v_ref[0, rows(hi, DIL_HALF), :]],
                axis=0).astype(BF16)
            edge = jnp.where(((kcol < DIL_HALF) & (n == 0))
                             | ((kcol >= DIL_HALF + DIL_BQ) & (n == nblk - 1)), NEG_INF, 0.0)
            bias = bias_ref[c, 0].reshape(2 * DIL_BQ, DIL_BK) + edge
            o, m, l = _pair_attention(q_ref[0, qrows, :], k, v, bias, stack_heads=True)
            if c == 0:
                o_ref[0, qrows, :] = o
                m_scr[qrows, :] = m
                l_scr[qrows, :] = l
            else:
                m_old = m_scr[qrows, :]
                m_new = jnp.maximum(m_old, m)
                a_old = jnp.exp(m_old - m_new)
                a_new = jnp.exp(m - m_new)
                o_ref[0, qrows, :] = o_ref[0, qrows, :] * a_old + o * a_new
                l_scr[qrows, :] = l_scr[qrows, :] * a_old + l * a_new
                m_scr[qrows, :] = m_new
            return carry

        lax.fori_loop(0, SEQ // DIL_BQ, body, 0, unroll=2)

    def finish(i, carry):
        r = pl.ds(pl.multiple_of(i * 512, 512), 512)
        o_ref[0, r, :] = o_ref[0, r, :] / l_scr[r, :]
        return carry

    lax.fori_loop(0, SEQ // 512, finish, 0)


def _dil_attention(pc, bias):
    b = pc.shape[0]
    npair = DIL_WIDTH // LANES
    return pl.pallas_call(
        _dil_kernel,
        grid=(b, npair),
        in_specs=[
            pl.BlockSpec((1, SEQ, LANES), lambda bi, p: (bi, 0, p)),
            pl.BlockSpec((1, SEQ, LANES), lambda bi, p: (bi, 0, npair + p)),
            pl.BlockSpec((1, SEQ, LANES), lambda bi, p: (bi, 0, 2 * npair + p)),
            pl.BlockSpec((len(DIL_CONFIGS), 1, 2, DIL_BQ, DIL_BK), lambda bi, p: (0, p, 0, 0, 0)),
        ],
        out_specs=pl.BlockSpec((1, SEQ, LANES), lambda bi, p: (bi, 0, p)),
        out_shape=jax.ShapeDtypeStruct((b, SEQ, DIL_WIDTH), F32),
        scratch_shapes=[pltpu.VMEM((SEQ, LANES), F32), pltpu.VMEM((SEQ, LANES), F32)],
        compiler_params=_cparams(("parallel", "arbitrary"), 56),
        name="dil_attn",
    )(pc, pc, pc, bias)


def _t5_bucket(rel):
    half = T5_BUCKETS // 2
    exact = half // 2
    n = np.abs(rel)
    far = exact + (np.log(np.maximum(n, 1) / exact) / math.log(T5_MAX_DIST / exact) * (half - exact)).astype(np.int32)
    far = np.minimum(far, half - 1)
    return (np.where(rel > 0, half, 0) + np.where(n < exact, n, far)).astype(np.int32)


def _dil_bias(t5_table):
    rel = np.arange(-DIL_HALF, DIL_HALF + 1)
    period = DIL_BK + 1
    nh = t5_table.shape[1]
    out = []
    for _, dil in DIL_CONFIGS:
        vals = t5_table[_t5_bucket(dil * rel)].T
        v = jnp.concatenate([vals, jnp.full((nh, period - vals.shape[1]), NEG_INF, vals.dtype)], axis=1)
        out.append(jnp.tile(v, (1, DIL_BQ))[:, :DIL_BQ * DIL_BK].reshape(nh, DIL_BQ, DIL_BK))
    return jnp.stack(out).reshape(len(DIL_CONFIGS), nh // 2, 2, DIL_BQ, DIL_BK)


CONV_TM = 1024


def _conv_gate_kernel(p_ref, prev_ref, next_ref, w_ref, b_ref, z_ref, x0_ref):
    i = pl.program_id(1)
    p = p_ref[0]
    tm = p.shape[0]
    first = jnp.where(i > 0, prev_ref[0, 7:8, :], 0.0)
    last = jnp.where(i < pl.num_programs(1) - 1, next_ref[0, 0:1, :], 0.0)
    row = lax.broadcasted_iota(jnp.int32, (tm, 1), 0)
    up = jnp.where(row == 0, first, pltpu.roll(p, 1, axis=0))
    dn = jnp.where(row == tm - 1, last, pltpu.roll(p, tm - 1, axis=0))
    uc = up * w_ref[0:1, :] + p * w_ref[1:2, :] + dn * w_ref[2:3, :] + b_ref[...]
    z_ref[0] = uc[:, HY_WIDTH:2 * HY_WIDTH] * uc[:, :HY_WIDTH]
    x0_ref[0] = uc[:, 2 * HY_WIDTH:]


def _conv_gate(pb, w, bvec):
    b = pb.shape[0]
    wb = 3 * HY_WIDTH
    nt = SEQ // CONV_TM
    r8 = CONV_TM // 8
    return pl.pallas_call(
        _conv_gate_kernel,
        grid=(b, nt),
        in_specs=[
            pl.BlockSpec((1, CONV_TM, wb), lambda bi, i: (bi, i, 0)),
            pl.BlockSpec((1, 8, wb), lambda bi, i: (bi, jnp.maximum(i * r8 - 1, 0), 0)),
            pl.BlockSpec((1, 8, wb), lambda bi, i: (bi, jnp.minimum((i + 1) * r8, SEQ // 8 - 1), 0)),
            pl.BlockSpec((3, wb), lambda bi, i: (0, 0)),
            pl.BlockSpec((1, wb), lambda bi, i: (0, 0)),
        ],
        out_specs=[
            pl.BlockSpec((1, CONV_TM, HY_WIDTH), lambda bi, i: (bi, i, 0)),
            pl.BlockSpec((1, CONV_TM, HY_WIDTH), lambda bi, i: (bi, i, 0)),
        ],
        out_shape=[jax.ShapeDtypeStruct((b, SEQ, HY_WIDTH), F32)] * 2,
        compiler_params=_cparams(("parallel", "parallel"), 48),
        name="hy_conv_gate",
    )(pb, pb, pb, w, bvec)


FILT_TM = 1024
FILT_KPAD = 128
FILT_HALF = SEQ // 2


def _filter_kernel(z_ref, w1_ref, b1_ref, w2_ref, b2_ref, w3_ref, b3_ref, w4_ref, fr_ref, dec_ref, h_ref):
    hp = lax.Precision.HIGHEST
    z = z_ref[...]
    fr = fr_ref[...]
    h = jnp.sin(fr * (jnp.dot(z, w1_ref[...], precision=hp, preferred_element_type=F32) + b1_ref[...]))
    h = jnp.sin(fr * (jnp.dot(h, w2_ref[...], precision=hp, preferred_element_type=F32) + b2_ref[...]))
    h = jnp.sin(fr * (jnp.dot(h, w3_ref[...], precision=hp, preferred_element_type=F32) + b3_ref[...]))
    h = jnp.dot(h, w4_ref[...], precision=hp, preferred_element_type=F32)
    wout = 2 * HY_WIDTH
    dec = jnp.abs(dec_ref[...])
    lo = h[:, :wout] * jnp.exp(-z[:, 0:1] * dec)
    hi = h[:, wout:] * jnp.exp(-z[:, FILT_KPAD:FILT_KPAD + 1] * dec)
    row = pl.program_id(0) * FILT_TM + lax.broadcasted_iota(jnp.int32, (FILT_TM, 1), 0)
    col = lax.broadcasted_iota(jnp.int32, (1, wout), 1)
    h_ref[0] = jnp.where((row == 0) & (col >= HY_WIDTH), 0.0, lo)
    h_ref[1] = hi


def _filter_features():
    t = jnp.linspace(0.0, 1.0, SEQ, dtype=F32)[:, None]
    bands = jnp.linspace(1e-4, HY_BANDS - 1, HY_BANDS, dtype=F32)[None, :]
    ang = (2.0 * math.pi / SEQ) * jnp.arange(SEQ, dtype=F32)[:, None] * bands
    z = jnp.concatenate([t, jnp.cos(ang), -jnp.sin(ang)], axis=-1)
    z = jnp.pad(z, ((0, 0), (0, FILT_KPAD - z.shape[1])))
    return jnp.concatenate([z[:FILT_HALF], z[FILT_HALF:]], axis=1)


def _hyena_filters(zfeat, w1, b1, w2, b2, w3, b3, w4, freq, decay):
    def diag2(w):
        zero = jnp.zeros_like(w)
        return jnp.concatenate([jnp.concatenate([w, zero], axis=1), jnp.concatenate([zero, w], axis=1)], axis=0)

    twice = lambda v: jnp.concatenate([v, v])[None]
    w1p = jnp.pad(w1, ((0, FILT_KPAD - w1.shape[0]), (0, 0)))
    full = lambda a: pl.BlockSpec(a.shape, lambda i: (0,) * a.ndim)
    args = (diag2(w1p), twice(b1), diag2(w2), twice(b2), diag2(w3), twice(b3), diag2(w4), twice(freq), decay[None])
    out = pl.pallas_call(
        _filter_kernel,
        grid=(FILT_HALF // FILT_TM,),
        in_specs=[pl.BlockSpec((FILT_TM, 2 * FILT_KPAD), lambda i: (i, 0))] + [full(a) for a in args],
        out_specs=pl.BlockSpec((2, FILT_TM, 2 * HY_WIDTH), lambda i: (0, i, 0)),
        out_shape=jax.ShapeDtypeStruct((2, FILT_HALF, 2 * HY_WIDTH), F32),
        compiler_params=_cparams(("parallel",), 32),
        name="hy_filter",
    )(zfeat, *args)
    return out.reshape(1, SEQ, 2 * HY_WIDTH)


def _dft_tables():
    a = np.arange(DFT_R)
    w = np.exp(-2j * np.pi * np.outer(a, a) / DFT_R)
    tw = np.exp(-2j * np.pi * np.outer(a, a) / DFT_N)
    f32 = lambda x: jnp.asarray(x.astype(np.float32))
    f1 = np.concatenate([w.real[:, :DFT_HALF], w.imag[:, :DFT_HALF]], axis=0)
    m1 = np.concatenate([w.real[:DFT_HALF], w.imag[:DFT_HALF]], axis=1) / DFT_N
    c = w[None] * tw[:, :DFT_KB].T[:, None, :]
    blocks = lambda a, b, c, d: np.concatenate(
        [np.concatenate([a, b], axis=2), np.concatenate([c, d], axis=2)], axis=1)
    qu = blocks(c.real, -c.imag, c.imag, c.real)
    ct = np.swapaxes(c, 1, 2)
    gu = blocks(ct.real, ct.imag, -ct.imag, ct.real)
    ta = np.exp(-2j * np.pi * DFT_KB * np.outer(np.arange(DFT_R // DFT_KB), a) / DFT_N)
    ta = np.concatenate([ta.real, ta.imag], axis=1)[:, :, None] * np.ones((1, 1, LANES))
    bf = lambda x: f32(x).astype(BF16)
    return bf(f1), bf(qu), bf(gu), bf(m1), f32(ta)


DFT_KB = 8


def _dft_stage1(x_ref, f1_ref, s_scr):
    def body(i, carry):
        n2b = i * DFT_KB
        xs = jnp.concatenate([x_ref[0, pl.ds(n2b + u, DFT_HALF, stride=DFT_R), :] for u in range(DFT_KB)],
                             axis=1).astype(BF16)
        a = jnp.dot(f1_ref[...], xs, preferred_element_type=F32)
        for u in range(DFT_KB):
            s_scr[pl.ds(pl.multiple_of((n2b + u) * DFT_PITCH, 8), 2 * DFT_R), :] = a[:, u * LANES:(u + 1) * LANES]
        return carry

    lax.fori_loop(0, DFT_R // DFT_KB, body, 0)


def _dft_stage2_rows(s_scr, ta_ref, step, u):
    k1 = step * DFT_KB + u
    re = s_scr[pl.ds(k1, DFT_R, stride=DFT_PITCH), :]
    im = s_scr[pl.ds(DFT_R + k1, DFT_R, stride=DFT_PITCH), :]
    tr, ti = ta_ref[step, :DFT_R, :], ta_ref[step, DFT_R:, :]
    return jnp.concatenate([re * tr - im * ti, re * ti + im * tr], axis=0).astype(BF16)


def _dft_fwd_kernel(x_ref, f1_ref, qu_ref, ta_ref, o_ref, s_scr):
    step = pl.program_id(2)
    pl.when(step == 0)(functools.partial(_dft_stage1, x_ref, f1_ref, s_scr))
    for u in range(DFT_KB):
        a = _dft_stage2_rows(s_scr, ta_ref, step, u)
        o_ref[0, u] = jnp.dot(qu_ref[u], a, preferred_element_type=F32).astype(o_ref.dtype)


def _dft_fwd(x, f1, qu, ta, out_dtype):
    nb, _, c = x.shape
    full = lambda a: pl.BlockSpec(a.shape, lambda b, ct, s: (0,) * a.ndim)
    return pl.pallas_call(
        _dft_fwd_kernel,
        grid=(nb, c // LANES, DFT_R // DFT_KB),
        in_specs=[pl.BlockSpec((1, SEQ, LANES), lambda b, ct, s: (b, 0, ct)), full(f1), full(qu), full(ta)],
        out_specs=pl.BlockSpec((1, DFT_KB, 2 * DFT_R, LANES), lambda b, ct, s: (b, s, 0, ct)),
        out_shape=jax.ShapeDtypeStruct((nb, DFT_R, 2 * DFT_R, c), out_dtype),
        scratch_shapes=[pltpu.VMEM((DFT_R * DFT_PITCH, LANES), F32)],
        compiler_params=_cparams(("parallel", "parallel", "arbitrary"), 40),
        name="hy_dft_fwd",
    )(x, f1, qu, ta)


def _dft_stage1_inv(s_scr, m1_ref, o_ref):
    def body(i, carry):
        n2b = i * DFT_KB
        bc = jnp.concatenate(
            [s_scr[pl.ds(pl.multiple_of((n2b + u) * DFT_PITCH, 8), 2 * DFT_R), :] for u in range(DFT_KB)],
            axis=1).astype(BF16)
        y = jnp.dot(m1_ref[...], bc, preferred_element_type=F32)
        for u in range(DFT_KB):
            o_ref[0, pl.ds(n2b + u, DFT_HALF, stride=DFT_R), :] = y[:, u * LANES:(u + 1) * LANES]
        return carry

    lax.fori_loop(0, DFT_R // DFT_KB, body, 0)


def _dft_conv_kernel(x_ref, f1_ref, qu_ref, ta_ref, hf_ref, hb_ref, gu_ref, m1_ref, o_ref, s_scr):
    step = pl.program_id(2)
    pl.when(step == 0)(functools.partial(_dft_stage1, x_ref, f1_ref, s_scr))
    tr, ti = ta_ref[step, :DFT_R, :], ta_ref[step, DFT_R:, :]
    for u in range(DFT_KB):
        k1 = step * DFT_KB + u
        xk = jnp.dot(qu_ref[u], _dft_stage2_rows(s_scr, ta_ref, step, u), preferred_element_type=F32)
        xr, xi = xk[:DFT_R], xk[DFT_R:]
        hr = hf_ref[0, u, :DFT_R, :].astype(F32) + hb_ref[0, u, :DFT_R, :].astype(F32)
        hi = hf_ref[0, u, DFT_R:, :].astype(F32) - hb_ref[0, u, DFT_R:, :].astype(F32)
        y = jnp.concatenate([xr * hr - xi * hi, xr * hi + xi * hr], axis=0).astype(BF16)
        b = jnp.dot(gu_ref[u], y, preferred_element_type=F32)
        br, bi = b[:DFT_R], b[DFT_R:]
        s_scr[pl.ds(k1, DFT_R, stride=DFT_PITCH), :] = br * tr + bi * ti
        s_scr[pl.ds(DFT_R + k1, DFT_R, stride=DFT_PITCH), :] = bi * tr - br * ti
    pl.when(step == pl.num_programs(2) - 1)(functools.partial(_dft_stage1_inv, s_scr, m1_ref, o_ref))


def _dft_conv(x, hs, f1, qu, gu, m1, ta):
    b, _, c = x.shape
    nct = c // LANES
    full = lambda a: pl.BlockSpec(a.shape, lambda bi, ct, s: (0,) * a.ndim)
    return pl.pallas_call(
        _dft_conv_kernel,
        grid=(b, nct, DFT_R // DFT_KB),
        in_specs=[
            pl.BlockSpec((1, SEQ, LANES), lambda bi, ct, s: (bi, 0, ct)),
            full(f1), full(qu), full(ta),
            pl.BlockSpec((1, DFT_KB, 2 * DFT_R, LANES), lambda bi, ct, s: (0, s, 0, ct)),
            pl.BlockSpec((1, DFT_KB, 2 * DFT_R, LANES), lambda bi, ct, s: (0, s, 0, nct + ct)),
            full(gu), full(m1),
        ],
        out_specs=pl.BlockSpec((1, SEQ, LANES), lambda bi, ct, s: (bi, 0, ct)),
        out_shape=jax.ShapeDtypeStruct((b, SEQ, c), F32),
        scratch_shapes=[pltpu.VMEM((DFT_R * DFT_PITCH, LANES), F32)],
        compiler_params=_cparams(("parallel", "parallel", "arbitrary"), 48),
        name="hy_dft_conv",
    )(x, f1, qu, ta, hs, hs, gu, m1)


MIXOUT_TM = 512


def _mix_out_kernel(x_ref, ya_ref, yh_ref, z_ref, x0_ref, yc_ref, skip_ref, gg_ref, w_ref, gpost_ref, o_ref):
    a0, b0 = NA_WIDTH, NA_WIDTH + HY_WIDTH
    yb = x0_ref[...] * (yh_ref[...] + skip_ref[...] * z_ref[...])
    acc = jnp.dot(_rms(ya_ref[...], gg_ref[:, :a0]).astype(BF16), w_ref[:a0, :], preferred_element_type=F32)
    acc += jnp.dot(_rms(yb, gg_ref[:, a0:b0]).astype(BF16), w_ref[a0:b0, :], preferred_element_type=F32)
    acc += jnp.dot(_rms(yc_ref[...], gg_ref[:, b0:]).astype(BF16), w_ref[b0:, :], preferred_element_type=F32)
    o_ref[...] = x_ref[...] + _rms(acc, gpost_ref[...])


def _mix_out(x, ya, yh, z, x0, yc, skip, gg, w, gpost):
    n = x.shape[0]
    tok = lambda width: pl.BlockSpec((MIXOUT_TM, width), lambda i: (i, 0))
    full = lambda a: pl.BlockSpec(a.shape, lambda i: (0,) * a.ndim)
    return pl.pallas_call(
        _mix_out_kernel,
        grid=(n // MIXOUT_TM,),
        in_specs=[tok(D_MODEL), tok(NA_WIDTH), tok(HY_WIDTH), tok(HY_WIDTH), tok(HY_WIDTH), tok(DIL_WIDTH),
                  full(skip), full(gg), full(w), full(gpost)],
        out_specs=tok(D_MODEL),
        out_shape=jax.ShapeDtypeStruct((n, D_MODEL), F32),
        compiler_params=_cparams(("parallel",), 40),
        name="mix_out",
    )(x, ya, yh, z, x0, yc, skip, gg, w, gpost)


XA_TM = 512


def _xattn_kernel(x_ref, gpre_ref, wq_ref, kv_ref, wo_ref, gpost_ref, o_ref):
    x = x_ref[...]
    h = _rms(x, gpre_ref[...]).astype(BF16)
    q = jnp.dot(h, wq_ref[...], preferred_element_type=F32) * (XA_HEAD_DIM ** -0.5)
    outs = []
    for hd in range(XA_HEADS):
        c0 = hd * XA_HEAD_DIM
        qh = q[:, c0:c0 + XA_HEAD_DIM].astype(BF16)
        kh = kv_ref[0, :, c0:c0 + XA_HEAD_DIM].astype(BF16)
        vh = kv_ref[0, :, D_MODEL + c0:D_MODEL + c0 + XA_HEAD_DIM].astype(BF16)
        s = lax.dot_general(qh, kh, (((1,), (1,)), ((), ())), preferred_element_type=F32)
        p = jnp.exp(s - jnp.max(s, axis=-1, keepdims=True))
        l = jnp.sum(p, axis=-1, keepdims=True)
        outs.append(jnp.dot(p.astype(BF16), vh, preferred_element_type=F32) / l)
    o = jnp.concatenate(outs, axis=-1).astype(BF16)
    xa = jnp.dot(o, wo_ref[...], preferred_element_type=F32)
    o_ref[...] = x + _rms(xa, gpost_ref[...])


def _xattn(x, gpre, wq, kv, wo, gpost):
    n = x.shape[0]
    per_batch = SEQ // XA_TM
    full = lambda a: pl.BlockSpec(a.shape, lambda i: (0,) * a.ndim)
    return pl.pallas_call(
        _xattn_kernel,
        grid=(n // XA_TM,),
        in_specs=[
            pl.BlockSpec((XA_TM, D_MODEL), lambda i: (i, 0)),
            full(gpre), full(wq),
            pl.BlockSpec((1,) + kv.shape[1:], lambda i: (i // per_batch, 0, 0)),
            full(wo), full(gpost),
        ],
        out_specs=pl.BlockSpec((XA_TM, D_MODEL), lambda i: (i, 0)),
        out_shape=jax.ShapeDtypeStruct((n, D_MODEL), F32),
        compiler_params=_cparams(("parallel",), 40),
        name="xattn",
    )(x, gpre, wq, kv, wo, gpost)


def kernel(x, mem, norm_g, w_in, w_out, na_rpb, t5_table, hy_conv_w, hy_conv_b, hy_f_w1, hy_f_b1, hy_f_w2, hy_f_b2, hy_f_w3, hy_f_b3, hy_f_w4, hy_freq, hy_decay, hy_skip, xa_wq, xa_wkv, xa_wo, ffn_w_in, ffn_w_out):
    bsz, seq, d = x.shape
    assert (seq, d) == (SEQ, D_MODEL)
    depth = norm_g.shape[0]
    mem_len = mem.shape[1]
    n = bsz * seq
    scale = HEAD_DIM ** -0.5
    wa, wb, wc = 3 * NA_WIDTH, 3 * HY_WIDTH, 3 * DIL_WIDTH
    colscale = np.ones((1, wa + wb + wc), np.float32)
    colscale[:, :NA_WIDTH] = scale
    colscale[:, wa + wb:wa + wb + DIL_WIDTH] = scale
    colscale = jnp.asarray(colscale)

    zfeat = _filter_features()
    f1, qu, gu, m1, ta = _dft_tables()
    dil_bias = _dil_bias(t5_table)

    xf = x.reshape(n, d)
    memf = mem.reshape(bsz * mem_len, d)
    for l in range(depth):
        g = norm_g[l][:, None, :]
        xf = _ffn(xf, g[0], ffn_w_in[l, 0].astype(BF16), ffn_w_out[l, 0].astype(BF16), g[1])

        pa, pb, pc = _mix_proj(xf, g[2], w_in[l].astype(BF16), colscale)
        pa = pa.reshape(bsz, seq, wa)
        pb = pb.reshape(bsz, seq, wb)
        pc = pc.reshape(bsz, seq, wc)
        ya = _na_attention(pa, _na_tz(na_rpb[l]))
        z, x0 = _conv_gate(pb, hy_conv_w[l], hy_conv_b[l][None])
        hfilt = _hyena_filters(zfeat, hy_f_w1[l], hy_f_b1[l], hy_f_w2[l], hy_f_b2[l], hy_f_w3[l], hy_f_b3[l],
                               hy_f_w4[l], hy_freq[l], hy_decay[l])
        yh = _dft_conv(z, _dft_fwd(hfilt, f1, qu, ta, BF16), f1, qu, gu, m1, ta)
        yc = _dil_attention(pc, dil_bias)
        xf = _mix_out(xf, ya.reshape(n, NA_WIDTH), yh.reshape(n, HY_WIDTH), z.reshape(n, HY_WIDTH),
                      x0.reshape(n, HY_WIDTH), yc.reshape(n, DIL_WIDTH), hy_skip[l][None], g[3],
                      w_out[l].astype(BF16), g[4])

        kv = _norm_mm(memf, g[6], xa_wkv[l].astype(BF16), mem_len).reshape(bsz, mem_len, 2 * D_MODEL)
        xf = _xattn(xf, g[5], xa_wq[l].astype(BF16), kv, xa_wo[l].astype(BF16), g[7])

        xf = _ffn(xf, g[8], ffn_w_in[l, 1].astype(BF16), ffn_w_out[l, 1].astype(BF16), g[9])
    return xf.reshape(bsz, seq, d)
```

```python
import functools
import math

import numpy as np
import jax
import jax.numpy as jnp
from jax import lax
from jax.experimental import pallas as pl
from jax.experimental.pallas import tpu as pltpu

F32 = jnp.float32
BF16 = jnp.bfloat16

D_MODEL = 1024
SEQ = 8192
GRID_W = 64
HEAD_DIM = 64
NA_WIDTH = 384
HY_WIDTH = 256
DIL_WIDTH = 384
NA_ROWS = 8
NA_COLS = 16
HY_BANDS = 16
DIL_CONFIGS = ((128, 1), (512, 4), (2048, 16))
T5_BUCKETS = 32
T5_MAX_DIST = 1024
XA_HEADS = 4
XA_HEAD_DIM = 256
FFN_HIDDEN = 2816
RMS_EPS = 1e-6
NEG_INF = -1e30

LANES = 128
MIB = 1024 * 1024

NA_QROWS = 8
NA_KROWS = 16
NA_BQ = NA_QROWS * GRID_W
NA_GROUPS = 4
NA_GQ = GRID_W // NA_GROUPS
NA_GK = 2 * NA_GQ
DIL_HALF = 64
DIL_BQ = 128
DIL_UNROLL = 8
DIL_BK = DIL_BQ + 2 * DIL_HALF
DFT_R = 128
DFT_HALF = 64
DFT_N = DFT_R * DFT_R
DFT_PITCH = 264


def _cparams(sem, vmem_mib):
    return pltpu.CompilerParams(dimension_semantics=sem, vmem_limit_bytes=vmem_mib * MIB)


def _rms(x, g):
    return x * lax.rsqrt(jnp.mean(x * x, axis=-1, keepdims=True) + RMS_EPS) * g


FFN_TM = 1024
FFN_TH = 256


def _ffn_kernel(x_ref, gpre_ref, wg_ref, wu_ref, wo_ref, gpost_ref, o_ref, h_scr, acc_scr):
    j = pl.program_id(1)

    @pl.when(j == 0)
    def _():
        h_scr[...] = _rms(x_ref[...], gpre_ref[...]).astype(BF16)
        acc_scr[...] = jnp.zeros_like(acc_scr)

    h = h_scr[...]
    gate = jnp.dot(h, wg_ref[...], preferred_element_type=F32)
    up = jnp.dot(h, wu_ref[...], preferred_element_type=F32)
    act = (gate * jax.nn.sigmoid(gate) * up).astype(BF16)
    acc_scr[...] += jnp.dot(act, wo_ref[...], preferred_element_type=F32)

    @pl.when(j == pl.num_programs(1) - 1)
    def _():
        o_ref[...] = x_ref[...] + 0.5 * _rms(acc_scr[...], gpost_ref[...])


def _ffn(x, g_pre, w_in, w_out, g_post):
    n = x.shape[0]
    nh = FFN_HIDDEN // FFN_TH
    return pl.pallas_call(
        _ffn_kernel,
        grid=(n // FFN_TM, nh),
        in_specs=[
            pl.BlockSpec((FFN_TM, D_MODEL), lambda i, j: (i, 0)),
            pl.BlockSpec((1, D_MODEL), lambda i, j: (0, 0)),
            pl.BlockSpec((D_MODEL, FFN_TH), lambda i, j: (0, j)),
            pl.BlockSpec((D_MODEL, FFN_TH), lambda i, j: (0, j + nh)),
            pl.BlockSpec((FFN_TH, D_MODEL), lambda i, j: (j, 0)),
            pl.BlockSpec((1, D_MODEL), lambda i, j: (0, 0)),
        ],
        out_specs=pl.BlockSpec((FFN_TM, D_MODEL), lambda i, j: (i, 0)),
        out_shape=jax.ShapeDtypeStruct((n, D_MODEL), F32),
        scratch_shapes=[pltpu.VMEM((FFN_TM, D_MODEL), BF16), pltpu.VMEM((FFN_TM, D_MODEL), F32)],
        compiler_params=_cparams(("parallel", "arbitrary"), 48),
        name="ffn",
    )(x, g_pre, w_in, w_in, w_out, g_post)


PROJ_TM = 512


def _mix_proj_kernel(x_ref, g_ref, w_ref, cs_ref, pa_ref, pb_ref, pc_ref):
    h = _rms(x_ref[...], g_ref[...]).astype(BF16)
    p = jnp.dot(h, w_ref[...], preferred_element_type=F32) * cs_ref[...]
    wa = 3 * NA_WIDTH
    wb = 3 * HY_WIDTH
    pa_ref[...] = p[:, :wa]
    pb_ref[...] = p[:, wa:wa + wb]
    pc_ref[...] = p[:, wa + wb:]


def _mix_proj(x, g, w, colscale):
    n = x.shape[0]
    wa, wb, wc = 3 * NA_WIDTH, 3 * HY_WIDTH, 3 * DIL_WIDTH
    ncol = wa + wb + wc
    return pl.pallas_call(
        _mix_proj_kernel,
        grid=(n // PROJ_TM,),
        in_specs=[
            pl.BlockSpec((PROJ_TM, D_MODEL), lambda i: (i, 0)),
            pl.BlockSpec((1, D_MODEL), lambda i: (0, 0)),
            pl.BlockSpec((D_MODEL, ncol), lambda i: (0, 0)),
            pl.BlockSpec((1, ncol), lambda i: (0, 0)),
        ],
        out_specs=[
            pl.BlockSpec((PROJ_TM, wa), lambda i: (i, 0)),
            pl.BlockSpec((PROJ_TM, wb), lambda i: (i, 0)),
            pl.BlockSpec((PROJ_TM, wc), lambda i: (i, 0)),
        ],
        out_shape=[
            jax.ShapeDtypeStruct((n, wa), F32),
            jax.ShapeDtypeStruct((n, wb), F32),
            jax.ShapeDtypeStruct((n, wc), F32),
        ],
        compiler_params=_cparams(("parallel",), 48),
        name="mix_proj",
    )(x, g, w, colscale)


def _norm_mm_kernel(x_ref, g_ref, w_ref, o_ref):
    h = _rms(x_ref[...], g_ref[...]).astype(BF16)
    o_ref[...] = jnp.dot(h, w_ref[...], preferred_element_type=F32)


def _norm_mm(x, g, w, tm):
    n, ncol = x.shape[0], w.shape[1]
    return pl.pallas_call(
        _norm_mm_kernel,
        grid=(n // tm,),
        in_specs=[
            pl.BlockSpec((tm, D_MODEL), lambda i: (i, 0)),
            pl.BlockSpec((1, D_MODEL), lambda i: (0, 0)),
            pl.BlockSpec((D_MODEL, ncol), lambda i: (0, 0)),
        ],
        out_specs=pl.BlockSpec((tm, ncol), lambda i: (i, 0)),
        out_shape=jax.ShapeDtypeStruct((n, ncol), F32),
        compiler_params=_cparams(("parallel",), 32),
        name="norm_mm",
    )(x, g, w)


def _softmax_pv(q, k, v, bias):
    s = lax.dot_general(q, k, (((1,), (1,)), ((), ())), preferred_element_type=F32) + bias
    m = jnp.max(s, axis=-1, keepdims=True)
    p = jnp.exp(s - m)
    l = jnp.sum(p, axis=-1, keepdims=True)
    return jnp.dot(p.astype(BF16), v, preferred_element_type=F32), m, l


def _pair_attention(q, k, v, bias, stack_heads):
    bq = q.shape[0]
    lane = lax.broadcasted_iota(jnp.int32, (1, LANES), 1)
    first = lane < HEAD_DIM
    q0 = jnp.where(first, q, 0.0).astype(BF16)
    q1 = jnp.where(first, 0.0, q).astype(BF16)
    if stack_heads:
        res = _softmax_pv(jnp.concatenate([q0, q1], axis=0), k, v, bias)
        return tuple(jnp.where(first, a[:bq], a[bq:]) for a in res)
    res0 = _softmax_pv(q0, k, v, bias[:bq])
    res1 = _softmax_pv(q1, k, v, bias[bq:])
    return tuple(jnp.where(first, a, b) for a, b in zip(res0, res1))


def _na_key_col0(g):
    return int(np.clip(g * NA_GQ - NA_COLS // 2, 0, GRID_W - NA_GK))


def _na_fill_bias(tz_ref, bias_scr, j):
    nrows = SEQ // GRID_W
    start_row = int(np.clip(j * NA_QROWS - NA_ROWS // 2, 0, nrows - NA_KROWS))
    neg = jnp.full((NA_GQ, NA_GK), NEG_INF, F32)
    per_tile = LANES // NA_GK
    for g in range(NA_GROUPS):
        for hh in range(2):
            for rl in range(NA_QROWS):
                r = j * NA_QROWS + rl
                rs = int(np.clip(r - NA_ROWS // 2, 0, nrows - NA_ROWS))
                r0 = (hh * NA_QROWS + rl) * NA_GQ
                for kt in range(NA_KROWS // per_tile):
                    tiles = []
                    for kl in range(kt * per_tile, (kt + 1) * per_tile):
                        kr = start_row + kl
                        tiles.append(tz_ref[hh, kr - r + NA_ROWS - 1, g] if rs <= kr < rs + NA_ROWS else neg)
                    bias_scr[g, r0:r0 + NA_GQ, kt * LANES:(kt + 1) * LANES] = jnp.concatenate(tiles, axis=1)


def _na_kernel(q_ref, k_ref, v_ref, tz_ref, o_ref, bias_scr):
    j = pl.program_id(2)
    nblk = SEQ // NA_BQ
    for case_j in (0, 1, nblk - 1):
        pl.when(j == case_j)(functools.partial(_na_fill_bias, tz_ref, bias_scr, case_j))

    nrows = SEQ // GRID_W
    start_row = jnp.clip(j * NA_QROWS - NA_ROWS // 2, 0, nrows - NA_KROWS)
    start = start_row * GRID_W
    for g in range(NA_GROUPS):
        kc0 = _na_key_col0(g)
        krows = [pl.ds(pl.multiple_of(start + kl * GRID_W + kc0, 8), NA_GK) for kl in range(NA_KROWS)]
        qrows = [slice(rl * GRID_W + g * NA_GQ, rl * GRID_W + (g + 1) * NA_GQ) for rl in range(NA_QROWS)]
        k = jnp.concatenate([k_ref[0, r, :] for r in krows], axis=0).astype(BF16)
        v = jnp.concatenate([v_ref[0, r, :] for r in krows], axis=0).astype(BF16)
        q = jnp.concatenate([q_ref[0, r, :] for r in qrows], axis=0)
        o, _, l = _pair_attention(q, k, v, bias_scr[g], stack_heads=True)
        o = o / l
        for rl, r in enumerate(qrows):
            o_ref[0, r, :] = o[rl * NA_GQ:(rl + 1) * NA_GQ]


def _na_attention(pa, tzg):
    b = pa.shape[0]
    npair = NA_WIDTH // LANES
    nblk = SEQ // NA_BQ
    return pl.pallas_call(
        _na_kernel,
        grid=(b, npair, nblk),
        in_specs=[
            pl.BlockSpec((1, NA_BQ, LANES), lambda bi, p, j: (bi, j, p)),
            pl.BlockSpec((1, SEQ, LANES), lambda bi, p, j: (bi, 0, npair + p)),
            pl.BlockSpec((1, SEQ, LANES), lambda bi, p, j: (bi, 0, 2 * npair + p)),
            pl.BlockSpec((2,) + tzg.shape[1:], lambda bi, p, j: (p, 0, 0, 0, 0)),
        ],
        out_specs=pl.BlockSpec((1, NA_BQ, LANES), lambda bi, p, j: (bi, j, p)),
        out_shape=jax.ShapeDtypeStruct((b, SEQ, NA_WIDTH), F32),
        scratch_shapes=[pltpu.VMEM((NA_GROUPS, 2 * NA_QROWS * NA_GQ, NA_KROWS * NA_GK), F32)],
        compiler_params=_cparams(("parallel", "parallel", "arbitrary"), 48),
        name="na_attn",
    )(pa, pa, pa, tzg)


def _na_tz(rpb):
    col = np.arange(GRID_W)
    cs = np.clip(col - NA_COLS // 2, 0, GRID_W - NA_COLS)
    col_ok = (col[None, :] >= cs[:, None]) & (col[None, :] < cs[:, None] + NA_COLS)
    period = GRID_W + 1
    half = NA_COLS - 1
    filler = jnp.zeros(rpb.shape[:2] + (period - (2 * half + 1),), rpb.dtype)
    v = jnp.concatenate([rpb[..., half:], filler, rpb[..., :half]], axis=-1)
    t = jnp.tile(v, (1, 1, GRID_W))[..., :GRID_W * GRID_W].reshape(rpb.shape[:2] + (GRID_W, GRID_W))
    t = jnp.where(col_ok[None, None], t, NEG_INF)
    groups = []
    for g in range(NA_GROUPS):
        kc0 = _na_key_col0(g)
        assert col_ok[g * NA_GQ:(g + 1) * NA_GQ, :kc0].sum() == 0 and col_ok[g * NA_GQ:(g + 1) * NA_GQ, kc0 + NA_GK:].sum() == 0
        groups.append(t[:, :, g * NA_GQ:(g + 1) * NA_GQ, kc0:kc0 + NA_GK])
    return jnp.stack(groups, axis=2)


def _dil_kernel(q_ref, k_ref, v_ref, bias_ref, o_ref, m_scr, l_scr):
    kcol = lax.broadcasted_iota(jnp.int32, (1, DIL_BK), 1)

    order = sorted(range(len(DIL_CONFIGS)), key=lambda c: -DIL_CONFIGS[c][1])
    for c in order:
        dil = DIL_CONFIGS[c][1]
        m_len = SEQ // dil
        nblk = m_len // DIL_BQ

        def body(it, carry, c=c, dil=dil, m_len=m_len, nblk=nblk):
            g = it // nblk
            n = it % nblk
            pos = n * DIL_BQ

            def rows(p0, size):
                return pl.ds(g + p0 * dil, size, stride=dil) if dil > 1 else pl.ds(p0, size)

            lo = jnp.maximum(pos - DIL_HALF, 0)
            hi = jnp.minimum(pos + DIL_BQ, m_len - DIL_HALF)
            qrows = rows(pos, DIL_BQ)
            k = jnp.concatenate(
                [k_ref[0, rows(lo, DIL_HALF), :], k_ref[0, qrows, :], k_ref[0, rows(hi, DIL_HALF), :]],
                axis=0).astype(BF16)
            v = jnp.concatenate(
                [v_ref[0, rows(lo, DIL_HALF), :], v_ref[0, qrows, :], v_ref[0, rows(hi, DIL_HALF), :]],
                axis=0).astype(BF16)
            edge = jnp.where(((kcol < DIL_HALF) & (n == 0))
                             | ((kcol >= DIL_HALF + DIL_BQ) & (n == nblk - 1)), NEG_INF, 0.0)
            bias = bias_ref[c, 0].reshape(2 * DIL_BQ, DIL_BK) + edge
            o, m, l = _pair_attention(q_ref[0, qrows, :], k, v, bias, stack_heads=True)
            if c == order[0]:
                o_ref[0, qrows, :] = o
                m_scr[qrows, :] = m
                l_scr[qrows, :] = l
                return carry
            m_old = m_scr[qrows, :]
            m_new = jnp.maximum(m_old, m)
            a_old = jnp.exp(m_old - m_new)
            a_new = jnp.exp(m - m_new)
            o_new = o_ref[0, qrows, :] * a_old + o * a_new
            l_new = l_scr[qrows, :] * a_old + l * a_new
            if c == order[-1]:
                o_ref[0, qrows, :] = o_new / l_new
            else:
                o_ref[0, qrows, :] = o_new
                l_scr[qrows, :] = l_new
                m_scr[qrows, :] = m_new
            return carry

        lax.fori_loop(0, SEQ // DIL_BQ, body, 0, unroll=DIL_UNROLL)


def _dil_attention(pc, bias):
    b = pc.shape[0]
    npair = DIL_WIDTH // LANES
    return pl.pallas_call(
        _dil_kernel,
        grid=(b, npair),
        in_specs=[
            pl.BlockSpec((1, SEQ, LANES), lambda bi, p: (bi, 0, p)),
            pl.BlockSpec((1, SEQ, LANES), lambda bi, p: (bi, 0, npair + p)),
            pl.BlockSpec((1, SEQ, LANES), lambda bi, p: (bi, 0, 2 * npair + p)),
            pl.BlockSpec((len(DIL_CONFIGS), 1, 2, DIL_BQ, DIL_BK), lambda bi, p: (0, p, 0, 0, 0)),
        ],
        out_specs=pl.BlockSpec((1, SEQ, LANES), lambda bi, p: (bi, 0, p)),
        out_shape=jax.ShapeDtypeStruct((b, SEQ, DIL_WIDTH), F32),
        scratch_shapes=[pltpu.VMEM((SEQ, LANES), F32), pltpu.VMEM((SEQ, LANES), F32)],
        compiler_params=_cparams(("parallel", "arbitrary"), 56),
        name="dil_attn",
    )(pc, pc, pc, bias)


def _t5_bucket(rel):
    half = T5_BUCKETS // 2
    exact = half // 2
    n = np.abs(rel)
    far = exact + (np.log(np.maximum(n, 1) / exact) / math.log(T5_MAX_DIST / exact) * (half - exact)).astype(np.int32)
    far = np.minimum(far, half - 1)
    return (np.where(rel > 0, half, 0) + np.where(n < exact, n, far)).astype(np.int32)


def _dil_bias(t5_table):
    rel = np.arange(-DIL_HALF, DIL_HALF + 1)
    period = DIL_BK + 1
    nh = t5_table.shape[1]
    out = []
    for _, dil in DIL_CONFIGS:
        vals = t5_table[_t5_bucket(dil * rel)].T
        v = jnp.concatenate([vals, jnp.full((nh, period - vals.shape[1]), NEG_INF, vals.dtype)], axis=1)
        out.append(jnp.tile(v, (1, DIL_BQ))[:, :DIL_BQ * DIL_BK].reshape(nh, DIL_BQ, DIL_BK))
    return jnp.stack(out).reshape(len(DIL_CONFIGS), nh // 2, 2, DIL_BQ, DIL_BK)


CONV_TM = 1024


def _conv_gate_kernel(p_ref, prev_ref, next_ref, w_ref, b_ref, z_ref, x0_ref):
    i = pl.program_id(1)
    p = p_ref[0]
    tm = p.shape[0]
    first = jnp.where(i > 0, prev_ref[0, 7:8, :], 0.0)
    last = jnp.where(i < pl.num_programs(1) - 1, next_ref[0, 0:1, :], 0.0)
    row = lax.broadcasted_iota(jnp.int32, (tm, 1), 0)
    up = jnp.where(row == 0, first, pltpu.roll(p, 1, axis=0))
    dn = jnp.where(row == tm - 1, last, pltpu.roll(p, tm - 1, axis=0))
    uc = up * w_ref[0:1, :] + p * w_ref[1:2, :] + dn * w_ref[2:3, :] + b_ref[...]
    z_ref[0] = uc[:, HY_WIDTH:2 * HY_WIDTH] * uc[:, :HY_WIDTH]
    x0_ref[0] = uc[:, 2 * HY_WIDTH:]


def _conv_gate(pb, w, bvec):
    b = pb.shape[0]
    wb = 3 * HY_WIDTH
    nt = SEQ // CONV_TM
    r8 = CONV_TM // 8
    return pl.pallas_call(
        _conv_gate_kernel,
        grid=(b, nt),
        in_specs=[
            pl.BlockSpec((1, CONV_TM, wb), lambda bi, i: (bi, i, 0)),
            pl.BlockSpec((1, 8, wb), lambda bi, i: (bi, jnp.maximum(i * r8 - 1, 0), 0)),
            pl.BlockSpec((1, 8, wb), lambda bi, i: (bi, jnp.minimum((i + 1) * r8, SEQ // 8 - 1), 0)),
            pl.BlockSpec((3, wb), lambda bi, i: (0, 0)),
            pl.BlockSpec((1, wb), lambda bi, i: (0, 0)),
        ],
        out_specs=[
            pl.BlockSpec((1, CONV_TM, HY_WIDTH), lambda bi, i: (bi, i, 0)),
            pl.BlockSpec((1, CONV_TM, HY_WIDTH), lambda bi, i: (bi, i, 0)),
        ],
        out_shape=[jax.ShapeDtypeStruct((b, SEQ, HY_WIDTH), F32)] * 2,
        compiler_params=_cparams(("parallel", "parallel"), 48),
        name="hy_conv_gate",
    )(pb, pb, pb, w, bvec)


FILT_TM = 1024
FILT_KPAD = 128
FILT_HALF = SEQ // 2


def _filter_kernel(z_ref, w1_ref, b1_ref, w2_ref, b2_ref, w3_ref, b3_ref, w4_ref, fr_ref, dec_ref, h_ref):
    hp = lax.Precision.HIGHEST
    z = z_ref[...]
    fr = fr_ref[...]
    h = jnp.sin(fr * (jnp.dot(z, w1_ref[...], precision=hp, preferred_element_type=F32) + b1_ref[...]))
    h = jnp.sin(fr * (jnp.dot(h, w2_ref[...], precision=hp, preferred_element_type=F32) + b2_ref[...]))
    h = jnp.sin(fr * (jnp.dot(h, w3_ref[...], precision=hp, preferred_element_type=F32) + b3_ref[...]))
    h = jnp.dot(h, w4_ref[...], precision=hp, preferred_element_type=F32)
    wout = 2 * HY_WIDTH
    dec = jnp.abs(dec_ref[...])
    lo = h[:, :wout] * jnp.exp(-z[:, 0:1] * dec)
    hi = h[:, wout:] * jnp.exp(-z[:, FILT_KPAD:FILT_KPAD + 1] * dec)
    row = pl.program_id(0) * FILT_TM + lax.broadcasted_iota(jnp.int32, (FILT_TM, 1), 0)
    col = lax.broadcasted_iota(jnp.int32, (1, wout), 1)
    h_ref[0] = jnp.where((row == 0) & (col >= HY_WIDTH), 0.0, lo)
    h_ref[1] = hi


def _filter_features():
    t = jnp.linspace(0.0, 1.0, SEQ, dtype=F32)[:, None]
    bands = jnp.linspace(1e-4, HY_BANDS - 1, HY_BANDS, dtype=F32)[None, :]
    ang = (2.0 * math.pi / SEQ) * jnp.arange(SEQ, dtype=F32)[:, None] * bands
    z = jnp.concatenate([t, jnp.cos(ang), -jnp.sin(ang)], axis=-1)
    z = jnp.pad(z, ((0, 0), (0, FILT_KPAD - z.shape[1])))
    return jnp.concatenate([z[:FILT_HALF], z[FILT_HALF:]], axis=1)


def _hyena_filters(zfeat, w1, b1, w2, b2, w3, b3, w4, freq, decay):
    def diag2(w):
        zero = jnp.zeros_like(w)
        return jnp.concatenate([jnp.concatenate([w, zero], axis=1), jnp.concatenate([zero, w], axis=1)], axis=0)

    twice = lambda v: jnp.concatenate([v, v])[None]
    w1p = jnp.pad(w1, ((0, FILT_KPAD - w1.shape[0]), (0, 0)))
    full = lambda a: pl.BlockSpec(a.shape, lambda i: (0,) * a.ndim)
    args = (diag2(w1p), twice(b1), diag2(w2), twice(b2), diag2(w3), twice(b3), diag2(w4), twice(freq), decay[None])
    out = pl.pallas_call(
        _filter_kernel,
        grid=(FILT_HALF // FILT_TM,),
        in_specs=[pl.BlockSpec((FILT_TM, 2 * FILT_KPAD), lambda i: (i, 0))] + [full(a) for a in args],
        out_specs=pl.BlockSpec((2, FILT_TM, 2 * HY_WIDTH), lambda i: (0, i, 0)),
        out_shape=jax.ShapeDtypeStruct((2, FILT_HALF, 2 * HY_WIDTH), F32),
        compiler_params=_cparams(("parallel",), 32),
        name="hy_filter",
    )(zfeat, *args)
    return out.reshape(1, SEQ, 2 * HY_WIDTH)


def _dft_tables():
    a = np.arange(DFT_R)
    w = np.exp(-2j * np.pi * np.outer(a, a) / DFT_R)
    tw = np.exp(-2j * np.pi * np.outer(a, a) / DFT_N)
    f32 = lambda x: jnp.asarray(x.astype(np.float32))
    f1 = np.concatenate([w.real[:, :DFT_HALF], w.imag[:, :DFT_HALF]], axis=0)
    m1 = np.concatenate([w.real[:DFT_HALF], w.imag[:DFT_HALF]], axis=1) / DFT_N
    c = w[None] * tw[:, :DFT_KB].T[:, None, :]
    blocks = lambda a, b, c, d: np.concatenate(
        [np.concatenate([a, b], axis=2), np.concatenate([c, d], axis=2)], axis=1)
    qu = blocks(c.real, -c.imag, c.imag, c.real)
    ct = np.swapaxes(c, 1, 2)
    gu = blocks(ct.real, ct.imag, -ct.imag, ct.real)
    ta = np.exp(-2j * np.pi * DFT_KB * np.outer(np.arange(DFT_R // DFT_KB), a) / DFT_N)
    ta = np.concatenate([ta.real, ta.imag], axis=1)[:, :, None] * np.ones((1, 1, LANES))
    bf = lambda x: f32(x).astype(BF16)
    return bf(f1), bf(qu), bf(gu), bf(m1), f32(ta)


DFT_KB = 8


def _dft_stage1(x_ref, f1_ref, s_scr):
    def body(i, carry):
        n2b = i * DFT_KB
        xs = jnp.concatenate([x_ref[0, pl.ds(n2b + u, DFT_HALF, stride=DFT_R), :] for u in range(DFT_KB)],
                             axis=1).astype(BF16)
        a = jnp.dot(f1_ref[...], xs, preferred_element_type=F32)
        for u in range(DFT_KB):
            s_scr[pl.ds(pl.multiple_of((n2b + u) * DFT_PITCH, 8), 2 * DFT_R), :] = a[:, u * LANES:(u + 1) * LANES]
        return carry

    lax.fori_loop(0, DFT_R // DFT_KB, body, 0)


def _dft_stage2_rows(s_scr, ta_ref, step, u):
    k1 = step * DFT_KB + u
    re = s_scr[pl.ds(k1, DFT_R, stride=DFT_PITCH), :]
    im = s_scr[pl.ds(DFT_R + k1, DFT_R, stride=DFT_PITCH), :]
    tr, ti = ta_ref[step, :DFT_R, :], ta_ref[step, DFT_R:, :]
    return jnp.concatenate([re * tr - im * ti, re * ti + im * tr], axis=0).astype(BF16)


def _dft_fwd_kernel(x_ref, f1_ref, qu_ref, ta_ref, o_ref, s_scr):
    step = pl.program_id(2)
    pl.when(step == 0)(functools.partial(_dft_stage1, x_ref, f1_ref, s_scr))
    for u in range(DFT_KB):
        a = _dft_stage2_rows(s_scr, ta_ref, step, u)
        o_ref[0, u] = jnp.dot(qu_ref[u], a, preferred_element_type=F32).astype(o_ref.dtype)


def _dft_fwd(x, f1, qu, ta, out_dtype):
    nb, _, c = x.shape
    full = lambda a: pl.BlockSpec(a.shape, lambda b, ct, s: (0,) * a.ndim)
    return pl.pallas_call(
        _dft_fwd_kernel,
        grid=(nb, c // LANES, DFT_R // DFT_KB),
        in_specs=[pl.BlockSpec((1, SEQ, LANES), lambda b, ct, s: (b, 0, ct)), full(f1), full(qu), full(ta)],
        out_specs=pl.BlockSpec((1, DFT_KB, 2 * DFT_R, LANES), lambda b, ct, s: (b, s, 0, ct)),
        out_shape=jax.ShapeDtypeStruct((nb, DFT_R, 2 * DFT_R, c), out_dtype),
        scratch_shapes=[pltpu.VMEM((DFT_R * DFT_PITCH, LANES), F32)],
        compiler_params=_cparams(("parallel", "parallel", "arbitrary"), 40),
        name="hy_dft_fwd",
    )(x, f1, qu, ta)


def _dft_stage1_inv(s_scr, m1_ref, o_ref):
    def body(i, carry):
        n2b = i * DFT_KB
        bc = jnp.concatenate(
            [s_scr[pl.ds(pl.multiple_of((n2b + u) * DFT_PITCH, 8), 2 * DFT_R), :] for u in range(DFT_KB)],
            axis=1).astype(BF16)
        y = jnp.dot(m1_ref[...], bc, preferred_element_type=F32)
        for u in range(DFT_KB):
            o_ref[0, pl.ds(n2b + u, DFT_HALF, stride=DFT_R), :] = y[:, u * LANES:(u + 1) * LANES]
        return carry

    lax.fori_loop(0, DFT_R // DFT_KB, body, 0)


def _dft_conv_kernel(x_ref, f1_ref, qu_ref, ta_ref, hf_ref, hb_ref, gu_ref, m1_ref, o_ref, s_scr):
    step = pl.program_id(2)
    pl.when(step == 0)(functools.partial(_dft_stage1, x_ref, f1_ref, s_scr))
    tr, ti = ta_ref[step, :DFT_R, :], ta_ref[step, DFT_R:, :]
    for u in range(DFT_KB):
        k1 = step * DFT_KB + u
        xk = jnp.dot(qu_ref[u], _dft_stage2_rows(s_scr, ta_ref, step, u), preferred_element_type=F32)
        xr, xi = xk[:DFT_R], xk[DFT_R:]
        hr = hf_ref[0, u, :DFT_R, :].astype(F32) + hb_ref[0, u, :DFT_R, :].astype(F32)
        hi = hf_ref[0, u, DFT_R:, :].astype(F32) - hb_ref[0, u, DFT_R:, :].astype(F32)
        y = jnp.concatenate([xr * hr - xi * hi, xr * hi + xi * hr], axis=0).astype(BF16)
        b = jnp.dot(gu_ref[u], y, preferred_element_type=F32)
        br, bi = b[:DFT_R], b[DFT_R:]
        s_scr[pl.ds(k1, DFT_R, stride=DFT_PITCH), :] = br * tr + bi * ti
        s_scr[pl.ds(DFT_R + k1, DFT_R, stride=DFT_PITCH), :] = bi * tr - br * ti
    pl.when(step == pl.num_programs(2) - 1)(functools.partial(_dft_stage1_inv, s_scr, m1_ref, o_ref))


def _dft_conv(x, hs, f1, qu, gu, m1, ta):
    b, _, c = x.shape
    nct = c // LANES
    full = lambda a: pl.BlockSpec(a.shape, lambda bi, ct, s: (0,) * a.ndim)
    return pl.pallas_call(
        _dft_conv_kernel,
        grid=(b, nct, DFT_R // DFT_KB),
        in_specs=[
            pl.BlockSpec((1, SEQ, LANES), lambda bi, ct, s: (bi, 0, ct)),
            full(f1), full(qu), full(ta),
            pl.BlockSpec((1, DFT_KB, 2 * DFT_R, LANES), lambda bi, ct, s: (0, s, 0, ct)),
            pl.BlockSpec((1, DFT_KB, 2 * DFT_R, LANES), lambda bi, ct, s: (0, s, 0, nct + ct)),
            full(gu), full(m1),
        ],
        out_specs=pl.BlockSpec((1, SEQ, LANES), lambda bi, ct, s: (bi, 0, ct)),
        out_shape=jax.ShapeDtypeStruct((b, SEQ, c), F32),
        scratch_shapes=[pltpu.VMEM((DFT_R * DFT_PITCH, LANES), F32)],
        compiler_params=_cparams(("parallel", "parallel", "arbitrary"), 48),
        name="hy_dft_conv",
    )(x, f1, qu, ta, hs, hs, gu, m1)


MIXOUT_TM = 512


def _mix_out_kernel(x_ref, ya_ref, yh_ref, z_ref, x0_ref, yc_ref, skip_ref, gg_ref, w_ref, gpost_ref, o_ref):
    a0, b0 = NA_WIDTH, NA_WIDTH + HY_WIDTH
    yb = x0_ref[...] * (yh_ref[...] + skip_ref[...] * z_ref[...])
    acc = jnp.dot(_rms(ya_ref[...], gg_ref[:, :a0]).astype(BF16), w_ref[:a0, :], preferred_element_type=F32)
    acc += jnp.dot(_rms(yb, gg_ref[:, a0:b0]).astype(BF16), w_ref[a0:b0, :], preferred_element_type=F32)
    acc += jnp.dot(_rms(yc_ref[...], gg_ref[:, b0:]).astype(BF16), w_ref[b0:, :], preferred_element_type=F32)
    o_ref[...] = x_ref[...] + _rms(acc, gpost_ref[...])


def _mix_out(x, ya, yh, z, x0, yc, skip, gg, w, gpost):
    n = x.shape[0]
    tok = lambda width: pl.BlockSpec((MIXOUT_TM, width), lambda i: (i, 0))
    full = lambda a: pl.BlockSpec(a.shape, lambda i: (0,) * a.ndim)
    return pl.pallas_call(
        _mix_out_kernel,
        grid=(n // MIXOUT_TM,),
        in_specs=[tok(D_MODEL), tok(NA_WIDTH), tok(HY_WIDTH), tok(HY_WIDTH), tok(HY_WIDTH), tok(DIL_WIDTH),
                  full(skip), full(gg), full(w), full(gpost)],
        out_specs=tok(D_MODEL),
        out_shape=jax.ShapeDtypeStruct((n, D_MODEL), F32),
        compiler_params=_cparams(("parallel",), 40),
        name="mix_out",
    )(x, ya, yh, z, x0, yc, skip, gg, w, gpost)


XA_TM = 512


def _xattn_kernel(x_ref, gpre_ref, wq_ref, kv_ref, wo_ref, gpost_ref, o_ref):
    x = x_ref[...]
    h = _rms(x, gpre_ref[...]).astype(BF16)
    q = jnp.dot(h, wq_ref[...], preferred_element_type=F32) * (XA_HEAD_DIM ** -0.5)
    outs = []
    for hd in range(XA_HEADS):
        c0 = hd * XA_HEAD_DIM
        qh = q[:, c0:c0 + XA_HEAD_DIM].astype(BF16)
        kh = kv_ref[0, :, c0:c0 + XA_HEAD_DIM].astype(BF16)
        vh = kv_ref[0, :, D_MODEL + c0:D_MODEL + c0 + XA_HEAD_DIM].astype(BF16)
        s = lax.dot_general(qh, kh, (((1,), (1,)), ((), ())), preferred_element_type=F32)
        p = jnp.exp(s - jnp.max(s, axis=-1, keepdims=True))
        l = jnp.sum(p, axis=-1, keepdims=True)
        outs.append(jnp.dot(p.astype(BF16), vh, preferred_element_type=F32) / l)
    o = jnp.concatenate(outs, axis=-1).astype(BF16)
    xa = jnp.dot(o, wo_ref[...], preferred_element_type=F32)
    o_ref[...] = x + _rms(xa, gpost_ref[...])


def _xattn(x, gpre, wq, kv, wo, gpost):
    n = x.shape[0]
    per_batch = SEQ // XA_TM
    full = lambda a: pl.BlockSpec(a.shape, lambda i: (0,) * a.ndim)
    return pl.pallas_call(
        _xattn_kernel,
        grid=(n // XA_TM,),
        in_specs=[
            pl.BlockSpec((XA_TM, D_MODEL), lambda i: (i, 0)),
            full(gpre), full(wq),
            pl.BlockSpec((1,) + kv.shape[1:], lambda i: (i // per_batch, 0, 0)),
            full(wo), full(gpost),
        ],
        out_specs=pl.BlockSpec((XA_TM, D_MODEL), lambda i: (i, 0)),
        out_shape=jax.ShapeDtypeStruct((n, D_MODEL), F32),
        compiler_params=_cparams(("parallel",), 40),
        name="xattn",
    )(x, gpre, wq, kv, wo, gpost)


def kernel(x, mem, norm_g, w_in, w_out, na_rpb, t5_table, hy_conv_w, hy_conv_b, hy_f_w1, hy_f_b1, hy_f_w2, hy_f_b2, hy_f_w3, hy_f_b3, hy_f_w4, hy_freq, hy_decay, hy_skip, xa_wq, xa_wkv, xa_wo, ffn_w_in, ffn_w_out):
    bsz, seq, d = x.shape
    assert (seq, d) == (SEQ, D_MODEL)
    depth = norm_g.shape[0]
    mem_len = mem.shape[1]
    n = bsz * seq
    scale = HEAD_DIM ** -0.5
    wa, wb, wc = 3 * NA_WIDTH, 3 * HY_WIDTH, 3 * DIL_WIDTH
    colscale = np.ones((1, wa + wb + wc), np.float32)
    colscale[:, :NA_WIDTH] = scale
    colscale[:, wa + wb:wa + wb + DIL_WIDTH] = scale
    colscale = jnp.asarray(colscale)

    zfeat = _filter_features()
    f1, qu, gu, m1, ta = _dft_tables()
    dil_bias = _dil_bias(t5_table)

    xf = x.reshape(n, d)
    memf = mem.reshape(bsz * mem_len, d)
    for l in range(depth):
        g = norm_g[l][:, None, :]
        xf = _ffn(xf, g[0], ffn_w_in[l, 0].astype(BF16), ffn_w_out[l, 0].astype(BF16), g[1])

        pa, pb, pc = _mix_proj(xf, g[2], w_in[l].astype(BF16), colscale)
        pa = pa.reshape(bsz, seq, wa)
        pb = pb.reshape(bsz, seq, wb)
        pc = pc.reshape(bsz, seq, wc)
        ya = _na_attention(pa, _na_tz(na_rpb[l]))
        z, x0 = _conv_gate(pb, hy_conv_w[l], hy_conv_b[l][None])
        hfilt = _hyena_filters(zfeat, hy_f_w1[l], hy_f_b1[l], hy_f_w2[l], hy_f_b2[l], hy_f_w3[l], hy_f_b3[l],
                               hy_f_w4[l], hy_freq[l], hy_decay[l])
        yh = _dft_conv(z, _dft_fwd(hfilt, f1, qu, ta, BF16), f1, qu, gu, m1, ta)
        yc = _dil_attention(pc, dil_bias)
        xf = _mix_out(xf, ya.reshape(n, NA_WIDTH), yh.reshape(n, HY_WIDTH), z.reshape(n, HY_WIDTH),
                      x0.reshape(n, HY_WIDTH), yc.reshape(n, DIL_WIDTH), hy_skip[l][None], g[3],
                      w_out[l].astype(BF16), g[4])

        kv = _norm_mm(memf, g[6], xa_wkv[l].astype(BF16), mem_len).reshape(bsz, mem_len, 2 * D_MODEL)
        xf = _xattn(xf, g[5], xa_wq[l].astype(BF16), kv, xa_wo[l].astype(BF16), g[7])

        xf = _ffn(xf, g[8], ffn_w_in[l, 1].astype(BF16), ffn_w_out[l, 1].astype(BF16), g[9])
    return xf.reshape(bsz, seq, d)
```

```python
import functools
import math

import numpy as np
import jax
import jax.numpy as jnp
from jax import lax
from jax.experimental import pallas as pl
from jax.experimental.pallas import tpu as pltpu

F32 = jnp.float32
BF16 = jnp.bfloat16

D_MODEL = 1024
SEQ = 8192
GRID_W = 64
HEAD_DIM = 64
NA_WIDTH = 384
HY_WIDTH = 256
DIL_WIDTH = 384
NA_ROWS = 8
NA_COLS = 16
HY_BANDS = 16
DIL_CONFIGS = ((128, 1), (512, 4), (2048, 16))
T5_BUCKETS = 32
T5_MAX_DIST = 1024
XA_HEADS = 4
XA_HEAD_DIM = 256
FFN_HIDDEN = 2816
RMS_EPS = 1e-6
NEG_INF = -1e30

LANES = 128
MIB = 1024 * 1024

NA_QROWS = 8
NA_KROWS = 16
NA_BQ = NA_QROWS * GRID_W
NA_GROUPS = 4
NA_GQ = GRID_W // NA_GROUPS
NA_GK = 2 * NA_GQ
DIL_HALF = 64
DIL_BQ = 128
DIL_UNROLL = 8
DIL_BK = DIL_BQ + 2 * DIL_HALF
DFT_R = 128
DFT_HALF = 64
DFT_N = DFT_R * DFT_R
DFT_PITCH = 264


def _cparams(sem, vmem_mib):
    return pltpu.CompilerParams(dimension_semantics=sem, vmem_limit_bytes=vmem_mib * MIB)


def _rms(x, g):
    return x * lax.rsqrt(jnp.mean(x * x, axis=-1, keepdims=True) + RMS_EPS) * g


FFN_TM = 512
FFN_TH = 256


def _ffn_kernel(x_ref, gpre_ref, win_ref, wout_ref, gpost_ref, o_ref):
    x = x_ref[...]
    h = _rms(x, gpre_ref[...]).astype(BF16)
    acc = jnp.zeros((FFN_TM, D_MODEL), F32)
    for j in range(FFN_HIDDEN // FFN_TH):
        c0 = j * FFN_TH
        gate = jnp.dot(h, win_ref[:, c0:c0 + FFN_TH], preferred_element_type=F32)
        up = jnp.dot(h, win_ref[:, FFN_HIDDEN + c0:FFN_HIDDEN + c0 + FFN_TH], preferred_element_type=F32)
        act = (gate * jax.nn.sigmoid(gate) * up).astype(BF16)
        acc += jnp.dot(act, wout_ref[c0:c0 + FFN_TH, :], preferred_element_type=F32)
    o_ref[...] = x + 0.5 * _rms(acc, gpost_ref[...])


def _ffn(x, g_pre, w_in, w_out, g_post):
    n = x.shape[0]
    const = lambda a: pl.BlockSpec(a.shape, lambda i: (0,) * a.ndim, pipeline_mode=pl.Buffered(1))
    return pl.pallas_call(
        _ffn_kernel,
        grid=(n // FFN_TM,),
        in_specs=[
            pl.BlockSpec((FFN_TM, D_MODEL), lambda i: (i, 0)),
            const(g_pre), const(w_in), const(w_out), const(g_post),
        ],
        out_specs=pl.BlockSpec((FFN_TM, D_MODEL), lambda i: (i, 0)),
        out_shape=jax.ShapeDtypeStruct((n, D_MODEL), F32),
        compiler_params=_cparams(("parallel",), 48),
        name="ffn",
    )(x, g_pre, w_in, w_out, g_post)


PROJ_TM = 512


def _mix_proj_kernel(x_ref, g_ref, w_ref, cs_ref, pa_ref, pb_ref, pc_ref):
    h = _rms(x_ref[...], g_ref[...]).astype(BF16)
    p = jnp.dot(h, w_ref[...], preferred_element_type=F32) * cs_ref[...]
    wa = 3 * NA_WIDTH
    wb = 3 * HY_WIDTH
    pa_ref[...] = p[:, :wa]
    pb_ref[...] = p[:, wa:wa + wb]
    pc_ref[...] = p[:, wa + wb:]


def _mix_proj(x, g, w, colscale):
    n = x.shape[0]
    wa, wb, wc = 3 * NA_WIDTH, 3 * HY_WIDTH, 3 * DIL_WIDTH
    ncol = wa + wb + wc
    return pl.pallas_call(
        _mix_proj_kernel,
        grid=(n // PROJ_TM,),
        in_specs=[
            pl.BlockSpec((PROJ_TM, D_MODEL), lambda i: (i, 0)),
            pl.BlockSpec((1, D_MODEL), lambda i: (0, 0)),
            pl.BlockSpec((D_MODEL, ncol), lambda i: (0, 0)),
            pl.BlockSpec((1, ncol), lambda i: (0, 0)),
        ],
        out_specs=[
            pl.BlockSpec((PROJ_TM, wa), lambda i: (i, 0)),
            pl.BlockSpec((PROJ_TM, wb), lambda i: (i, 0)),
            pl.BlockSpec((PROJ_TM, wc), lambda i: (i, 0)),
        ],
        out_shape=[
            jax.ShapeDtypeStruct((n, wa), F32),
            jax.ShapeDtypeStruct((n, wb), F32),
            jax.ShapeDtypeStruct((n, wc), F32),
        ],
        compiler_params=_cparams(("parallel",), 48),
        name="mix_proj",
    )(x, g, w, colscale)


def _norm_mm_kernel(x_ref, g_ref, w_ref, o_ref):
    h = _rms(x_ref[...], g_ref[...]).astype(BF16)
    o_ref[...] = jnp.dot(h, w_ref[...], preferred_element_type=F32)


def _norm_mm(x, g, w, tm):
    n, ncol = x.shape[0], w.shape[1]
    return pl.pallas_call(
        _norm_mm_kernel,
        grid=(n // tm,),
        in_specs=[
            pl.BlockSpec((tm, D_MODEL), lambda i: (i, 0)),
            pl.BlockSpec((1, D_MODEL), lambda i: (0, 0)),
            pl.BlockSpec((D_MODEL, ncol), lambda i: (0, 0)),
        ],
        out_specs=pl.BlockSpec((tm, ncol), lambda i: (i, 0)),
        out_shape=jax.ShapeDtypeStruct((n, ncol), F32),
        compiler_params=_cparams(("parallel",), 32),
        name="norm_mm",
    )(x, g, w)


def _softmax_pv(q, k, v, bias):
    s = lax.dot_general(q, k, (((1,), (1,)), ((), ())), preferred_element_type=F32) + bias
    m = jnp.max(s, axis=-1, keepdims=True)
    p = jnp.exp(s - m)
    l = jnp.sum(p, axis=-1, keepdims=True)
    return jnp.dot(p.astype(BF16), v, preferred_element_type=F32), m, l


def _pair_attention(q, k, v, bias, stack_heads):
    bq = q.shape[0]
    lane = lax.broadcasted_iota(jnp.int32, (1, LANES), 1)
    first = lane < HEAD_DIM
    q0 = jnp.where(first, q, 0.0).astype(BF16)
    q1 = jnp.where(first, 0.0, q).astype(BF16)
    if stack_heads:
        res = _softmax_pv(jnp.concatenate([q0, q1], axis=0), k, v, bias)
        return tuple(jnp.where(first, a[:bq], a[bq:]) for a in res)
    res0 = _softmax_pv(q0, k, v, bias[:bq])
    res1 = _softmax_pv(q1, k, v, bias[bq:])
    return tuple(jnp.where(first, a, b) for a, b in zip(res0, res1))


def _na_key_col0(g):
    return int(np.clip(g * NA_GQ - NA_COLS // 2, 0, GRID_W - NA_GK))


def _na_fill_bias(tz_ref, bias_scr, j):
    nrows = SEQ // GRID_W
    start_row = int(np.clip(j * NA_QROWS - NA_ROWS // 2, 0, nrows - NA_KROWS))
    neg = jnp.full((NA_GQ, NA_GK), NEG_INF, F32)
    per_tile = LANES // NA_GK
    for g in range(NA_GROUPS):
        for hh in range(2):
            for rl in range(NA_QROWS):
                r = j * NA_QROWS + rl
                rs = int(np.clip(r - NA_ROWS // 2, 0, nrows - NA_ROWS))
                r0 = (hh * NA_QROWS + rl) * NA_GQ
                for kt in range(NA_KROWS // per_tile):
                    tiles = []
                    for kl in range(kt * per_tile, (kt + 1) * per_tile):
                        kr = start_row + kl
                        tiles.append(tz_ref[hh, kr - r + NA_ROWS - 1, g] if rs <= kr < rs + NA_ROWS else neg)
                    bias_scr[g, r0:r0 + NA_GQ, kt * LANES:(kt + 1) * LANES] = jnp.concatenate(tiles, axis=1)


def _na_kernel(q_ref, k_ref, v_ref, tz_ref, o_ref, bias_scr):
    j = pl.program_id(2)
    nblk = SEQ // NA_BQ
    for case_j in (0, 1, nblk - 1):
        pl.when(j == case_j)(functools.partial(_na_fill_bias, tz_ref, bias_scr, case_j))

    nrows = SEQ // GRID_W
    start_row = jnp.clip(j * NA_QROWS - NA_ROWS // 2, 0, nrows - NA_KROWS)
    start = start_row * GRID_W
    for g in range(NA_GROUPS):
        kc0 = _na_key_col0(g)
        krows = [pl.ds(pl.multiple_of(start + kl * GRID_W + kc0, 8), NA_GK) for kl in range(NA_KROWS)]
        qrows = [slice(rl * GRID_W + g * NA_GQ, rl * GRID_W + (g + 1) * NA_GQ) for rl in range(NA_QROWS)]
        k = jnp.concatenate([k_ref[0, r, :] for r in krows], axis=0).astype(BF16)
        v = jnp.concatenate([v_ref[0, r, :] for r in krows], axis=0).astype(BF16)
        q = jnp.concatenate([q_ref[0, r, :] for r in qrows], axis=0)
        o, _, l = _pair_attention(q, k, v, bias_scr[g], stack_heads=True)
        o = o / l
        for rl, r in enumerate(qrows):
            o_ref[0, r, :] = o[rl * NA_GQ:(rl + 1) * NA_GQ]


def _na_attention(pa, tzg):
    b = pa.shape[0]
    npair = NA_WIDTH // LANES
    nblk = SEQ // NA_BQ
    return pl.pallas_call(
        _na_kernel,
        grid=(b, npair, nblk),
        in_specs=[
            pl.BlockSpec((1, NA_BQ, LANES), lambda bi, p, j: (bi, j, p)),
            pl.BlockSpec((1, SEQ, LANES), lambda bi, p, j: (bi, 0, npair + p)),
            pl.BlockSpec((1, SEQ, LANES), lambda bi, p, j: (bi, 0, 2 * npair + p)),
            pl.BlockSpec((2,) + tzg.shape[1:], lambda bi, p, j: (p, 0, 0, 0, 0)),
        ],
        out_specs=pl.BlockSpec((1, NA_BQ, LANES), lambda bi, p, j: (bi, j, p)),
        out_shape=jax.ShapeDtypeStruct((b, SEQ, NA_WIDTH), F32),
        scratch_shapes=[pltpu.VMEM((NA_GROUPS, 2 * NA_QROWS * NA_GQ, NA_KROWS * NA_GK), F32)],
        compiler_params=_cparams(("parallel", "parallel", "arbitrary"), 48),
        name="na_attn",
    )(pa, pa, pa, tzg)


def _na_tz(rpb):
    col = np.arange(GRID_W)
    cs = np.clip(col - NA_COLS // 2, 0, GRID_W - NA_COLS)
    col_ok = (col[None, :] >= cs[:, None]) & (col[None, :] < cs[:, None] + NA_COLS)
    period = GRID_W + 1
    half = NA_COLS - 1
    filler = jnp.zeros(rpb.shape[:2] + (period - (2 * half + 1),), rpb.dtype)
    v = jnp.concatenate([rpb[..., half:], filler, rpb[..., :half]], axis=-1)
    t = jnp.tile(v, (1, 1, GRID_W))[..., :GRID_W * GRID_W].reshape(rpb.shape[:2] + (GRID_W, GRID_W))
    t = jnp.where(col_ok[None, None], t, NEG_INF)
    groups = []
    for g in range(NA_GROUPS):
        kc0 = _na_key_col0(g)
        assert col_ok[g * NA_GQ:(g + 1) * NA_GQ, :kc0].sum() == 0 and col_ok[g * NA_GQ:(g + 1) * NA_GQ, kc0 + NA_GK:].sum() == 0
        groups.append(t[:, :, g * NA_GQ:(g + 1) * NA_GQ, kc0:kc0 + NA_GK])
    return jnp.stack(groups, axis=2)


def _dil_kernel(q_ref, k_ref, v_ref, bias_ref, o_ref, m_scr, l_scr):
    kcol = lax.broadcasted_iota(jnp.int32, (1, DIL_BK), 1)

    order = sorted(range(len(DIL_CONFIGS)), key=lambda c: -DIL_CONFIGS[c][1])
    for c in order:
        dil = DIL_CONFIGS[c][1]
        m_len = SEQ // dil
        nblk = m_len // DIL_BQ

        def body(it, carry, c=c, dil=dil, m_len=m_len, nblk=nblk):
            g = it // nblk
            n = it % nblk
            pos = n * DIL_BQ

            def rows(p0, size):
                return pl.ds(g + p0 * dil, size, stride=dil) if dil > 1 else pl.ds(p0, size)

            lo = jnp.maximum(pos - DIL_HALF, 0)
            hi = jnp.minimum(pos + DIL_BQ, m_len - DIL_HALF)
            qrows = rows(pos, DIL_BQ)
            k = jnp.concatenate(
                [k_ref[0, rows(lo, DIL_HALF), :], k_ref[0, qrows, :], k_ref[0, rows(hi, DIL_HALF), :]],
                axis=0).astype(BF16)
            v = jnp.concatenate(
                [v_ref[0, rows(lo, DIL_HALF), :], v_ref[0, qrows, :], v_ref[0, rows(hi, DIL_HALF), :]],
                axis=0).astype(BF16)
            edge = jnp.where(((kcol < DIL_HALF) & (n == 0))
                             | ((kcol >= DIL_HALF + DIL_BQ) & (n == nblk - 1)), NEG_INF, 0.0)
            bias = bias_ref[c, 0].reshape(2 * DIL_BQ, DIL_BK) + edge
            o, m, l = _pair_attention(q_ref[0, qrows, :], k, v, bias, stack_heads=True)
            if c == order[0]:
                o_ref[0, qrows, :] = o
                m_scr[qrows, :] = m
                l_scr[qrows, :] = l
                return carry
            m_old = m_scr[qrows, :]
            m_new = jnp.maximum(m_old, m)
            a_old = jnp.exp(m_old - m_new)
            a_new = jnp.exp(m - m_new)
            o_new = o_ref[0, qrows, :] * a_old + o * a_new
            l_new = l_scr[qrows, :] * a_old + l * a_new
            if c == order[-1]:
                o_ref[0, qrows, :] = o_new / l_new
            else:
                o_ref[0, qrows, :] = o_new
                l_scr[qrows, :] = l_new
                m_scr[qrows, :] = m_new
            return carry

        lax.fori_loop(0, SEQ // DIL_BQ, body, 0, unroll=DIL_UNROLL)


def _dil_attention(pc, bias):
    b = pc.shape[0]
    npair = DIL_WIDTH // LANES
    return pl.pallas_call(
        _dil_kernel,
        grid=(b, npair),
        in_specs=[
            pl.BlockSpec((1, SEQ, LANES), lambda bi, p: (bi, 0, p)),
            pl.BlockSpec((1, SEQ, LANES), lambda bi, p: (bi, 0, npair + p)),
            pl.BlockSpec((1, SEQ, LANES), lambda bi, p: (bi, 0, 2 * npair + p)),
            pl.BlockSpec((len(DIL_CONFIGS), 1, 2, DIL_BQ, DIL_BK), lambda bi, p: (0, p, 0, 0, 0)),
        ],
        out_specs=pl.BlockSpec((1, SEQ, LANES), lambda bi, p: (bi, 0, p)),
        out_shape=jax.ShapeDtypeStruct((b, SEQ, DIL_WIDTH), F32),
        scratch_shapes=[pltpu.VMEM((SEQ, LANES), F32), pltpu.VMEM((SEQ, LANES), F32)],
        compiler_params=_cparams(("parallel", "arbitrary"), 56),
        name="dil_attn",
    )(pc, pc, pc, bias)


def _t5_bucket(rel):
    half = T5_BUCKETS // 2
    exact = half // 2
    n = np.abs(rel)
    far = exact + (np.log(np.maximum(n, 1) / exact) / math.log(T5_MAX_DIST / exact) * (half - exact)).astype(np.int32)
    far = np.minimum(far, half - 1)
    return (np.where(rel > 0, half, 0) + np.where(n < exact, n, far)).astype(np.int32)


def _dil_bias(t5_table):
    rel = np.arange(-DIL_HALF, DIL_HALF + 1)
    period = DIL_BK + 1
    nh = t5_table.shape[1]
    out = []
    for _, dil in DIL_CONFIGS:
        vals = t5_table[_t5_bucket(dil * rel)].T
        v = jnp.concatenate([vals, jnp.full((nh, period - vals.shape[1]), NEG_INF, vals.dtype)], axis=1)
        out.append(jnp.tile(v, (1, DIL_BQ))[:, :DIL_BQ * DIL_BK].reshape(nh, DIL_BQ, DIL_BK))
    return jnp.stack(out).reshape(len(DIL_CONFIGS), nh // 2, 2, DIL_BQ, DIL_BK)


CONV_TM = 1024


def _conv_gate_kernel(p_ref, prev_ref, next_ref, w_ref, b_ref, z_ref, x0_ref):
    i = pl.program_id(1)
    p = p_ref[0]
    tm = p.shape[0]
    first = jnp.where(i > 0, prev_ref[0, 7:8, :], 0.0)
    last = jnp.where(i < pl.num_programs(1) - 1, next_ref[0, 0:1, :], 0.0)
    row = lax.broadcasted_iota(jnp.int32, (tm, 1), 0)
    up = jnp.where(row == 0, first, pltpu.roll(p, 1, axis=0))
    dn = jnp.where(row == tm - 1, last, pltpu.roll(p, tm - 1, axis=0))
    uc = up * w_ref[0:1, :] + p * w_ref[1:2, :] + dn * w_ref[2:3, :] + b_ref[...]
    z_ref[0] = uc[:, HY_WIDTH:2 * HY_WIDTH] * uc[:, :HY_WIDTH]
    x0_ref[0] = uc[:, 2 * HY_WIDTH:]


def _conv_gate(pb, w, bvec):
    b = pb.shape[0]
    wb = 3 * HY_WIDTH
    nt = SEQ // CONV_TM
    r8 = CONV_TM // 8
    return pl.pallas_call(
        _conv_gate_kernel,
        grid=(b, nt),
        in_specs=[
            pl.BlockSpec((1, CONV_TM, wb), lambda bi, i: (bi, i, 0)),
            pl.BlockSpec((1, 8, wb), lambda bi, i: (bi, jnp.maximum(i * r8 - 1, 0), 0)),
            pl.BlockSpec((1, 8, wb), lambda bi, i: (bi, jnp.minimum((i + 1) * r8, SEQ // 8 - 1), 0)),
            pl.BlockSpec((3, wb), lambda bi, i: (0, 0)),
            pl.BlockSpec((1, wb), lambda bi, i: (0, 0)),
        ],
        out_specs=[
            pl.BlockSpec((1, CONV_TM, HY_WIDTH), lambda bi, i: (bi, i, 0)),
            pl.BlockSpec((1, CONV_TM, HY_WIDTH), lambda bi, i: (bi, i, 0)),
        ],
        out_shape=[jax.ShapeDtypeStruct((b, SEQ, HY_WIDTH), F32)] * 2,
        compiler_params=_cparams(("parallel", "parallel"), 48),
        name="hy_conv_gate",
    )(pb, pb, pb, w, bvec)


FILT_TM = 1024
FILT_KPAD = 128
FILT_HALF = SEQ // 2


def _filter_kernel(z_ref, w1_ref, b1_ref, w2_ref, b2_ref, w3_ref, b3_ref, w4_ref, fr_ref, dec_ref, h_ref):
    hp = lax.Precision.HIGHEST
    z = z_ref[...]
    fr = fr_ref[...]
    h = jnp.sin(fr * (jnp.dot(z, w1_ref[...], precision=hp, preferred_element_type=F32) + b1_ref[...]))
    h = jnp.sin(fr * (jnp.dot(h, w2_ref[...], precision=hp, preferred_element_type=F32) + b2_ref[...]))
    h = jnp.sin(fr * (jnp.dot(h, w3_ref[...], precision=hp, preferred_element_type=F32) + b3_ref[...]))
    h = jnp.dot(h, w4_ref[...], precision=hp, preferred_element_type=F32)
    wout = 2 * HY_WIDTH
    dec = jnp.abs(dec_ref[...])
    lo = h[:, :wout] * jnp.exp(-z[:, 0:1] * dec)
    hi = h[:, wout:] * jnp.exp(-z[:, FILT_KPAD:FILT_KPAD + 1] * dec)
    row = pl.program_id(0) * FILT_TM + lax.broadcasted_iota(jnp.int32, (FILT_TM, 1), 0)
    col = lax.broadcasted_iota(jnp.int32, (1, wout), 1)
    h_ref[0] = jnp.where((row == 0) & (col >= HY_WIDTH), 0.0, lo)
    h_ref[1] = hi


def _filter_features():
    t = jnp.linspace(0.0, 1.0, SEQ, dtype=F32)[:, None]
    bands = jnp.linspace(1e-4, HY_BANDS - 1, HY_BANDS, dtype=F32)[None, :]
    ang = (2.0 * math.pi / SEQ) * jnp.arange(SEQ, dtype=F32)[:, None] * bands
    z = jnp.concatenate([t, jnp.cos(ang), -jnp.sin(ang)], axis=-1)
    z = jnp.pad(z, ((0, 0), (0, FILT_KPAD - z.shape[1])))
    return jnp.concatenate([z[:FILT_HALF], z[FILT_HALF:]], axis=1)


def _hyena_filters(zfeat, w1, b1, w2, b2, w3, b3, w4, freq, decay):
    def diag2(w):
        zero = jnp.zeros_like(w)
        return jnp.concatenate([jnp.concatenate([w, zero], axis=1), jnp.concatenate([zero, w], axis=1)], axis=0)

    twice = lambda v: jnp.concatenate([v, v])[None]
    w1p = jnp.pad(w1, ((0, FILT_KPAD - w1.shape[0]), (0, 0)))
    full = lambda a: pl.BlockSpec(a.shape, lambda i: (0,) * a.ndim)
    args = (diag2(w1p), twice(b1), diag2(w2), twice(b2), diag2(w3), twice(b3), diag2(w4), twice(freq), decay[None])
    out = pl.pallas_call(
        _filter_kernel,
        grid=(FILT_HALF // FILT_TM,),
        in_specs=[pl.BlockSpec((FILT_TM, 2 * FILT_KPAD), lambda i: (i, 0))] + [full(a) for a in args],
        out_specs=pl.BlockSpec((2, FILT_TM, 2 * HY_WIDTH), lambda i: (0, i, 0)),
        out_shape=jax.ShapeDtypeStruct((2, FILT_HALF, 2 * HY_WIDTH), F32),
        compiler_params=_cparams(("parallel",), 32),
        name="hy_filter",
    )(zfeat, *args)
    return out.reshape(1, SEQ, 2 * HY_WIDTH)


def _dft_tables():
    a = np.arange(DFT_R)
    w = np.exp(-2j * np.pi * np.outer(a, a) / DFT_R)
    tw = np.exp(-2j * np.pi * np.outer(a, a) / DFT_N)
    f32 = lambda x: jnp.asarray(x.astype(np.float32))
    f1 = np.concatenate([w.real[:, :DFT_HALF], w.imag[:, :DFT_HALF]], axis=0)
    m1 = np.concatenate([w.real[:DFT_HALF], w.imag[:DFT_HALF]], axis=1) / DFT_N
    c = w[None] * tw[:, :DFT_KB].T[:, None, :]
    blocks = lambda a, b, c, d: np.concatenate(
        [np.concatenate([a, b], axis=2), np.concatenate([c, d], axis=2)], axis=1)
    qu = blocks(c.real, -c.imag, c.imag, c.real)
    ct = np.swapaxes(c, 1, 2)
    gu = blocks(ct.real, ct.imag, -ct.imag, ct.real)
    ta = np.exp(-2j * np.pi * DFT_KB * np.outer(np.arange(DFT_R // DFT_KB), a) / DFT_N)
    ta = np.concatenate([ta.real, ta.imag], axis=1)[:, :, None] * np.ones((1, 1, LANES))
    bf = lambda x: f32(x).astype(BF16)
    return bf(f1), bf(qu), bf(gu), bf(m1), f32(ta)


DFT_KB = 8


def _dft_stage1(x_ref, f1_ref, s_scr):
    def body(i, carry):
        n2b = i * DFT_KB
        xs = jnp.concatenate([x_ref[0, pl.ds(n2b + u, DFT_HALF, stride=DFT_R), :] for u in range(DFT_KB)],
                             axis=1).astype(BF16)
        a = jnp.dot(f1_ref[...], xs, preferred_element_type=F32)
        for u in range(DFT_KB):
            s_scr[pl.ds(pl.multiple_of((n2b + u) * DFT_PITCH, 8), 2 * DFT_R), :] = a[:, u * LANES:(u + 1) * LANES]
        return carry

    lax.fori_loop(0, DFT_R // DFT_KB, body, 0)


def _dft_stage2_rows(s_scr, ta_ref, step, u):
    k1 = step * DFT_KB + u
    re = s_scr[pl.ds(k1, DFT_R, stride=DFT_PITCH), :]
    im = s_scr[pl.ds(DFT_R + k1, DFT_R, stride=DFT_PITCH), :]
    tr, ti = ta_ref[step, :DFT_R, :], ta_ref[step, DFT_R:, :]
    return jnp.concatenate([re * tr - im * ti, re * ti + im * tr], axis=0).astype(BF16)


def _dft_fwd_kernel(x_ref, f1_ref, qu_ref, ta_ref, o_ref, s_scr):
    step = pl.program_id(2)
    pl.when(step == 0)(functools.partial(_dft_stage1, x_ref, f1_ref, s_scr))
    for u in range(DFT_KB):
        a = _dft_stage2_rows(s_scr, ta_ref, step, u)
        o_ref[0, u] = jnp.dot(qu_ref[u], a, preferred_element_type=F32).astype(o_ref.dtype)


def _dft_fwd(x, f1, qu, ta, out_dtype):
    nb, _, c = x.shape
    full = lambda a: pl.BlockSpec(a.shape, lambda b, ct, s: (0,) * a.ndim)
    return pl.pallas_call(
        _dft_fwd_kernel,
        grid=(nb, c // LANES, DFT_R // DFT_KB),
        in_specs=[pl.BlockSpec((1, SEQ, LANES), lambda b, ct, s: (b, 0, ct)), full(f1), full(qu), full(ta)],
        out_specs=pl.BlockSpec((1, DFT_KB, 2 * DFT_R, LANES), lambda b, ct, s: (b, s, 0, ct)),
        out_shape=jax.ShapeDtypeStruct((nb, DFT_R, 2 * DFT_R, c), out_dtype),
        scratch_shapes=[pltpu.VMEM((DFT_R * DFT_PITCH, LANES), F32)],
        compiler_params=_cparams(("parallel", "parallel", "arbitrary"), 40),
        name="hy_dft_fwd",
    )(x, f1, qu, ta)


def _dft_stage1_inv(s_scr, m1_ref, o_ref):
    def body(i, carry):
        n2b = i * DFT_KB
        bc = jnp.concatenate(
            [s_scr[pl.ds(pl.multiple_of((n2b + u) * DFT_PITCH, 8), 2 * DFT_R), :] for u in range(DFT_KB)],
            axis=1).astype(BF16)
        y = jnp.dot(m1_ref[...], bc, preferred_element_type=F32)
        for u in range(DFT_KB):
            o_ref[0, pl.ds(n2b + u, DFT_HALF, stride=DFT_R), :] = y[:, u * LANES:(u + 1) * LANES]
        return carry

    lax.fori_loop(0, DFT_R // DFT_KB, body, 0)


def _dft_conv_kernel(x_ref, f1_ref, qu_ref, ta_ref, hf_ref, hb_ref, gu_ref, m1_ref, o_ref, s_scr):
    step = pl.program_id(2)
    pl.when(step == 0)(functools.partial(_dft_stage1, x_ref, f1_ref, s_scr))
    tr, ti = ta_ref[step, :DFT_R, :], ta_ref[step, DFT_R:, :]
    for u in range(DFT_KB):
        k1 = step * DFT_KB + u
        xk = jnp.dot(qu_ref[u], _dft_stage2_rows(s_scr, ta_ref, step, u), preferred_element_type=F32)
        xr, xi = xk[:DFT_R], xk[DFT_R:]
        hr = hf_ref[0, u, :DFT_R, :].astype(F32) + hb_ref[0, u, :DFT_R, :].astype(F32)
        hi = hf_ref[0, u, DFT_R:, :].astype(F32) - hb_ref[0, u, DFT_R:, :].astype(F32)
        y = jnp.concatenate([xr * hr - xi * hi, xr * hi + xi * hr], axis=0).astype(BF16)
        b = jnp.dot(gu_ref[u], y, preferred_element_type=F32)
        br, bi = b[:DFT_R], b[DFT_R:]
        s_scr[pl.ds(k1, DFT_R, stride=DFT_PITCH), :] = br * tr + bi * ti
        s_scr[pl.ds(DFT_R + k1, DFT_R, stride=DFT_PITCH), :] = bi * tr - br * ti
    pl.when(step == pl.num_programs(2) - 1)(functools.partial(_dft_stage1_inv, s_scr, m1_ref, o_ref))


def _dft_conv(x, hs, f1, qu, gu, m1, ta):
    b, _, c = x.shape
    nct = c // LANES
    full = lambda a: pl.BlockSpec(a.shape, lambda bi, ct, s: (0,) * a.ndim)
    return pl.pallas_call(
        _dft_conv_kernel,
        grid=(b, nct, DFT_R // DFT_KB),
        in_specs=[
            pl.BlockSpec((1, SEQ, LANES), lambda bi, ct, s: (bi, 0, ct)),
            full(f1), full(qu), full(ta),
            pl.BlockSpec((1, DFT_KB, 2 * DFT_R, LANES), lambda bi, ct, s: (0, s, 0, ct)),
            pl.BlockSpec((1, DFT_KB, 2 * DFT_R, LANES), lambda bi, ct, s: (0, s, 0, nct + ct)),
            full(gu), full(m1),
        ],
        out_specs=pl.BlockSpec((1, SEQ, LANES), lambda bi, ct, s: (bi, 0, ct)),
        out_shape=jax.ShapeDtypeStruct((b, SEQ, c), F32),
        scratch_shapes=[pltpu.VMEM((DFT_R * DFT_PITCH, LANES), F32)],
        compiler_params=_cparams(("parallel", "parallel", "arbitrary"), 48),
        name="hy_dft_conv",
    )(x, f1, qu, ta, hs, hs, gu, m1)


MIXOUT_TM = 512


def _mix_out_kernel(x_ref, ya_ref, yh_ref, z_ref, x0_ref, yc_ref, skip_ref, gg_ref, w_ref, gpost_ref, o_ref):
    a0, b0 = NA_WIDTH, NA_WIDTH + HY_WIDTH
    yb = x0_ref[...] * (yh_ref[...] + skip_ref[...] * z_ref[...])
    acc = jnp.dot(_rms(ya_ref[...], gg_ref[:, :a0]).astype(BF16), w_ref[:a0, :], preferred_element_type=F32)
    acc += jnp.dot(_rms(yb, gg_ref[:, a0:b0]).astype(BF16), w_ref[a0:b0, :], preferred_element_type=F32)
    acc += jnp.dot(_rms(yc_ref[...], gg_ref[:, b0:]).astype(BF16), w_ref[b0:, :], preferred_element_type=F32)
    o_ref[...] = x_ref[...] + _rms(acc, gpost_ref[...])


def _mix_out(x, ya, yh, z, x0, yc, skip, gg, w, gpost):
    n = x.shape[0]
    tok = lambda width: pl.BlockSpec((MIXOUT_TM, width), lambda i: (i, 0))
    full = lambda a: pl.BlockSpec(a.shape, lambda i: (0,) * a.ndim)
    return pl.pallas_call(
        _mix_out_kernel,
        grid=(n // MIXOUT_TM,),
        in_specs=[tok(D_MODEL), tok(NA_WIDTH), tok(HY_WIDTH), tok(HY_WIDTH), tok(HY_WIDTH), tok(DIL_WIDTH),
                  full(skip), full(gg), full(w), full(gpost)],
        out_specs=tok(D_MODEL),
        out_shape=jax.ShapeDtypeStruct((n, D_MODEL), F32),
        compiler_params=_cparams(("parallel",), 40),
        name="mix_out",
    )(x, ya, yh, z, x0, yc, skip, gg, w, gpost)


XA_TM = 512


def _xattn_kernel(x_ref, gpre_ref, wq_ref, kv_ref, wo_ref, gpost_ref, o_ref):
    x = x_ref[...]
    h = _rms(x, gpre_ref[...]).astype(BF16)
    q = jnp.dot(h, wq_ref[...], preferred_element_type=F32) * (XA_HEAD_DIM ** -0.5)
    outs = []
    for hd in range(XA_HEADS):
        c0 = hd * XA_HEAD_DIM
        qh = q[:, c0:c0 + XA_HEAD_DIM].astype(BF16)
        kh = kv_ref[0, :, c0:c0 + XA_HEAD_DIM].astype(BF16)
        vh = kv_ref[0, :, D_MODEL + c0:D_MODEL + c0 + XA_HEAD_DIM].astype(BF16)
        s = lax.dot_general(qh, kh, (((1,), (1,)), ((), ())), preferred_element_type=F32)
        p = jnp.exp(s - jnp.max(s, axis=-1, keepdims=True))
        l = jnp.sum(p, axis=-1, keepdims=True)
        outs.append(jnp.dot(p.astype(BF16), vh, preferred_element_type=F32) / l)
    o = jnp.concatenate(outs, axis=-1).astype(BF16)
    xa = jnp.dot(o, wo_ref[...], preferred_element_type=F32)
    o_ref[...] = x + _rms(xa, gpost_ref[...])


def _xattn(x, gpre, wq, kv, wo, gpost):
    n = x.shape[0]
    per_batch = SEQ // XA_TM
    full = lambda a: pl.BlockSpec(a.shape, lambda i: (0,) * a.ndim)
    return pl.pallas_call(
        _xattn_kernel,
        grid=(n // XA_TM,),
        in_specs=[
            pl.BlockSpec((XA_TM, D_MODEL), lambda i: (i, 0)),
            full(gpre), full(wq),
            pl.BlockSpec((1,) + kv.shape[1:], lambda i: (i // per_batch, 0, 0)),
            full(wo), full(gpost),
        ],
        out_specs=pl.BlockSpec((XA_TM, D_MODEL), lambda i: (i, 0)),
        out_shape=jax.ShapeDtypeStruct((n, D_MODEL), F32),
        compiler_params=_cparams(("parallel",), 40),
        name="xattn",
    )(x, gpre, wq, kv, wo, gpost)


def kernel(x, mem, norm_g, w_in, w_out, na_rpb, t5_table, hy_conv_w, hy_conv_b, hy_f_w1, hy_f_b1, hy_f_w2, hy_f_b2, hy_f_w3, hy_f_b3, hy_f_w4, hy_freq, hy_decay, hy_skip, xa_wq, xa_wkv, xa_wo, ffn_w_in, ffn_w_out):
    bsz, seq, d = x.shape
    assert (seq, d) == (SEQ, D_MODEL)
    depth = norm_g.shape[0]
    mem_len = mem.shape[1]
    n = bsz * seq
    scale = HEAD_DIM ** -0.5
    wa, wb, wc = 3 * NA_WIDTH, 3 * HY_WIDTH, 3 * DIL_WIDTH
    colscale = np.ones((1, wa + wb + wc), np.float32)
    colscale[:, :NA_WIDTH] = scale
    colscale[:, wa + wb:wa + wb + DIL_WIDTH] = scale
    colscale = jnp.asarray(colscale)

    zfeat = _filter_features()
    f1, qu, gu, m1, ta = _dft_tables()
    dil_bias = _dil_bias(t5_table)

    xf = x.reshape(n, d)
    memf = mem.reshape(bsz * mem_len, d)
    for l in range(depth):
        g = norm_g[l][:, None, :]
        xf = _ffn(xf, g[0], ffn_w_in[l, 0].astype(BF16), ffn_w_out[l, 0].astype(BF16), g[1])

        pa, pb, pc = _mix_proj(xf, g[2], w_in[l].astype(BF16), colscale)
        pa = pa.reshape(bsz, seq, wa)
        pb = pb.reshape(bsz, seq, wb)
        pc = pc.reshape(bsz, seq, wc)
        ya = _na_attention(pa, _na_tz(na_rpb[l]))
        z, x0 = _conv_gate(pb, hy_conv_w[l], hy_conv_b[l][None])
        hfilt = _hyena_filters(zfeat, hy_f_w1[l], hy_f_b1[l], hy_f_w2[l], hy_f_b2[l], hy_f_w3[l], hy_f_b3[l],
                               hy_f_w4[l], hy_freq[l], hy_decay[l])
        yh = _dft_conv(z, _dft_fwd(hfilt, f1, qu, ta, BF16), f1, qu, gu, m1, ta)
        yc = _dil_attention(pc, dil_bias)
        xf = _mix_out(xf, ya.reshape(n, NA_WIDTH), yh.reshape(n, HY_WIDTH), z.reshape(n, HY_WIDTH),
                      x0.reshape(n, HY_WIDTH), yc.reshape(n, DIL_WIDTH), hy_skip[l][None], g[3],
                      w_out[l].astype(BF16), g[4])

        kv = _norm_mm(memf, g[6], xa_wkv[l].astype(BF16), mem_len).reshape(bsz, mem_len, 2 * D_MODEL)
        xf = _xattn(xf, g[5], xa_wq[l].astype(BF16), kv, xa_wo[l].astype(BF16), g[7])

        xf = _ffn(xf, g[8], ffn_w_in[l, 1].astype(BF16), ffn_w_out[l, 1].astype(BF16), g[9])
    return xf.reshape(bsz, seq, d)
```

```python
import functools
import math

import numpy as np
import jax
import jax.numpy as jnp
from jax import lax
from jax.experimental import pallas as pl
from jax.experimental.pallas import tpu as pltpu

F32 = jnp.float32
BF16 = jnp.bfloat16

D_MODEL = 1024
SEQ = 8192
GRID_W = 64
HEAD_DIM = 64
NA_WIDTH = 384
HY_WIDTH = 256
DIL_WIDTH = 384
NA_ROWS = 8
NA_COLS = 16
HY_BANDS = 16
DIL_CONFIGS = ((128, 1), (512, 4), (2048, 16))
T5_BUCKETS = 32
T5_MAX_DIST = 1024
XA_HEADS = 4
XA_HEAD_DIM = 256
FFN_HIDDEN = 2816
RMS_EPS = 1e-6
NEG_INF = -1e30

LANES = 128
MIB = 1024 * 1024

NA_QROWS = 8
NA_KROWS = 16
NA_BQ = NA_QROWS * GRID_W
NA_GROUPS = 4
NA_GQ = GRID_W // NA_GROUPS
NA_GK = 2 * NA_GQ
DIL_HALF = 64
DIL_BQ = 128
DIL_UNROLL = 8
DIL_BK = DIL_BQ + 2 * DIL_HALF
DFT_R = 128
DFT_HALF = 64
DFT_N = DFT_R * DFT_R
DFT_PITCH = 264


def _cparams(sem, vmem_mib):
    return pltpu.CompilerParams(dimension_semantics=sem, vmem_limit_bytes=vmem_mib * MIB)


def _rms(x, g):
    return x * lax.rsqrt(jnp.mean(x * x, axis=-1, keepdims=True) + RMS_EPS) * g


FFN_TM = 512
FFN_TH = 256


def _ffn_body(x, gpre_ref, win_ref, wout_ref, gpost_ref):
    h = _rms(x, gpre_ref[...]).astype(BF16)
    acc = jnp.zeros(x.shape, F32)
    for j in range(FFN_HIDDEN // FFN_TH):
        c0 = j * FFN_TH
        gate = jnp.dot(h, win_ref[:, c0:c0 + FFN_TH], preferred_element_type=F32)
        up = jnp.dot(h, win_ref[:, FFN_HIDDEN + c0:FFN_HIDDEN + c0 + FFN_TH], preferred_element_type=F32)
        act = (gate * jax.nn.sigmoid(gate) * up).astype(BF16)
        acc += jnp.dot(act, wout_ref[c0:c0 + FFN_TH, :], preferred_element_type=F32)
    return x + 0.5 * _rms(acc, gpost_ref[...])


def _pre_mixer_kernel(x_ref, gfpre_ref, fwin_ref, fwout_ref, gfpost_ref, gmix_ref, w_ref, cs_ref,
                      o_ref, pa_ref, pb_ref, pc_ref):
    x = _ffn_body(x_ref[...], gfpre_ref, fwin_ref, fwout_ref, gfpost_ref)
    o_ref[...] = x
    h = _rms(x, gmix_ref[...]).astype(BF16)
    c0 = 0
    for ref in (pa_ref, pb_ref, pc_ref):
        c1 = c0 + ref.shape[1]
        ref[...] = jnp.dot(h, w_ref[:, c0:c1], preferred_element_type=F32) * cs_ref[:, c0:c1]
        c0 = c1


def _pre_mixer(x, g_fpre, fw_in, fw_out, g_fpost, g_mix, w, colscale):
    n = x.shape[0]
    widths = (3 * NA_WIDTH, 3 * HY_WIDTH, 3 * DIL_WIDTH)
    tok = lambda width: pl.BlockSpec((FFN_TM, width), lambda i: (i, 0))
    const = lambda a: pl.BlockSpec(a.shape, lambda i: (0,) * a.ndim, pipeline_mode=pl.Buffered(1))
    return pl.pallas_call(
        _pre_mixer_kernel,
        grid=(n // FFN_TM,),
        in_specs=[tok(D_MODEL), const(g_fpre), const(fw_in), const(fw_out), const(g_fpost),
                  const(g_mix), const(w), const(colscale)],
        out_specs=[tok(D_MODEL)] + [tok(wd) for wd in widths],
        out_shape=[jax.ShapeDtypeStruct((n, wd), F32) for wd in (D_MODEL,) + widths],
        compiler_params=_cparams(("parallel",), 56),
        name="pre_mixer",
    )(x, g_fpre, fw_in, fw_out, g_fpost, g_mix, w, colscale)


def _norm_mm_kernel(x_ref, g_ref, w_ref, o_ref):
    h = _rms(x_ref[...], g_ref[...]).astype(BF16)
    o_ref[...] = jnp.dot(h, w_ref[...], preferred_element_type=F32).astype(o_ref.dtype)


def _norm_mm(x, g, w, tm, out_dtype):
    n, ncol = x.shape[0], w.shape[1]
    return pl.pallas_call(
        _norm_mm_kernel,
        grid=(n // tm,),
        in_specs=[
            pl.BlockSpec((tm, D_MODEL), lambda i: (i, 0)),
            pl.BlockSpec((1, D_MODEL), lambda i: (0, 0)),
            pl.BlockSpec((D_MODEL, ncol), lambda i: (0, 0)),
        ],
        out_specs=pl.BlockSpec((tm, ncol), lambda i: (i, 0)),
        out_shape=jax.ShapeDtypeStruct((n, ncol), out_dtype),
        compiler_params=_cparams(("parallel",), 32),
        name="norm_mm",
    )(x, g, w)


def _softmax_pv(q, k, v, bias):
    s = lax.dot_general(q, k, (((1,), (1,)), ((), ())), preferred_element_type=F32) + bias
    m = jnp.max(s, axis=-1, keepdims=True)
    p = jnp.exp(s - m)
    l = jnp.sum(p, axis=-1, keepdims=True)
    return jnp.dot(p.astype(BF16), v, preferred_element_type=F32), m, l


def _pair_attention(q, k, v, bias, stack_heads):
    bq = q.shape[0]
    lane = lax.broadcasted_iota(jnp.int32, (1, LANES), 1)
    first = lane < HEAD_DIM
    q0 = jnp.where(first, q, 0.0).astype(BF16)
    q1 = jnp.where(first, 0.0, q).astype(BF16)
    if stack_heads:
        res = _softmax_pv(jnp.concatenate([q0, q1], axis=0), k, v, bias)
        return tuple(jnp.where(first, a[:bq], a[bq:]) for a in res)
    res0 = _softmax_pv(q0, k, v, bias[:bq])
    res1 = _softmax_pv(q1, k, v, bias[bq:])
    return tuple(jnp.where(first, a, b) for a, b in zip(res0, res1))


def _na_key_col0(g):
    return int(np.clip(g * NA_GQ - NA_COLS // 2, 0, GRID_W - NA_GK))


def _na_fill_bias(tz_ref, bias_scr, j):
    nrows = SEQ // GRID_W
    start_row = int(np.clip(j * NA_QROWS - NA_ROWS // 2, 0, nrows - NA_KROWS))
    neg = jnp.full((NA_GQ, NA_GK), NEG_INF, F32)
    per_tile = LANES // NA_GK
    for g in range(NA_GROUPS):
        for hh in range(2):
            for rl in range(NA_QROWS):
                r = j * NA_QROWS + rl
                rs = int(np.clip(r - NA_ROWS // 2, 0, nrows - NA_ROWS))
                r0 = (hh * NA_QROWS + rl) * NA_GQ
                for kt in range(NA_KROWS // per_tile):
                    tiles = []
                    for kl in range(kt * per_tile, (kt + 1) * per_tile):
                        kr = start_row + kl
                        tiles.append(tz_ref[hh, kr - r + NA_ROWS - 1, g] if rs <= kr < rs + NA_ROWS else neg)
                    bias_scr[g, r0:r0 + NA_GQ, kt * LANES:(kt + 1) * LANES] = jnp.concatenate(tiles, axis=1)


def _na_kernel(q_ref, k_ref, v_ref, tz_ref, o_ref, bias_scr):
    j = pl.program_id(2)
    nblk = SEQ // NA_BQ
    for case_j in (0, 1, nblk - 1):
        pl.when(j == case_j)(functools.partial(_na_fill_bias, tz_ref, bias_scr, case_j))

    nrows = SEQ // GRID_W
    start_row = jnp.clip(j * NA_QROWS - NA_ROWS // 2, 0, nrows - NA_KROWS)
    start = start_row * GRID_W
    for g in range(NA_GROUPS):
        kc0 = _na_key_col0(g)
        krows = [pl.ds(pl.multiple_of(start + kl * GRID_W + kc0, 8), NA_GK) for kl in range(NA_KROWS)]
        qrows = [slice(rl * GRID_W + g * NA_GQ, rl * GRID_W + (g + 1) * NA_GQ) for rl in range(NA_QROWS)]
        k = jnp.concatenate([k_ref[0, r, :] for r in krows], axis=0).astype(BF16)
        v = jnp.concatenate([v_ref[0, r, :] for r in krows], axis=0).astype(BF16)
        q = jnp.concatenate([q_ref[0, r, :] for r in qrows], axis=0)
        o, _, l = _pair_attention(q, k, v, bias_scr[g], stack_heads=True)
        o = o / l
        for rl, r in enumerate(qrows):
            o_ref[0, r, :] = o[rl * NA_GQ:(rl + 1) * NA_GQ]


def _na_attention(pa, tzg):
    b = pa.shape[0]
    npair = NA_WIDTH // LANES
    nblk = SEQ // NA_BQ
    return pl.pallas_call(
        _na_kernel,
        grid=(b, npair, nblk),
        in_specs=[
            pl.BlockSpec((1, NA_BQ, LANES), lambda bi, p, j: (bi, j, p)),
            pl.BlockSpec((1, SEQ, LANES), lambda bi, p, j: (bi, 0, npair + p)),
            pl.BlockSpec((1, SEQ, LANES), lambda bi, p, j: (bi, 0, 2 * npair + p)),
            pl.BlockSpec((2,) + tzg.shape[1:], lambda bi, p, j: (p, 0, 0, 0, 0)),
        ],
        out_specs=pl.BlockSpec((1, NA_BQ, LANES), lambda bi, p, j: (bi, j, p)),
        out_shape=jax.ShapeDtypeStruct((b, SEQ, NA_WIDTH), F32),
        scratch_shapes=[pltpu.VMEM((NA_GROUPS, 2 * NA_QROWS * NA_GQ, NA_KROWS * NA_GK), F32)],
        compiler_params=_cparams(("parallel", "parallel", "arbitrary"), 48),
        name="na_attn",
    )(pa, pa, pa, tzg)


def _na_tz(rpb):
    col = np.arange(GRID_W)
    cs = np.clip(col - NA_COLS // 2, 0, GRID_W - NA_COLS)
    col_ok = (col[None, :] >= cs[:, None]) & (col[None, :] < cs[:, None] + NA_COLS)
    period = GRID_W + 1
    half = NA_COLS - 1
    filler = jnp.zeros(rpb.shape[:2] + (period - (2 * half + 1),), rpb.dtype)
    v = jnp.concatenate([rpb[..., half:], filler, rpb[..., :half]], axis=-1)
    t = jnp.tile(v, (1, 1, GRID_W))[..., :GRID_W * GRID_W].reshape(rpb.shape[:2] + (GRID_W, GRID_W))
    t = jnp.where(col_ok[None, None], t, NEG_INF)
    groups = []
    for g in range(NA_GROUPS):
        kc0 = _na_key_col0(g)
        assert col_ok[g * NA_GQ:(g + 1) * NA_GQ, :kc0].sum() == 0 and col_ok[g * NA_GQ:(g + 1) * NA_GQ, kc0 + NA_GK:].sum() == 0
        groups.append(t[:, :, g * NA_GQ:(g + 1) * NA_GQ, kc0:kc0 + NA_GK])
    return jnp.stack(groups, axis=2)


def _dil_kernel(q_ref, k_ref, v_ref, bias_ref, o_ref, m_scr, l_scr):
    kcol = lax.broadcasted_iota(jnp.int32, (1, DIL_BK), 1)

    order = sorted(range(len(DIL_CONFIGS)), key=lambda c: -DIL_CONFIGS[c][1])
    for c in order:
        dil = DIL_CONFIGS[c][1]
        m_len = SEQ // dil
        nblk = m_len // DIL_BQ

        def body(it, carry, c=c, dil=dil, m_len=m_len, nblk=nblk):
            g = it // nblk
            n = it % nblk
            pos = n * DIL_BQ

            def rows(p0, size):
                return pl.ds(g + p0 * dil, size, stride=dil) if dil > 1 else pl.ds(p0, size)

            lo = jnp.maximum(pos - DIL_HALF, 0)
            hi = jnp.minimum(pos + DIL_BQ, m_len - DIL_HALF)
            qrows = rows(pos, DIL_BQ)
            k = jnp.concatenate(
                [k_ref[0, rows(lo, DIL_HALF), :], k_ref[0, qrows, :], k_ref[0, rows(hi, DIL_HALF), :]],
                axis=0).astype(BF16)
            v = jnp.concatenate(
                [v_ref[0, rows(lo, DIL_HALF), :], v_ref[0, qrows, :], v_ref[0, rows(hi, DIL_HALF), :]],
                axis=0).astype(BF16)
            edge = jnp.where(((kcol < DIL_HALF) & (n == 0))
                             | ((kcol >= DIL_HALF + DIL_BQ) & (n == nblk - 1)), NEG_INF, 0.0)
            bias = bias_ref[c, 0].reshape(2 * DIL_BQ, DIL_BK) + edge
            o, m, l = _pair_attention(q_ref[0, qrows, :], k, v, bias, stack_heads=True)
            if c == order[0]:
                o_ref[0, qrows, :] = o
                m_scr[qrows, :] = m
                l_scr[qrows, :] = l
                return carry
            m_old = m_scr[qrows, :]
            m_new = jnp.maximum(m_old, m)
            a_old = jnp.exp(m_old - m_new)
            a_new = jnp.exp(m - m_new)
            o_new = o_ref[0, qrows, :] * a_old + o * a_new
            l_new = l_scr[qrows, :] * a_old + l * a_new
            if c == order[-1]:
                o_ref[0, qrows, :] = o_new / l_new
            else:
                o_ref[0, qrows, :] = o_new
                l_scr[qrows, :] = l_new
                m_scr[qrows, :] = m_new
            return carry

        lax.fori_loop(0, SEQ // DIL_BQ, body, 0, unroll=DIL_UNROLL)


def _dil_attention(pc, bias):
    b = pc.shape[0]
    npair = DIL_WIDTH // LANES
    return pl.pallas_call(
        _dil_kernel,
        grid=(b, npair),
        in_specs=[
            pl.BlockSpec((1, SEQ, LANES), lambda bi, p: (bi, 0, p)),
            pl.BlockSpec((1, SEQ, LANES), lambda bi, p: (bi, 0, npair + p)),
            pl.BlockSpec((1, SEQ, LANES), lambda bi, p: (bi, 0, 2 * npair + p)),
            pl.BlockSpec((len(DIL_CONFIGS), 1, 2, DIL_BQ, DIL_BK), lambda bi, p: (0, p, 0, 0, 0)),
        ],
        out_specs=pl.BlockSpec((1, SEQ, LANES), lambda bi, p: (bi, 0, p)),
        out_shape=jax.ShapeDtypeStruct((b, SEQ, DIL_WIDTH), F32),
        scratch_shapes=[pltpu.VMEM((SEQ, LANES), F32), pltpu.VMEM((SEQ, LANES), F32)],
        compiler_params=_cparams(("parallel", "arbitrary"), 56),
        name="dil_attn",
    )(pc, pc, pc, bias)


def _t5_bucket(rel):
    half = T5_BUCKETS // 2
    exact = half // 2
    n = np.abs(rel)
    far = exact + (np.log(np.maximum(n, 1) / exact) / math.log(T5_MAX_DIST / exact) * (half - exact)).astype(np.int32)
    far = np.minimum(far, half - 1)
    return (np.where(rel > 0, half, 0) + np.where(n < exact, n, far)).astype(np.int32)


def _dil_bias(t5_table):
    rel = np.arange(-DIL_HALF, DIL_HALF + 1)
    period = DIL_BK + 1
    nh = t5_table.shape[1]
    out = []
    for _, dil in DIL_CONFIGS:
        vals = t5_table[_t5_bucket(dil * rel)].T
        v = jnp.concatenate([vals, jnp.full((nh, period - vals.shape[1]), NEG_INF, vals.dtype)], axis=1)
        out.append(jnp.tile(v, (1, DIL_BQ))[:, :DIL_BQ * DIL_BK].reshape(nh, DIL_BQ, DIL_BK))
    return jnp.stack(out).reshape(len(DIL_CONFIGS), nh // 2, 2, DIL_BQ, DIL_BK)


CONV_TM = 1024


def _conv_gate_kernel(p_ref, prev_ref, next_ref, w_ref, b_ref, z_ref, x0_ref):
    i = pl.program_id(1)
    p = p_ref[0]
    tm = p.shape[0]
    first = jnp.where(i > 0, prev_ref[0, 7:8, :], 0.0)
    last = jnp.where(i < pl.num_programs(1) - 1, next_ref[0, 0:1, :], 0.0)
    row = lax.broadcasted_iota(jnp.int32, (tm, 1), 0)
    up = jnp.where(row == 0, first, pltpu.roll(p, 1, axis=0))
    dn = jnp.where(row == tm - 1, last, pltpu.roll(p, tm - 1, axis=0))
    uc = up * w_ref[0:1, :] + p * w_ref[1:2, :] + dn * w_ref[2:3, :] + b_ref[...]
    z_ref[0] = uc[:, HY_WIDTH:2 * HY_WIDTH] * uc[:, :HY_WIDTH]
    x0_ref[0] = uc[:, 2 * HY_WIDTH:]


def _conv_gate(pb, w, bvec):
    b = pb.shape[0]
    wb = 3 * HY_WIDTH
    nt = SEQ // CONV_TM
    r8 = CONV_TM // 8
    return pl.pallas_call(
        _conv_gate_kernel,
        grid=(b, nt),
        in_specs=[
            pl.BlockSpec((1, CONV_TM, wb), lambda bi, i: (bi, i, 0)),
            pl.BlockSpec((1, 8, wb), lambda bi, i: (bi, jnp.maximum(i * r8 - 1, 0), 0)),
            pl.BlockSpec((1, 8, wb), lambda bi, i: (bi, jnp.minimum((i + 1) * r8, SEQ // 8 - 1), 0)),
            pl.BlockSpec((3, wb), lambda bi, i: (0, 0)),
            pl.BlockSpec((1, wb), lambda bi, i: (0, 0)),
        ],
        out_specs=[
            pl.BlockSpec((1, CONV_TM, HY_WIDTH), lambda bi, i: (bi, i, 0)),
            pl.BlockSpec((1, CONV_TM, HY_WIDTH), lambda bi, i: (bi, i, 0)),
        ],
        out_shape=[jax.ShapeDtypeStruct((b, SEQ, HY_WIDTH), F32)] * 2,
        compiler_params=_cparams(("parallel", "parallel"), 48),
        name="hy_conv_gate",
    )(pb, pb, pb, w, bvec)


FILT_TM = 1024
FILT_KPAD = 128
FILT_HALF = SEQ // 2


def _filter_kernel(z_ref, w1_ref, b1_ref, w2_ref, b2_ref, w3_ref, b3_ref, w4_ref, fr_ref, dec_ref, h_ref):
    hp = lax.Precision.HIGHEST
    z = z_ref[...]
    fr = fr_ref[...]
    h = jnp.sin(fr * (jnp.dot(z, w1_ref[...], precision=hp, preferred_element_type=F32) + b1_ref[...]))
    h = jnp.sin(fr * (jnp.dot(h, w2_ref[...], precision=hp, preferred_element_type=F32) + b2_ref[...]))
    h = jnp.sin(fr * (jnp.dot(h, w3_ref[...], precision=hp, preferred_element_type=F32) + b3_ref[...]))
    h = jnp.dot(h, w4_ref[...], precision=hp, preferred_element_type=F32)
    wout = 2 * HY_WIDTH
    dec = jnp.abs(dec_ref[...])
    lo = h[:, :wout] * jnp.exp(-z[:, 0:1] * dec)
    hi = h[:, wout:] * jnp.exp(-z[:, FILT_KPAD:FILT_KPAD + 1] * dec)
    row = pl.program_id(0) * FILT_TM + lax.broadcasted_iota(jnp.int32, (FILT_TM, 1), 0)
    col = lax.broadcasted_iota(jnp.int32, (1, wout), 1)
    h_ref[0] = jnp.where((row == 0) & (col >= HY_WIDTH), 0.0, lo)
    h_ref[1] = hi


def _filter_features():
    t = jnp.linspace(0.0, 1.0, SEQ, dtype=F32)[:, None]
    bands = jnp.linspace(1e-4, HY_BANDS - 1, HY_BANDS, dtype=F32)[None, :]
    ang = (2.0 * math.pi / SEQ) * jnp.arange(SEQ, dtype=F32)[:, None] * bands
    z = jnp.concatenate([t, jnp.cos(ang), -jnp.sin(ang)], axis=-1)
    z = jnp.pad(z, ((0, 0), (0, FILT_KPAD - z.shape[1])))
    return jnp.concatenate([z[:FILT_HALF], z[FILT_HALF:]], axis=1)


def _hyena_filters(zfeat, w1, b1, w2, b2, w3, b3, w4, freq, decay):
    def diag2(w):
        zero = jnp.zeros_like(w)
        return jnp.concatenate([jnp.concatenate([w, zero], axis=1), jnp.concatenate([zero, w], axis=1)], axis=0)

    twice = lambda v: jnp.concatenate([v, v])[None]
    w1p = jnp.pad(w1, ((0, FILT_KPAD - w1.shape[0]), (0, 0)))
    full = lambda a: pl.BlockSpec(a.shape, lambda i: (0,) * a.ndim)
    args = (diag2(w1p), twice(b1), diag2(w2), twice(b2), diag2(w3), twice(b3), diag2(w4), twice(freq), decay[None])
    out = pl.pallas_call(
        _filter_kernel,
        grid=(FILT_HALF // FILT_TM,),
        in_specs=[pl.BlockSpec((FILT_TM, 2 * FILT_KPAD), lambda i: (i, 0))] + [full(a) for a in args],
        out_specs=pl.BlockSpec((2, FILT_TM, 2 * HY_WIDTH), lambda i: (0, i, 0)),
        out_shape=jax.ShapeDtypeStruct((2, FILT_HALF, 2 * HY_WIDTH), F32),
        compiler_params=_cparams(("parallel",), 32),
        name="hy_filter",
    )(zfeat, *args)
    return out.reshape(1, SEQ, 2 * HY_WIDTH)


def _dft_tables():
    a = np.arange(DFT_R)
    w = np.exp(-2j * np.pi * np.outer(a, a) / DFT_R)
    tw = np.exp(-2j * np.pi * np.outer(a, a) / DFT_N)
    f32 = lambda x: jnp.asarray(x.astype(np.float32))
    f1 = np.concatenate([w.real[:, :DFT_HALF], w.imag[:, :DFT_HALF]], axis=0)
    m1 = np.concatenate([w.real[:DFT_HALF], w.imag[:DFT_HALF]], axis=1) / DFT_N
    c = w[None] * tw[:, :DFT_KB].T[:, None, :]
    blocks = lambda a, b, c, d: np.concatenate(
        [np.concatenate([a, b], axis=2), np.concatenate([c, d], axis=2)], axis=1)
    qu = blocks(c.real, -c.imag, c.imag, c.real)
    ct = np.swapaxes(c, 1, 2)
    gu = blocks(ct.real, ct.imag, -ct.imag, ct.real)
    ta = np.exp(-2j * np.pi * DFT_KB * np.outer(np.arange(DFT_R // DFT_KB), a) / DFT_N)
    ta = np.concatenate([ta.real, ta.imag], axis=1)[:, :, None] * np.ones((1, 1, LANES))
    bf = lambda x: f32(x).astype(BF16)
    return bf(f1), bf(qu), bf(gu), bf(m1), f32(ta)


DFT_KB = 8


def _dft_stage1(x_ref, f1_ref, s_scr):
    def body(i, carry):
        n2b = i * DFT_KB
        xs = jnp.concatenate([x_ref[0, pl.ds(n2b + u, DFT_HALF, stride=DFT_R), :] for u in range(DFT_KB)],
                             axis=1).astype(BF16)
        a = jnp.dot(f1_ref[...], xs, preferred_element_type=F32)
        for u in range(DFT_KB):
            s_scr[pl.ds(pl.multiple_of((n2b + u) * DFT_PITCH, 8), 2 * DFT_R), :] = a[:, u * LANES:(u + 1) * LANES]
        return carry

    lax.fori_loop(0, DFT_R // DFT_KB, body, 0)


def _dft_stage2_rows(s_scr, ta_ref, step, u):
    k1 = step * DFT_KB + u
    re = s_scr[pl.ds(k1, DFT_R, stride=DFT_PITCH), :]
    im = s_scr[pl.ds(DFT_R + k1, DFT_R, stride=DFT_PITCH), :]
    tr, ti = ta_ref[step, :DFT_R, :], ta_ref[step, DFT_R:, :]
    return jnp.concatenate([re * tr - im * ti, re * ti + im * tr], axis=0).astype(BF16)


def _dft_fwd_kernel(x_ref, f1_ref, qu_ref, ta_ref, o_ref, s_scr):
    step = pl.program_id(2)
    pl.when(step == 0)(functools.partial(_dft_stage1, x_ref, f1_ref, s_scr))
    for u in range(DFT_KB):
        a = _dft_stage2_rows(s_scr, ta_ref, step, u)
        o_ref[0, u] = jnp.dot(qu_ref[u], a, preferred_element_type=F32).astype(o_ref.dtype)


def _dft_fwd(x, f1, qu, ta, out_dtype):
    nb, _, c = x.shape
    full = lambda a: pl.BlockSpec(a.shape, lambda b, ct, s: (0,) * a.ndim)
    return pl.pallas_call(
        _dft_fwd_kernel,
        grid=(nb, c // LANES, DFT_R // DFT_KB),
        in_specs=[pl.BlockSpec((1, SEQ, LANES), lambda b, ct, s: (b, 0, ct)), full(f1), full(qu), full(ta)],
        out_specs=pl.BlockSpec((1, DFT_KB, 2 * DFT_R, LANES), lambda b, ct, s: (b, s, 0, ct)),
        out_shape=jax.ShapeDtypeStruct((nb, DFT_R, 2 * DFT_R, c), out_dtype),
        scratch_shapes=[pltpu.VMEM((DFT_R * DFT_PITCH, LANES), F32)],
        compiler_params=_cparams(("parallel", "parallel", "arbitrary"), 40),
        name="hy_dft_fwd",
    )(x, f1, qu, ta)


def _dft_stage1_inv(s_scr, m1_ref, o_ref):
    def body(i, carry):
        n2b = i * DFT_KB
        bc = jnp.concatenate(
            [s_scr[pl.ds(pl.multiple_of((n2b + u) * DFT_PITCH, 8), 2 * DFT_R), :] for u in range(DFT_KB)],
            axis=1).astype(BF16)
        y = jnp.dot(m1_ref[...], bc, preferred_element_type=F32)
        for u in range(DFT_KB):
            o_ref[0, pl.ds(n2b + u, DFT_HALF, stride=DFT_R), :] = y[:, u * LANES:(u + 1) * LANES]
        return carry

    lax.fori_loop(0, DFT_R // DFT_KB, body, 0)


def _dft_conv_kernel(x_ref, f1_ref, qu_ref, ta_ref, hf_ref, hb_ref, gu_ref, m1_ref, o_ref, s_scr):
    step = pl.program_id(2)
    pl.when(step == 0)(functools.partial(_dft_stage1, x_ref, f1_ref, s_scr))
    tr, ti = ta_ref[step, :DFT_R, :], ta_ref[step, DFT_R:, :]
    for u in range(DFT_KB):
        k1 = step * DFT_KB + u
        xk = jnp.dot(qu_ref[u], _dft_stage2_rows(s_scr, ta_ref, step, u), preferred_element_type=F32)
        xr, xi = xk[:DFT_R], xk[DFT_R:]
        hr = hf_ref[0, u, :DFT_R, :].astype(F32) + hb_ref[0, u, :DFT_R, :].astype(F32)
        hi = hf_ref[0, u, DFT_R:, :].astype(F32) - hb_ref[0, u, DFT_R:, :].astype(F32)
        y = jnp.concatenate([xr * hr - xi * hi, xr * hi + xi * hr], axis=0).astype(BF16)
        b = jnp.dot(gu_ref[u], y, preferred_element_type=F32)
        br, bi = b[:DFT_R], b[DFT_R:]
        s_scr[pl.ds(k1, DFT_R, stride=DFT_PITCH), :] = br * tr + bi * ti
        s_scr[pl.ds(DFT_R + k1, DFT_R, stride=DFT_PITCH), :] = bi * tr - br * ti
    pl.when(step == pl.num_programs(2) - 1)(functools.partial(_dft_stage1_inv, s_scr, m1_ref, o_ref))


def _dft_conv(x, hs, f1, qu, gu, m1, ta):
    b, _, c = x.shape
    nct = c // LANES
    full = lambda a: pl.BlockSpec(a.shape, lambda bi, ct, s: (0,) * a.ndim)
    return pl.pallas_call(
        _dft_conv_kernel,
        grid=(b, nct, DFT_R // DFT_KB),
        in_specs=[
            pl.BlockSpec((1, SEQ, LANES), lambda bi, ct, s: (bi, 0, ct)),
            full(f1), full(qu), full(ta),
            pl.BlockSpec((1, DFT_KB, 2 * DFT_R, LANES), lambda bi, ct, s: (0, s, 0, ct)),
            pl.BlockSpec((1, DFT_KB, 2 * DFT_R, LANES), lambda bi, ct, s: (0, s, 0, nct + ct)),
            full(gu), full(m1),
        ],
        out_specs=pl.BlockSpec((1, SEQ, LANES), lambda bi, ct, s: (bi, 0, ct)),
        out_shape=jax.ShapeDtypeStruct((b, SEQ, c), F32),
        scratch_shapes=[pltpu.VMEM((DFT_R * DFT_PITCH, LANES), F32)],
        compiler_params=_cparams(("parallel", "parallel", "arbitrary"), 48),
        name="hy_dft_conv",
    )(x, f1, qu, ta, hs, hs, gu, m1)


def _mix_out_body(x, ya, yh, z, x0, yc, skip_ref, gg_ref, w_ref, gpost_ref):
    a0, b0 = NA_WIDTH, NA_WIDTH + HY_WIDTH
    yb = x0 * (yh + skip_ref[...] * z)
    acc = jnp.dot(_rms(ya, gg_ref[:, :a0]).astype(BF16), w_ref[:a0, :], preferred_element_type=F32)
    acc += jnp.dot(_rms(yb, gg_ref[:, a0:b0]).astype(BF16), w_ref[a0:b0, :], preferred_element_type=F32)
    acc += jnp.dot(_rms(yc, gg_ref[:, b0:]).astype(BF16), w_ref[b0:, :], preferred_element_type=F32)
    return x + _rms(acc, gpost_ref[...])


def _xattn_body(x, gpre_ref, wq_ref, kv_ref, wo_ref, gpost_ref):
    h = _rms(x, gpre_ref[...]).astype(BF16)
    q = jnp.dot(h, wq_ref[...], preferred_element_type=F32) * (XA_HEAD_DIM ** -0.5)
    outs = []
    for hd in range(XA_HEADS):
        c0 = hd * XA_HEAD_DIM
        qh = q[:, c0:c0 + XA_HEAD_DIM].astype(BF16)
        kh = kv_ref[0, :, c0:c0 + XA_HEAD_DIM]
        vh = kv_ref[0, :, D_MODEL + c0:D_MODEL + c0 + XA_HEAD_DIM]
        s = lax.dot_general(qh, kh, (((1,), (1,)), ((), ())), preferred_element_type=F32)
        p = jnp.exp(s - jnp.max(s, axis=-1, keepdims=True))
        l = jnp.sum(p, axis=-1, keepdims=True)
        outs.append(jnp.dot(p.astype(BF16), vh, preferred_element_type=F32) / l)
    o = jnp.concatenate(outs, axis=-1).astype(BF16)
    xa = jnp.dot(o, wo_ref[...], preferred_element_type=F32)
    return x + _rms(xa, gpost_ref[...])


POST_TM = 512


def _post_mixer_kernel(x_ref, ya_ref, yh_ref, z_ref, x0_ref, yc_ref, skip_ref, gg_ref, wout_ref, gmix_ref,
                       gxpre_ref, wq_ref, kv_ref, wo_ref, gxpost_ref, gfpre_ref, fwin_ref, fwout_ref, gfpost_ref,
                       o_ref):
    x = _mix_out_body(x_ref[...], ya_ref[...], yh_ref[...], z_ref[...], x0_ref[...], yc_ref[...],
                      skip_ref, gg_ref, wout_ref, gmix_ref)
    x = _xattn_body(x, gxpre_ref, wq_ref, kv_ref, wo_ref, gxpost_ref)
    o_ref[...] = _ffn_body(x, gfpre_ref, fwin_ref, fwout_ref, gfpost_ref)


def _post_mixer(x, ya, yh, z, x0, yc, skip, gg, w_out, g_mix, g_xpre, wq, kv, wo, g_xpost, g_fpre, fw_in, fw_out,
                g_fpost):
    n = x.shape[0]
    per_batch = SEQ // POST_TM
    tok = lambda width: pl.BlockSpec((POST_TM, width), lambda i: (i, 0))
    const = lambda a: pl.BlockSpec(a.shape, lambda i: (0,) * a.ndim, pipeline_mode=pl.Buffered(1))
    kv_spec = pl.BlockSpec((1,) + kv.shape[1:], lambda i: (i // per_batch, 0, 0))
    return pl.pallas_call(
        _post_mixer_kernel,
        grid=(n // POST_TM,),
        in_specs=[tok(D_MODEL), tok(NA_WIDTH), tok(HY_WIDTH), tok(HY_WIDTH), tok(HY_WIDTH), tok(DIL_WIDTH),
                  const(skip), const(gg), const(w_out), const(g_mix),
                  const(g_xpre), const(wq), kv_spec, const(wo), const(g_xpost),
                  const(g_fpre), const(fw_in), const(fw_out), const(g_fpost)],
        out_specs=tok(D_MODEL),
        out_shape=jax.ShapeDtypeStruct((n, D_MODEL), F32),
        compiler_params=_cparams(("parallel",), 56),
        name="post_mixer",
    )(x, ya, yh, z, x0, yc, skip, gg, w_out, g_mix, g_xpre, wq, kv, wo, g_xpost, g_fpre, fw_in, fw_out, g_fpost)


def kernel(x, mem, norm_g, w_in, w_out, na_rpb, t5_table, hy_conv_w, hy_conv_b, hy_f_w1, hy_f_b1, hy_f_w2, hy_f_b2, hy_f_w3, hy_f_b3, hy_f_w4, hy_freq, hy_decay, hy_skip, xa_wq, xa_wkv, xa_wo, ffn_w_in, ffn_w_out):
    bsz, seq, d = x.shape
    assert (seq, d) == (SEQ, D_MODEL)
    depth = norm_g.shape[0]
    mem_len = mem.shape[1]
    n = bsz * seq
    scale = HEAD_DIM ** -0.5
    wa, wb, wc = 3 * NA_WIDTH, 3 * HY_WIDTH, 3 * DIL_WIDTH
    colscale = np.ones((1, wa + wb + wc), np.float32)
    colscale[:, :NA_WIDTH] = scale
    colscale[:, wa + wb:wa + wb + DIL_WIDTH] = scale
    colscale = jnp.asarray(colscale)

    zfeat = _filter_features()
    f1, qu, gu, m1, ta = _dft_tables()
    dil_bias = _dil_bias(t5_table)

    xf = x.reshape(n, d)
    memf = mem.reshape(bsz * mem_len, d)
    for l in range(depth):
        g = norm_g[l][:, None, :]
        xf, pa, pb, pc = _pre_mixer(xf, g[0], ffn_w_in[l, 0].astype(BF16), ffn_w_out[l, 0].astype(BF16), g[1],
                                    g[2], w_in[l].astype(BF16), colscale)
        pa = pa.reshape(bsz, seq, wa)
        pb = pb.reshape(bsz, seq, wb)
        pc = pc.reshape(bsz, seq, wc)
        ya = _na_attention(pa, _na_tz(na_rpb[l]))
        z, x0 = _conv_gate(pb, hy_conv_w[l], hy_conv_b[l][None])
        hfilt = _hyena_filters(zfeat, hy_f_w1[l], hy_f_b1[l], hy_f_w2[l], hy_f_b2[l], hy_f_w3[l], hy_f_b3[l],
                               hy_f_w4[l], hy_freq[l], hy_decay[l])
        yh = _dft_conv(z, _dft_fwd(hfilt, f1, qu, ta, BF16), f1, qu, gu, m1, ta)
        yc = _dil_attention(pc, dil_bias)
        kv = _norm_mm(memf, g[6], xa_wkv[l].astype(BF16), mem_len, BF16).reshape(bsz, mem_len, 2 * D_MODEL)
        xf = _post_mixer(xf, ya.reshape(n, NA_WIDTH), yh.reshape(n, HY_WIDTH), z.reshape(n, HY_WIDTH),
                         x0.reshape(n, HY_WIDTH), yc.reshape(n, DIL_WIDTH), hy_skip[l][None], g[3],
                         w_out[l].astype(BF16), g[4],
                         g[5], xa_wq[l].astype(BF16), kv, xa_wo[l].astype(BF16), g[7],
                         g[8], ffn_w_in[l, 1].astype(BF16), ffn_w_out[l, 1].astype(BF16), g[9])
    return xf.reshape(bsz, seq, d)
```

```python
import functools
import math

import numpy as np
import jax
import jax.numpy as jnp
from jax import lax
from jax.experimental import pallas as pl
from jax.experimental.pallas import tpu as pltpu

F32 = jnp.float32
BF16 = jnp.bfloat16

D_MODEL = 1024
SEQ = 8192
GRID_W = 64
HEAD_DIM = 64
NA_WIDTH = 384
HY_WIDTH = 256
DIL_WIDTH = 384
NA_ROWS = 8
NA_COLS = 16
HY_BANDS = 16
DIL_CONFIGS = ((128, 1), (512, 4), (2048, 16))
T5_BUCKETS = 32
T5_MAX_DIST = 1024
XA_HEADS = 4
XA_HEAD_DIM = 256
FFN_HIDDEN = 2816
RMS_EPS = 1e-6
NEG_INF = -1e30
LOG2E = math.log2(math.e)

LANES = 128
MIB = 1024 * 1024

NA_QROWS = 8
NA_KROWS = 16
NA_BQ = NA_QROWS * GRID_W
NA_GROUPS = 4
NA_GQ = GRID_W // NA_GROUPS
NA_GK = 2 * NA_GQ
DIL_HALF = 64
DIL_BQ = 128
DIL_UNROLL = 8
DIL_BK = DIL_BQ + 2 * DIL_HALF
DFT_R = 128
DFT_HALF = 64
DFT_N = DFT_R * DFT_R
DFT_PITCH = 264


def _cparams(sem, vmem_mib):
    return pltpu.CompilerParams(dimension_semantics=sem, vmem_limit_bytes=vmem_mib * MIB)


def _rms(x, g):
    return x * lax.rsqrt(jnp.mean(x * x, axis=-1, keepdims=True) + RMS_EPS) * g


FFN_TM = 512
FFN_TH = 256


def _ffn_body(x, gpre_ref, win_ref, wout_ref, gpost_ref):
    h = _rms(x, gpre_ref[...]).astype(BF16)
    acc = jnp.zeros(x.shape, F32)
    for j in range(FFN_HIDDEN // FFN_TH):
        c0 = j * FFN_TH
        gate = jnp.dot(h, win_ref[:, c0:c0 + FFN_TH], preferred_element_type=F32)
        up = jnp.dot(h, win_ref[:, FFN_HIDDEN + c0:FFN_HIDDEN + c0 + FFN_TH], preferred_element_type=F32)
        act = (gate * jax.nn.sigmoid(gate) * up).astype(BF16)
        acc += jnp.dot(act, wout_ref[c0:c0 + FFN_TH, :], preferred_element_type=F32)
    return x + 0.5 * _rms(acc, gpost_ref[...])


def _pre_mixer_kernel(x_ref, gfpre_ref, fwin_ref, fwout_ref, gfpost_ref, gmix_ref, w_ref, cs_ref,
                      o_ref, pa_ref, pb_ref, pc_ref):
    x = _ffn_body(x_ref[...], gfpre_ref, fwin_ref, fwout_ref, gfpost_ref)
    o_ref[...] = x
    h = _rms(x, gmix_ref[...]).astype(BF16)
    c0 = 0
    for ref in (pa_ref, pb_ref, pc_ref):
        c1 = c0 + ref.shape[1]
        ref[...] = jnp.dot(h, w_ref[:, c0:c1], preferred_element_type=F32) * cs_ref[:, c0:c1]
        c0 = c1


def _pre_mixer(x, g_fpre, fw_in, fw_out, g_fpost, g_mix, w, colscale):
    n = x.shape[0]
    widths = (3 * NA_WIDTH, 3 * HY_WIDTH, 3 * DIL_WIDTH)
    tok = lambda width: pl.BlockSpec((FFN_TM, width), lambda i: (i, 0))
    const = lambda a: pl.BlockSpec(a.shape, lambda i: (0,) * a.ndim, pipeline_mode=pl.Buffered(1))
    return pl.pallas_call(
        _pre_mixer_kernel,
        grid=(n // FFN_TM,),
        in_specs=[tok(D_MODEL), const(g_fpre), const(fw_in), const(fw_out), const(g_fpost),
                  const(g_mix), const(w), const(colscale)],
        out_specs=[tok(D_MODEL)] + [tok(wd) for wd in widths],
        out_shape=[jax.ShapeDtypeStruct((n, wd), F32) for wd in (D_MODEL,) + widths],
        compiler_params=_cparams(("parallel",), 56),
        name="pre_mixer",
    )(x, g_fpre, fw_in, fw_out, g_fpost, g_mix, w, colscale)


def _norm_mm_kernel(x_ref, g_ref, w_ref, o_ref):
    h = _rms(x_ref[...], g_ref[...]).astype(BF16)
    o_ref[...] = jnp.dot(h, w_ref[...], preferred_element_type=F32).astype(o_ref.dtype)


def _norm_mm(x, g, w, tm, out_dtype):
    n, ncol = x.shape[0], w.shape[1]
    return pl.pallas_call(
        _norm_mm_kernel,
        grid=(n // tm,),
        in_specs=[
            pl.BlockSpec((tm, D_MODEL), lambda i: (i, 0)),
            pl.BlockSpec((1, D_MODEL), lambda i: (0, 0)),
            pl.BlockSpec((D_MODEL, ncol), lambda i: (0, 0)),
        ],
        out_specs=pl.BlockSpec((tm, ncol), lambda i: (i, 0)),
        out_shape=jax.ShapeDtypeStruct((n, ncol), out_dtype),
        compiler_params=_cparams(("parallel",), 32),
        name="norm_mm",
    )(x, g, w)


def _softmax_pv(q, k, v, bias):
    s = lax.dot_general(q, k, (((1,), (1,)), ((), ())), preferred_element_type=F32) + bias
    m = jnp.max(s, axis=-1, keepdims=True)
    p = jnp.exp2(s - m).astype(BF16)
    o = jnp.dot(p, jnp.concatenate([v, jnp.ones_like(v)], axis=1), preferred_element_type=F32)
    return o[:, :LANES], m, o[:, LANES:]


def _pair_attention(q, k, v, bias, stack_heads):
    bq = q.shape[0]
    lane = lax.broadcasted_iota(jnp.int32, (1, LANES), 1)
    first = lane < HEAD_DIM
    q0 = jnp.where(first, q, 0.0).astype(BF16)
    q1 = jnp.where(first, 0.0, q).astype(BF16)
    if stack_heads:
        res = _softmax_pv(jnp.concatenate([q0, q1], axis=0), k, v, bias)
        return tuple(jnp.where(first, a[:bq], a[bq:]) for a in res)
    res0 = _softmax_pv(q0, k, v, bias[:bq])
    res1 = _softmax_pv(q1, k, v, bias[bq:])
    return tuple(jnp.where(first, a, b) for a, b in zip(res0, res1))


def _na_key_col0(g):
    return int(np.clip(g * NA_GQ - NA_COLS // 2, 0, GRID_W - NA_GK))


def _na_fill_bias(tz_ref, bias_scr, j):
    nrows = SEQ // GRID_W
    start_row = int(np.clip(j * NA_QROWS - NA_ROWS // 2, 0, nrows - NA_KROWS))
    neg = jnp.full((NA_GQ, NA_GK), NEG_INF, F32)
    per_tile = LANES // NA_GK
    for g in range(NA_GROUPS):
        for hh in range(2):
            for rl in range(NA_QROWS):
                r = j * NA_QROWS + rl
                rs = int(np.clip(r - NA_ROWS // 2, 0, nrows - NA_ROWS))
                r0 = (hh * NA_QROWS + rl) * NA_GQ
                for kt in range(NA_KROWS // per_tile):
                    tiles = []
                    for kl in range(kt * per_tile, (kt + 1) * per_tile):
                        kr = start_row + kl
                        tiles.append(tz_ref[hh, kr - r + NA_ROWS - 1, g] if rs <= kr < rs + NA_ROWS else neg)
                    bias_scr[g, r0:r0 + NA_GQ, kt * LANES:(kt + 1) * LANES] = jnp.concatenate(tiles, axis=1)


def _na_kernel(q_ref, k_ref, v_ref, tz_ref, o_ref, bias_scr):
    j = pl.program_id(2)
    nblk = SEQ // NA_BQ
    for case_j in (0, 1, nblk - 1):
        pl.when(j == case_j)(functools.partial(_na_fill_bias, tz_ref, bias_scr, case_j))

    nrows = SEQ // GRID_W
    start_row = jnp.clip(j * NA_QROWS - NA_ROWS // 2, 0, nrows - NA_KROWS)
    start = start_row * GRID_W
    for g in range(NA_GROUPS):
        kc0 = _na_key_col0(g)
        krows = [pl.ds(pl.multiple_of(start + kl * GRID_W + kc0, 8), NA_GK) for kl in range(NA_KROWS)]
        qrows = [slice(rl * GRID_W + g * NA_GQ, rl * GRID_W + (g + 1) * NA_GQ) for rl in range(NA_QROWS)]
        k = jnp.concatenate([k_ref[0, r, :] for r in krows], axis=0).astype(BF16)
        v = jnp.concatenate([v_ref[0, r, :] for r in krows], axis=0).astype(BF16)
        q = jnp.concatenate([q_ref[0, r, :] for r in qrows], axis=0)
        o, _, l = _pair_attention(q, k, v, bias_scr[g], stack_heads=True)
        o = o / l
        for rl, r in enumerate(qrows):
            o_ref[0, r, :] = o[rl * NA_GQ:(rl + 1) * NA_GQ]


def _na_attention(pa, tzg):
    b = pa.shape[0]
    npair = NA_WIDTH // LANES
    nblk = SEQ // NA_BQ
    return pl.pallas_call(
        _na_kernel,
        grid=(b, npair, nblk),
        in_specs=[
            pl.BlockSpec((1, NA_BQ, LANES), lambda bi, p, j: (bi, j, p)),
            pl.BlockSpec((1, SEQ, LANES), lambda bi, p, j: (bi, 0, npair + p)),
            pl.BlockSpec((1, SEQ, LANES), lambda bi, p, j: (bi, 0, 2 * npair + p)),
            pl.BlockSpec((2,) + tzg.shape[1:], lambda bi, p, j: (p, 0, 0, 0, 0)),
        ],
        out_specs=pl.BlockSpec((1, NA_BQ, LANES), lambda bi, p, j: (bi, j, p)),
        out_shape=jax.ShapeDtypeStruct((b, SEQ, NA_WIDTH), F32),
        scratch_shapes=[pltpu.VMEM((NA_GROUPS, 2 * NA_QROWS * NA_GQ, NA_KROWS * NA_GK), F32)],
        compiler_params=_cparams(("parallel", "parallel", "arbitrary"), 48),
        name="na_attn",
    )(pa, pa, pa, tzg)


def _na_tz(rpb):
    col = np.arange(GRID_W)
    cs = np.clip(col - NA_COLS // 2, 0, GRID_W - NA_COLS)
    col_ok = (col[None, :] >= cs[:, None]) & (col[None, :] < cs[:, None] + NA_COLS)
    period = GRID_W + 1
    half = NA_COLS - 1
    filler = jnp.zeros(rpb.shape[:2] + (period - (2 * half + 1),), rpb.dtype)
    v = jnp.concatenate([rpb[..., half:], filler, rpb[..., :half]], axis=-1)
    t = jnp.tile(v, (1, 1, GRID_W))[..., :GRID_W * GRID_W].reshape(rpb.shape[:2] + (GRID_W, GRID_W))
    t = jnp.where(col_ok[None, None], t * LOG2E, NEG_INF)
    groups = []
    for g in range(NA_GROUPS):
        kc0 = _na_key_col0(g)
        assert col_ok[g * NA_GQ:(g + 1) * NA_GQ, :kc0].sum() == 0 and col_ok[g * NA_GQ:(g + 1) * NA_GQ, kc0 + NA_GK:].sum() == 0
        groups.append(t[:, :, g * NA_GQ:(g + 1) * NA_GQ, kc0:kc0 + NA_GK])
    return jnp.stack(groups, axis=2)


def _dil_kernel(q_ref, k_ref, v_ref, bias_ref, o_ref, m_scr, l_scr):
    order = sorted(range(len(DIL_CONFIGS)), key=lambda c: -DIL_CONFIGS[c][1])
    for c in order:
        dil = DIL_CONFIGS[c][1]
        m_len = SEQ // dil
        nblk = m_len // DIL_BQ

        def body(it, carry, c=c, dil=dil, m_len=m_len, nblk=nblk):
            g = it // nblk
            n = it % nblk
            pos = n * DIL_BQ

            def rows(p0, size):
                return pl.ds(g + p0 * dil, size, stride=dil) if dil > 1 else pl.ds(p0, size)

            lo = jnp.maximum(pos - DIL_HALF, 0)
            hi = jnp.minimum(pos + DIL_BQ, m_len - DIL_HALF)
            qrows = rows(pos, DIL_BQ)
            k = jnp.concatenate(
                [k_ref[0, rows(lo, DIL_HALF), :], k_ref[0, qrows, :], k_ref[0, rows(hi, DIL_HALF), :]],
                axis=0).astype(BF16)
            v = jnp.concatenate(
                [v_ref[0, rows(lo, DIL_HALF), :], v_ref[0, qrows, :], v_ref[0, rows(hi, DIL_HALF), :]],
                axis=0).astype(BF16)
            edge_case = jnp.where(n == 0, 0, jnp.where(n == nblk - 1, 2, 1))
            o, m, l = _pair_attention(q_ref[0, qrows, :], k, v, bias_ref[c, 0, edge_case], stack_heads=True)
            if c == order[0]:
                o_ref[0, qrows, :] = o
                m_scr[qrows, :] = m
                l_scr[qrows, :] = l
                return carry
            m_old = m_scr[qrows, :]
            m_new = jnp.maximum(m_old, m)
            a_old = jnp.exp2(m_old - m_new)
            a_new = jnp.exp2(m - m_new)
            o_new = o_ref[0, qrows, :] * a_old + o * a_new
            l_new = l_scr[qrows, :] * a_old + l * a_new
            if c == order[-1]:
                o_ref[0, qrows, :] = o_new / l_new
            else:
                o_ref[0, qrows, :] = o_new
                l_scr[qrows, :] = l_new
                m_scr[qrows, :] = m_new
            return carry

        lax.fori_loop(0, SEQ // DIL_BQ, body, 0, unroll=DIL_UNROLL)


def _dil_attention(pc, bias):
    b = pc.shape[0]
    npair = DIL_WIDTH // LANES
    assert all(SEQ // dil // DIL_BQ >= 2 for _, dil in DIL_CONFIGS)
    return pl.pallas_call(
        _dil_kernel,
        grid=(b, npair),
        in_specs=[
            pl.BlockSpec((1, SEQ, LANES), lambda bi, p: (bi, 0, p)),
            pl.BlockSpec((1, SEQ, LANES), lambda bi, p: (bi, 0, npair + p)),
            pl.BlockSpec((1, SEQ, LANES), lambda bi, p: (bi, 0, 2 * npair + p)),
            pl.BlockSpec((len(DIL_CONFIGS), 1, 3, 2 * DIL_BQ, DIL_BK), lambda bi, p: (0, p, 0, 0, 0)),
        ],
        out_specs=pl.BlockSpec((1, SEQ, LANES), lambda bi, p: (bi, 0, p)),
        out_shape=jax.ShapeDtypeStruct((b, SEQ, DIL_WIDTH), F32),
        scratch_shapes=[pltpu.VMEM((SEQ, LANES), F32), pltpu.VMEM((SEQ, LANES), F32)],
        compiler_params=_cparams(("parallel", "arbitrary"), 56),
        name="dil_attn",
    )(pc, pc, pc, bias)


def _t5_bucket(rel):
    half = T5_BUCKETS // 2
    exact = half // 2
    n = np.abs(rel)
    far = exact + (np.log(np.maximum(n, 1) / exact) / math.log(T5_MAX_DIST / exact) * (half - exact)).astype(np.int32)
    far = np.minimum(far, half - 1)
    return (np.where(rel > 0, half, 0) + np.where(n < exact, n, far)).astype(np.int32)


def _dil_bias(t5_table):
    rel = np.arange(-DIL_HALF, DIL_HALF + 1)
    period = DIL_BK + 1
    nh = t5_table.shape[1]
    out = []
    for _, dil in DIL_CONFIGS:
        vals = t5_table[_t5_bucket(dil * rel)].T
        v = jnp.concatenate([vals, jnp.full((nh, period - vals.shape[1]), NEG_INF, vals.dtype)], axis=1)
        out.append(jnp.tile(v, (1, DIL_BQ))[:, :DIL_BQ * DIL_BK].reshape(nh, DIL_BQ, DIL_BK))
    t = (jnp.stack(out) * LOG2E).reshape(len(DIL_CONFIGS), nh // 2, 1, 2 * DIL_BQ, DIL_BK)
    kcol = np.arange(DIL_BK)
    edge = np.stack([kcol < DIL_HALF, np.zeros_like(kcol, bool), kcol >= DIL_HALF + DIL_BQ])
    return jnp.where(edge[None, None, :, None, :], NEG_INF, t)


CONV_TM = 1024


def _conv_gate_kernel(p_ref, prev_ref, next_ref, w_ref, b_ref, z_ref, x0_ref):
    i = pl.program_id(1)
    p = p_ref[0]
    tm = p.shape[0]
    first = jnp.where(i > 0, prev_ref[0, 7:8, :], 0.0)
    last = jnp.where(i < pl.num_programs(1) - 1, next_ref[0, 0:1, :], 0.0)
    row = lax.broadcasted_iota(jnp.int32, (tm, 1), 0)
    up = jnp.where(row == 0, first, pltpu.roll(p, 1, axis=0))
    dn = jnp.where(row == tm - 1, last, pltpu.roll(p, tm - 1, axis=0))
    uc = up * w_ref[0:1, :] + p * w_ref[1:2, :] + dn * w_ref[2:3, :] + b_ref[...]
    z_ref[0] = uc[:, HY_WIDTH:2 * HY_WIDTH] * uc[:, :HY_WIDTH]
    x0_ref[0] = uc[:, 2 * HY_WIDTH:]


def _conv_gate(pb, w, bvec):
    b = pb.shape[0]
    wb = 3 * HY_WIDTH
    nt = SEQ // CONV_TM
    r8 = CONV_TM // 8
    return pl.pallas_call(
        _conv_gate_kernel,
        grid=(b, nt),
        in_specs=[
            pl.BlockSpec((1, CONV_TM, wb), lambda bi, i: (bi, i, 0)),
            pl.BlockSpec((1, 8, wb), lambda bi, i: (bi, jnp.maximum(i * r8 - 1, 0), 0)),
            pl.BlockSpec((1, 8, wb), lambda bi, i: (bi, jnp.minimum((i + 1) * r8, SEQ // 8 - 1), 0)),
            pl.BlockSpec((3, wb), lambda bi, i: (0, 0)),
            pl.BlockSpec((1, wb), lambda bi, i: (0, 0)),
        ],
        out_specs=[
            pl.BlockSpec((1, CONV_TM, HY_WIDTH), lambda bi, i: (bi, i, 0)),
            pl.BlockSpec((1, CONV_TM, HY_WIDTH), lambda bi, i: (bi, i, 0)),
        ],
        out_shape=[jax.ShapeDtypeStruct((b, SEQ, HY_WIDTH), F32)] * 2,
        compiler_params=_cparams(("parallel", "parallel"), 48),
        name="hy_conv_gate",
    )(pb, pb, pb, w, bvec)


FILT_TM = 1024
FILT_KPAD = 128
FILT_HALF = SEQ // 2


def _filter_kernel(z_ref, w1_ref, b1_ref, w2_ref, b2_ref, w3_ref, b3_ref, w4_ref, fr_ref, dec_ref, h_ref):
    hp = lax.Precision.HIGHEST
    z = z_ref[...]
    fr = fr_ref[...]
    h = jnp.sin(fr * (jnp.dot(z, w1_ref[...], precision=hp, preferred_element_type=F32) + b1_ref[...]))
    h = jnp.sin(fr * (jnp.dot(h, w2_ref[...], precision=hp, preferred_element_type=F32) + b2_ref[...]))
    h = jnp.sin(fr * (jnp.dot(h, w3_ref[...], precision=hp, preferred_element_type=F32) + b3_ref[...]))
    h = jnp.dot(h, w4_ref[...], precision=hp, preferred_element_type=F32)
    wout = 2 * HY_WIDTH
    dec = jnp.abs(dec_ref[...])
    lo = h[:, :wout] * jnp.exp(-z[:, 0:1] * dec)
    hi = h[:, wout:] * jnp.exp(-z[:, FILT_KPAD:FILT_KPAD + 1] * dec)
    row = pl.program_id(0) * FILT_TM + lax.broadcasted_iota(jnp.int32, (FILT_TM, 1), 0)
    col = lax.broadcasted_iota(jnp.int32, (1, wout), 1)
    h_ref[0] = jnp.where((row == 0) & (col >= HY_WIDTH), 0.0, lo)
    h_ref[1] = hi


def _filter_features():
    t = jnp.linspace(0.0, 1.0, SEQ, dtype=F32)[:, None]
    bands = jnp.linspace(1e-4, HY_BANDS - 1, HY_BANDS, dtype=F32)[None, :]
    ang = (2.0 * math.pi / SEQ) * jnp.arange(SEQ, dtype=F32)[:, None] * bands
    z = jnp.concatenate([t, jnp.cos(ang), -jnp.sin(ang)], axis=-1)
    z = jnp.pad(z, ((0, 0), (0, FILT_KPAD - z.shape[1])))
    return jnp.concatenate([z[:FILT_HALF], z[FILT_HALF:]], axis=1)


def _hyena_filters(zfeat, w1, b1, w2, b2, w3, b3, w4, freq, decay):
    def diag2(w):
        zero = jnp.zeros_like(w)
        return jnp.concatenate([jnp.concatenate([w, zero], axis=1), jnp.concatenate([zero, w], axis=1)], axis=0)

    twice = lambda v: jnp.concatenate([v, v])[None]
    w1p = jnp.pad(w1, ((0, FILT_KPAD - w1.shape[0]), (0, 0)))
    full = lambda a: pl.BlockSpec(a.shape, lambda i: (0,) * a.ndim)
    args = (diag2(w1p), twice(b1), diag2(w2), twice(b2), diag2(w3), twice(b3), diag2(w4), twice(freq), decay[None])
    out = pl.pallas_call(
        _filter_kernel,
        grid=(FILT_HALF // FILT_TM,),
        in_specs=[pl.BlockSpec((FILT_TM, 2 * FILT_KPAD), lambda i: (i, 0))] + [full(a) for a in args],
        out_specs=pl.BlockSpec((2, FILT_TM, 2 * HY_WIDTH), lambda i: (0, i, 0)),
        out_shape=jax.ShapeDtypeStruct((2, FILT_HALF, 2 * HY_WIDTH), F32),
        compiler_params=_cparams(("parallel",), 32),
        name="hy_filter",
    )(zfeat, *args)
    return out.reshape(1, SEQ, 2 * HY_WIDTH)


def _dft_tables():
    a = np.arange(DFT_R)
    w = np.exp(-2j * np.pi * np.outer(a, a) / DFT_R)
    tw = np.exp(-2j * np.pi * np.outer(a, a) / DFT_N)
    f32 = lambda x: jnp.asarray(x.astype(np.float32))
    f1 = np.concatenate([w.real[:, :DFT_HALF], w.imag[:, :DFT_HALF]], axis=0)
    m1 = np.concatenate([w.real[:DFT_HALF], w.imag[:DFT_HALF]], axis=1) / DFT_N
    c = w[None] * tw[:, :DFT_KB].T[:, None, :]
    blocks = lambda a, b, c, d: np.concatenate(
        [np.concatenate([a, b], axis=2), np.concatenate([c, d], axis=2)], axis=1)
    qu = blocks(c.real, -c.imag, c.imag, c.real)
    ct = np.swapaxes(c, 1, 2)
    gu = blocks(ct.real, ct.imag, -ct.imag, ct.real)
    ta = np.exp(-2j * np.pi * DFT_KB * np.outer(np.arange(DFT_R // DFT_KB), a) / DFT_N)
    ta = np.concatenate([ta.real, ta.imag], axis=1)[:, :, None] * np.ones((1, 1, LANES))
    bf = lambda x: f32(x).astype(BF16)
    return bf(f1), bf(qu), bf(gu), bf(m1), f32(ta)


DFT_KB = 8


def _dft_stage1(x_ref, f1_ref, s_scr):
    def body(i, carry):
        n2b = i * DFT_KB
        xs = jnp.concatenate([x_ref[0, pl.ds(n2b + u, DFT_HALF, stride=DFT_R), :] for u in range(DFT_KB)],
                             axis=1).astype(BF16)
        a = jnp.dot(f1_ref[...], xs, preferred_element_type=F32)
        for u in range(DFT_KB):
            s_scr[pl.ds(pl.multiple_of((n2b + u) * DFT_PITCH, 8), 2 * DFT_R), :] = a[:, u * LANES:(u + 1) * LANES]
        return carry

    lax.fori_loop(0, DFT_R // DFT_KB, body, 0)


def _dft_stage2_rows(s_scr, ta_ref, step, u):
    k1 = step * DFT_KB + u
    re = s_scr[pl.ds(k1, DFT_R, stride=DFT_PITCH), :]
    im = s_scr[pl.ds(DFT_R + k1, DFT_R, stride=DFT_PITCH), :]
    tr, ti = ta_ref[step, :DFT_R, :], ta_ref[step, DFT_R:, :]
    return jnp.concatenate([re * tr - im * ti, re * ti + im * tr], axis=0).astype(BF16)


def _dft_fwd_kernel(x_ref, f1_ref, qu_ref, ta_ref, o_ref, s_scr):
    step = pl.program_id(2)
    pl.when(step == 0)(functools.partial(_dft_stage1, x_ref, f1_ref, s_scr))
    for u in range(DFT_KB):
        a = _dft_stage2_rows(s_scr, ta_ref, step, u)
        o_ref[0, u] = jnp.dot(qu_ref[u], a, preferred_element_type=F32).astype(o_ref.dtype)


def _dft_fwd(x, f1, qu, ta, out_dtype):
    nb, _, c = x.shape
    full = lambda a: pl.BlockSpec(a.shape, lambda b, ct, s: (0,) * a.ndim)
    return pl.pallas_call(
        _dft_fwd_kernel,
        grid=(nb, c // LANES, DFT_R // DFT_KB),
        in_specs=[pl.BlockSpec((1, SEQ, LANES), lambda b, ct, s: (b, 0, ct)), full(f1), full(qu), full(ta)],
        out_specs=pl.BlockSpec((1, DFT_KB, 2 * DFT_R, LANES), lambda b, ct, s: (b, s, 0, ct)),
        out_shape=jax.ShapeDtypeStruct((nb, DFT_R, 2 * DFT_R, c), out_dtype),
        scratch_shapes=[pltpu.VMEM((DFT_R * DFT_PITCH, LANES), F32)],
        compiler_params=_cparams(("parallel", "parallel", "arbitrary"), 40),
        name="hy_dft_fwd",
    )(x, f1, qu, ta)


def _dft_stage1_inv(s_scr, m1_ref, o_ref):
    def body(i, carry):
        n2b = i * DFT_KB
        bc = jnp.concatenate(
            [s_scr[pl.ds(pl.multiple_of((n2b + u) * DFT_PITCH, 8), 2 * DFT_R), :] for u in range(DFT_KB)],
            axis=1).astype(BF16)
        y = jnp.dot(m1_ref[...], bc, preferred_element_type=F32)
        for u in range(DFT_KB):
            o_ref[0, pl.ds(n2b + u, DFT_HALF, stride=DFT_R), :] = y[:, u * LANES:(u + 1) * LANES]
        return carry

    lax.fori_loop(0, DFT_R // DFT_KB, body, 0)


def _dft_conv_kernel(x_ref, f1_ref, qu_ref, ta_ref, hf_ref, hb_ref, gu_ref, m1_ref, o_ref, s_scr):
    step = pl.program_id(2)
    pl.when(step == 0)(functools.partial(_dft_stage1, x_ref, f1_ref, s_scr))
    tr, ti = ta_ref[step, :DFT_R, :], ta_ref[step, DFT_R:, :]
    for u in range(DFT_KB):
        k1 = step * DFT_KB + u
        xk = jnp.dot(qu_ref[u], _dft_stage2_rows(s_scr, ta_ref, step, u), preferred_element_type=F32)
        xr, xi = xk[:DFT_R], xk[DFT_R:]
        hr = hf_ref[0, u, :DFT_R, :].astype(F32) + hb_ref[0, u, :DFT_R, :].astype(F32)
        hi = hf_ref[0, u, DFT_R:, :].astype(F32) - hb_ref[0, u, DFT_R:, :].astype(F32)
        y = jnp.concatenate([xr * hr - xi * hi, xr * hi + xi * hr], axis=0).astype(BF16)
        b = jnp.dot(gu_ref[u], y, preferred_element_type=F32)
        br, bi = b[:DFT_R], b[DFT_R:]
        s_scr[pl.ds(k1, DFT_R, stride=DFT_PITCH), :] = br * tr + bi * ti
        s_scr[pl.ds(DFT_R + k1, DFT_R, stride=DFT_PITCH), :] = bi * tr - br * ti
    pl.when(step == pl.num_programs(2) - 1)(functools.partial(_dft_stage1_inv, s_scr, m1_ref, o_ref))


def _dft_conv(x, hs, f1, qu, gu, m1, ta):
    b, _, c = x.shape
    nct = c // LANES
    full = lambda a: pl.BlockSpec(a.shape, lambda bi, ct, s: (0,) * a.ndim)
    return pl.pallas_call(
        _dft_conv_kernel,
        grid=(b, nct, DFT_R // DFT_KB),
        in_specs=[
            pl.BlockSpec((1, SEQ, LANES), lambda bi, ct, s: (bi, 0, ct)),
            full(f1), full(qu), full(ta),
            pl.BlockSpec((1, DFT_KB, 2 * DFT_R, LANES), lambda bi, ct, s: (0, s, 0, ct)),
            pl.BlockSpec((1, DFT_KB, 2 * DFT_R, LANES), lambda bi, ct, s: (0, s, 0, nct + ct)),
            full(gu), full(m1),
        ],
        out_specs=pl.BlockSpec((1, SEQ, LANES), lambda bi, ct, s: (bi, 0, ct)),
        out_shape=jax.ShapeDtypeStruct((b, SEQ, c), F32),
        scratch_shapes=[pltpu.VMEM((DFT_R * DFT_PITCH, LANES), F32)],
        compiler_params=_cparams(("parallel", "parallel", "arbitrary"), 48),
        name="hy_dft_conv",
    )(x, f1, qu, ta, hs, hs, gu, m1)


def _mix_out_body(x, ya, yh, z, x0, yc, skip_ref, gg_ref, w_ref, gpost_ref):
    a0, b0 = NA_WIDTH, NA_WIDTH + HY_WIDTH
    yb = x0 * (yh + skip_ref[...] * z)
    acc = jnp.dot(_rms(ya, gg_ref[:, :a0]).astype(BF16), w_ref[:a0, :], preferred_element_type=F32)
    acc += jnp.dot(_rms(yb, gg_ref[:, a0:b0]).astype(BF16), w_ref[a0:b0, :], preferred_element_type=F32)
    acc += jnp.dot(_rms(yc, gg_ref[:, b0:]).astype(BF16), w_ref[b0:, :], preferred_element_type=F32)
    return x + _rms(acc, gpost_ref[...])


def _xattn_body(x, gpre_ref, wq_ref, kv_ref, wo_ref, gpost_ref):
    h = _rms(x, gpre_ref[...]).astype(BF16)
    q = jnp.dot(h, wq_ref[...], preferred_element_type=F32) * (XA_HEAD_DIM ** -0.5)
    outs = []
    for hd in range(XA_HEADS):
        c0 = hd * XA_HEAD_DIM
        qh = q[:, c0:c0 + XA_HEAD_DIM].astype(BF16)
        kh = kv_ref[0, :, c0:c0 + XA_HEAD_DIM]
        vh = kv_ref[0, :, D_MODEL + c0:D_MODEL + c0 + XA_HEAD_DIM]
        s = lax.dot_general(qh, kh, (((1,), (1,)), ((), ())), preferred_element_type=F32)
        p = jnp.exp(s - jnp.max(s, axis=-1, keepdims=True))
        l = jnp.sum(p, axis=-1, keepdims=True)
        outs.append(jnp.dot(p.astype(BF16), vh, preferred_element_type=F32) / l)
    o = jnp.concatenate(outs, axis=-1).astype(BF16)
    xa = jnp.dot(o, wo_ref[...], preferred_element_type=F32)
    return x + _rms(xa, gpost_ref[...])


POST_TM = 512


def _post_mixer_kernel(x_ref, ya_ref, yh_ref, z_ref, x0_ref, yc_ref, skip_ref, gg_ref, wout_ref, gmix_ref,
                       gxpre_ref, wq_ref, kv_ref, wo_ref, gxpost_ref, gfpre_ref, fwin_ref, fwout_ref, gfpost_ref,
                       o_ref):
    x = _mix_out_body(x_ref[...], ya_ref[...], yh_ref[...], z_ref[...], x0_ref[...], yc_ref[...],
                      skip_ref, gg_ref, wout_ref, gmix_ref)
    x = _xattn_body(x, gxpre_ref, wq_ref, kv_ref, wo_ref, gxpost_ref)
    o_ref[...] = _ffn_body(x, gfpre_ref, fwin_ref, fwout_ref, gfpost_ref)


def _post_mixer(x, ya, yh, z, x0, yc, skip, gg, w_out, g_mix, g_xpre, wq, kv, wo, g_xpost, g_fpre, fw_in, fw_out,
                g_fpost):
    n = x.shape[0]
    per_batch = SEQ // POST_TM
    tok = lambda width: pl.BlockSpec((POST_TM, width), lambda i: (i, 0))
    const = lambda a: pl.BlockSpec(a.shape, lambda i: (0,) * a.ndim, pipeline_mode=pl.Buffered(1))
    kv_spec = pl.BlockSpec((1,) + kv.shape[1:], lambda i: (i // per_batch, 0, 0))
    return pl.pallas_call(
        _post_mixer_kernel,
        grid=(n // POST_TM,),
        in_specs=[tok(D_MODEL), tok(NA_WIDTH), tok(HY_WIDTH), tok(HY_WIDTH), tok(HY_WIDTH), tok(DIL_WIDTH),
                  const(skip), const(gg), const(w_out), const(g_mix),
                  const(g_xpre), const(wq), kv_spec, const(wo), const(g_xpost),
                  const(g_fpre), const(fw_in), const(fw_out), const(g_fpost)],
        out_specs=tok(D_MODEL),
        out_shape=jax.ShapeDtypeStruct((n, D_MODEL), F32),
        compiler_params=_cparams(("parallel",), 56),
        name="post_mixer",
    )(x, ya, yh, z, x0, yc, skip, gg, w_out, g_mix, g_xpre, wq, kv, wo, g_xpost, g_fpre, fw_in, fw_out, g_fpost)


def kernel(x, mem, norm_g, w_in, w_out, na_rpb, t5_table, hy_conv_w, hy_conv_b, hy_f_w1, hy_f_b1, hy_f_w2, hy_f_b2, hy_f_w3, hy_f_b3, hy_f_w4, hy_freq, hy_decay, hy_skip, xa_wq, xa_wkv, xa_wo, ffn_w_in, ffn_w_out):
    bsz, seq, d = x.shape
    assert (seq, d) == (SEQ, D_MODEL)
    depth = norm_g.shape[0]
    mem_len = mem.shape[1]
    n = bsz * seq
    scale = HEAD_DIM ** -0.5 * LOG2E
    wa, wb, wc = 3 * NA_WIDTH, 3 * HY_WIDTH, 3 * DIL_WIDTH
    colscale = np.ones((1, wa + wb + wc), np.float32)
    colscale[:, :NA_WIDTH] = scale
    colscale[:, wa + wb:wa + wb + DIL_WIDTH] = scale
    colscale = jnp.asarray(colscale)

    zfeat = _filter_features()
    f1, qu, gu, m1, ta = _dft_tables()
    dil_bias = _dil_bias(t5_table)

    xf = x.reshape(n, d)
    memf = mem.reshape(bsz * mem_len, d)
    for l in range(depth):
        g = norm_g[l][:, None, :]
        xf, pa, pb, pc = _pre_mixer(xf, g[0], ffn_w_in[l, 0].astype(BF16), ffn_w_out[l, 0].astype(BF16), g[1],
                                    g[2], w_in[l].astype(BF16), colscale)
        pa = pa.reshape(bsz, seq, wa)
        pb = pb.reshape(bsz, seq, wb)
        pc = pc.reshape(bsz, seq, wc)
        ya = _na_attention(pa, _na_tz(na_rpb[l]))
        z, x0 = _conv_gate(pb, hy_conv_w[l], hy_conv_b[l][None])
        hfilt = _hyena_filters(zfeat, hy_f_w1[l], hy_f_b1[l], hy_f_w2[l], hy_f_b2[l], hy_f_w3[l], hy_f_b3[l],
                               hy_f_w4[l], hy_freq[l], hy_decay[l])
        yh = _dft_conv(z, _dft_fwd(hfilt, f1, qu, ta, BF16), f1, qu, gu, m1, ta)
        yc = _dil_attention(pc, dil_bias)
        kv = _norm_mm(memf, g[6], xa_wkv[l].astype(BF16), mem_len, BF16).reshape(bsz, mem_len, 2 * D_MODEL)
        xf = _post_mixer(xf, ya.reshape(n, NA_WIDTH), yh.reshape(n, HY_WIDTH), z.reshape(n, HY_WIDTH),
                         x0.reshape(n, HY_WIDTH), yc.reshape(n, DIL_WIDTH), hy_skip[l][None], g[3],
                         w_out[l].astype(BF16), g[4],
                         g[5], xa_wq[l].astype(BF16), kv, xa_wo[l].astype(BF16), g[7],
                         g[8], ffn_w_in[l, 1].astype(BF16), ffn_w_out[l, 1].astype(BF16), g[9])
    return xf.reshape(bsz, seq, d)
```

```python
import functools
import math

import numpy as np
import jax
import jax.numpy as jnp
from jax import lax
from jax.experimental import pallas as pl
from jax.experimental.pallas import tpu as pltpu

F32 = jnp.float32
BF16 = jnp.bfloat16

D_MODEL = 1024
SEQ = 8192
GRID_W = 64
HEAD_DIM = 64
NA_WIDTH = 384
HY_WIDTH = 256
DIL_WIDTH = 384
NA_ROWS = 8
NA_COLS = 16
HY_BANDS = 16
DIL_CONFIGS = ((128, 1), (512, 4), (2048, 16))
T5_BUCKETS = 32
T5_MAX_DIST = 1024
XA_HEADS = 4
XA_HEAD_DIM = 256
FFN_HIDDEN = 2816
RMS_EPS = 1e-6
NEG_INF = -1e30
LOG2E = math.log2(math.e)

LANES = 128
MIB = 1024 * 1024

NA_QROWS = 8
NA_KROWS = 16
NA_BQ = NA_QROWS * GRID_W
NA_GROUPS = 4
NA_GQ = GRID_W // NA_GROUPS
NA_GK = 2 * NA_GQ
DIL_HALF = 64
DIL_BQ = 128
DIL_UNROLL = 8
DIL_BK = DIL_BQ + 2 * DIL_HALF
DFT_R = 128
DFT_HALF = 64
DFT_N = DFT_R * DFT_R
DFT_PITCH = 264


def _cparams(sem, vmem_mib):
    return pltpu.CompilerParams(dimension_semantics=sem, vmem_limit_bytes=vmem_mib * MIB)


def _resident(a, lead=()):
    k = len(lead)
    return pl.BlockSpec((None,) * k + a.shape[k:], lambda i: tuple(lead) + (0,) * (a.ndim - k),
                        pipeline_mode=pl.Buffered(1))


def _rms(x, g):
    return x * lax.rsqrt(jnp.mean(x * x, axis=-1, keepdims=True) + RMS_EPS) * g


FFN_TM = 512
FFN_TH = 256


def _ffn_body(x, gpre_ref, win_ref, wout_ref, gpost_ref):
    h = _rms(x, gpre_ref[...]).astype(BF16)
    acc = jnp.zeros(x.shape, F32)
    for j in range(FFN_HIDDEN // FFN_TH):
        c0 = j * FFN_TH
        gate = jnp.dot(h, win_ref[:, c0:c0 + FFN_TH], preferred_element_type=F32)
        up = jnp.dot(h, win_ref[:, FFN_HIDDEN + c0:FFN_HIDDEN + c0 + FFN_TH], preferred_element_type=F32)
        act = (gate * jax.nn.sigmoid(gate) * up).astype(BF16)
        acc += jnp.dot(act, wout_ref[c0:c0 + FFN_TH, :], preferred_element_type=F32)
    return x + 0.5 * _rms(acc, gpost_ref[...])


def _pre_mixer_kernel(x_ref, gfpre_ref, fwin_ref, fwout_ref, gfpost_ref, gmix_ref, w_ref, cs_ref,
                      o_ref, pa_ref, pb_ref, pc_ref):
    x = _ffn_body(x_ref[...], gfpre_ref, fwin_ref, fwout_ref, gfpost_ref)
    o_ref[...] = x
    h = _rms(x, gmix_ref[...]).astype(BF16)
    c0 = 0
    for ref in (pa_ref, pb_ref, pc_ref):
        c1 = c0 + ref.shape[1]
        ref[...] = jnp.dot(h, w_ref[:, c0:c1], preferred_element_type=F32) * cs_ref[:, c0:c1]
        c0 = c1


def _pre_mixer(x, layer, g_fpre, fw_in, fw_out, g_fpost, g_mix, w, colscale):
    n = x.shape[0]
    widths = (3 * NA_WIDTH, 3 * HY_WIDTH, 3 * DIL_WIDTH)
    tok = lambda width: pl.BlockSpec((FFN_TM, width), lambda i: (i, 0))
    const = _resident
    return pl.pallas_call(
        _pre_mixer_kernel,
        grid=(n // FFN_TM,),
        in_specs=[tok(D_MODEL), const(g_fpre), const(fw_in, (layer, 0)), const(fw_out, (layer, 0)), const(g_fpost),
                  const(g_mix), const(w, (layer,)), const(colscale)],
        out_specs=[tok(D_MODEL)] + [tok(wd) for wd in widths],
        out_shape=[jax.ShapeDtypeStruct((n, wd), F32) for wd in (D_MODEL,) + widths],
        compiler_params=_cparams(("parallel",), 56),
        name="pre_mixer",
    )(x, g_fpre, fw_in, fw_out, g_fpost, g_mix, w, colscale)


def _norm_mm_kernel(x_ref, g_ref, w_ref, o_ref):
    h = _rms(x_ref[...], g_ref[...]).astype(BF16)
    o_ref[...] = jnp.dot(h, w_ref[...], preferred_element_type=F32).astype(o_ref.dtype)


def _norm_mm(x, g, w, layer, tm, out_dtype):
    n, ncol = x.shape[0], w.shape[2]
    return pl.pallas_call(
        _norm_mm_kernel,
        grid=(n // tm,),
        in_specs=[pl.BlockSpec((tm, D_MODEL), lambda i: (i, 0)), _resident(g), _resident(w, (layer,))],
        out_specs=pl.BlockSpec((tm, ncol), lambda i: (i, 0)),
        out_shape=jax.ShapeDtypeStruct((n, ncol), out_dtype),
        compiler_params=_cparams(("parallel",), 32),
        name="norm_mm",
    )(x, g, w)


def _softmax_pv(q, k, v, bias):
    s = lax.dot_general(q, k, (((1,), (1,)), ((), ())), preferred_element_type=F32) + bias
    m = jnp.max(s, axis=-1, keepdims=True)
    p = jnp.exp2(s - m).astype(BF16)
    o = jnp.dot(p, jnp.concatenate([v, jnp.ones_like(v)], axis=1), preferred_element_type=F32)
    return o[:, :LANES], m, o[:, LANES:]


def _pair_attention(q, k, v, bias, stack_heads):
    bq = q.shape[0]
    lane = lax.broadcasted_iota(jnp.int32, (1, LANES), 1)
    first = lane < HEAD_DIM
    q0 = jnp.where(first, q, 0.0).astype(BF16)
    q1 = jnp.where(first, 0.0, q).astype(BF16)
    if stack_heads:
        res = _softmax_pv(jnp.concatenate([q0, q1], axis=0), k, v, bias)
        return tuple(jnp.where(first, a[:bq], a[bq:]) for a in res)
    res0 = _softmax_pv(q0, k, v, bias[:bq])
    res1 = _softmax_pv(q1, k, v, bias[bq:])
    return tuple(jnp.where(first, a, b) for a, b in zip(res0, res1))


def _na_key_col0(g):
    return int(np.clip(g * NA_GQ - NA_COLS // 2, 0, GRID_W - NA_GK))


def _na_fill_bias(tz_ref, bias_scr, j):
    nrows = SEQ // GRID_W
    start_row = int(np.clip(j * NA_QROWS - NA_ROWS // 2, 0, nrows - NA_KROWS))
    neg = jnp.full((NA_GQ, NA_GK), NEG_INF, F32)
    per_tile = LANES // NA_GK
    for g in range(NA_GROUPS):
        for hh in range(2):
            for rl in range(NA_QROWS):
                r = j * NA_QROWS + rl
                rs = int(np.clip(r - NA_ROWS // 2, 0, nrows - NA_ROWS))
                r0 = (hh * NA_QROWS + rl) * NA_GQ
                for kt in range(NA_KROWS // per_tile):
                    tiles = []
                    for kl in range(kt * per_tile, (kt + 1) * per_tile):
                        kr = start_row + kl
                        tiles.append(tz_ref[hh, kr - r + NA_ROWS - 1, g] if rs <= kr < rs + NA_ROWS else neg)
                    bias_scr[g, r0:r0 + NA_GQ, kt * LANES:(kt + 1) * LANES] = jnp.concatenate(tiles, axis=1)


def _na_row_block(q_ref, k_ref, v_ref, o_ref, bias_scr, j, q0):
    nrows = SEQ // GRID_W
    start_row = jnp.clip(j * NA_QROWS - NA_ROWS // 2, 0, nrows - NA_KROWS)
    start = start_row * GRID_W
    for g in range(NA_GROUPS):
        kc0 = _na_key_col0(g)
        krows = [pl.ds(pl.multiple_of(start + kl * GRID_W + kc0, 8), NA_GK) for kl in range(NA_KROWS)]
        qrows = [slice(q0 + rl * GRID_W + g * NA_GQ, q0 + rl * GRID_W + (g + 1) * NA_GQ) for rl in range(NA_QROWS)]
        k = jnp.concatenate([k_ref[0, r, :] for r in krows], axis=0).astype(BF16)
        v = jnp.concatenate([v_ref[0, r, :] for r in krows], axis=0).astype(BF16)
        q = jnp.concatenate([q_ref[0, r, :] for r in qrows], axis=0)
        o, _, l = _pair_attention(q, k, v, bias_scr[g], stack_heads=True)
        o = o / l
        for rl, r in enumerate(qrows):
            o_ref[0, r, :] = o[rl * NA_GQ:(rl + 1) * NA_GQ]


def _na_kernel(q_ref, k_ref, v_ref, tz_ref, o_ref, bias_scr):
    step = pl.program_id(2)
    nblk = SEQ // NA_BQ
    fill = functools.partial(_na_fill_bias, tz_ref, bias_scr)
    block = functools.partial(_na_row_block, q_ref, k_ref, v_ref, o_ref, bias_scr)
    pl.when(step == 0)(functools.partial(fill, 0))
    block(2 * step, 0)
    pl.when(step == 0)(functools.partial(fill, 1))
    pl.when(step == pl.num_programs(2) - 1)(functools.partial(fill, nblk - 1))
    block(2 * step + 1, NA_BQ)


def _na_attention(pa, tzg):
    b = pa.shape[0]
    npair = NA_WIDTH // LANES
    nstep = SEQ // (2 * NA_BQ)
    return pl.pallas_call(
        _na_kernel,
        grid=(b, npair, nstep),
        in_specs=[
            pl.BlockSpec((1, 2 * NA_BQ, LANES), lambda bi, p, j: (bi, j, p)),
            pl.BlockSpec((1, SEQ, LANES), lambda bi, p, j: (bi, 0, npair + p)),
            pl.BlockSpec((1, SEQ, LANES), lambda bi, p, j: (bi, 0, 2 * npair + p)),
            pl.BlockSpec((2,) + tzg.shape[1:], lambda bi, p, j: (p, 0, 0, 0, 0)),
        ],
        out_specs=pl.BlockSpec((1, 2 * NA_BQ, LANES), lambda bi, p, j: (bi, j, p)),
        out_shape=jax.ShapeDtypeStruct((b, SEQ, NA_WIDTH), F32),
        scratch_shapes=[pltpu.VMEM((NA_GROUPS, 2 * NA_QROWS * NA_GQ, NA_KROWS * NA_GK), F32)],
        compiler_params=_cparams(("parallel", "parallel", "arbitrary"), 48),
        name="na_attn",
    )(pa, pa, pa, tzg)


def _na_tz(rpb):
    col = np.arange(GRID_W)
    cs = np.clip(col - NA_COLS // 2, 0, GRID_W - NA_COLS)
    col_ok = (col[None, :] >= cs[:, None]) & (col[None, :] < cs[:, None] + NA_COLS)
    period = GRID_W + 1
    half = NA_COLS - 1
    filler = jnp.zeros(rpb.shape[:2] + (period - (2 * half + 1),), rpb.dtype)
    v = jnp.concatenate([rpb[..., half:], filler, rpb[..., :half]], axis=-1)
    t = jnp.tile(v, (1, 1, GRID_W))[..., :GRID_W * GRID_W].reshape(rpb.shape[:2] + (GRID_W, GRID_W))
    t = jnp.where(col_ok[None, None], t * LOG2E, NEG_INF)
    groups = []
    for g in range(NA_GROUPS):
        kc0 = _na_key_col0(g)
        assert col_ok[g * NA_GQ:(g + 1) * NA_GQ, :kc0].sum() == 0 and col_ok[g * NA_GQ:(g + 1) * NA_GQ, kc0 + NA_GK:].sum() == 0
        groups.append(t[:, :, g * NA_GQ:(g + 1) * NA_GQ, kc0:kc0 + NA_GK])
    return jnp.stack(groups, axis=2)


def _dil_kernel(q_ref, k_ref, v_ref, bias_ref, o_ref, m_scr, l_scr):
    order = sorted(range(len(DIL_CONFIGS)), key=lambda c: -DIL_CONFIGS[c][1])
    for c in order:
        dil = DIL_CONFIGS[c][1]
        m_len = SEQ // dil
        nblk = m_len // DIL_BQ

        def merge(qrows, o, m, l, c=c):
            if c == order[0]:
                o_ref[0, qrows, :] = o
                m_scr[qrows, :] = m
                l_scr[qrows, :] = l
                return
            m_old = m_scr[qrows, :]
            m_new = jnp.maximum(m_old, m)
            a_old = jnp.exp2(m_old - m_new)
            a_new = jnp.exp2(m - m_new)
            o_new = o_ref[0, qrows, :] * a_old + o * a_new
            l_new = l_scr[qrows, :] * a_old + l * a_new
            if c == order[-1]:
                o_ref[0, qrows, :] = o_new / l_new
            else:
                o_ref[0, qrows, :] = o_new
                l_scr[qrows, :] = l_new
                m_scr[qrows, :] = m_new

        def window(t, lo, pos, hi):
            return jnp.concatenate([t(lo, DIL_HALF), t(pos, DIL_BQ), t(hi, DIL_HALF)], axis=0)

        def block_body(it, carry, c=c, dil=dil, m_len=m_len, nblk=nblk):
            g = it // nblk
            n = it % nblk
            pos = n * DIL_BQ

            def rows(p0, size):
                return pl.ds(g + p0 * dil, size, stride=dil) if dil > 1 else pl.ds(p0, size)

            lo = jnp.maximum(pos - DIL_HALF, 0)
            hi = jnp.minimum(pos + DIL_BQ, m_len - DIL_HALF)
            k = window(lambda p0, size: k_ref[0, rows(p0, size), :], lo, pos, hi).astype(BF16)
            v = window(lambda p0, size: v_ref[0, rows(p0, size), :], lo, pos, hi).astype(BF16)
            edge_case = jnp.where(n == 0, 0, jnp.where(n == nblk - 1, 2, 1))
            qrows = rows(pos, DIL_BQ)
            merge(qrows, *_pair_attention(q_ref[0, qrows, :], k, v, bias_ref[c, 0, edge_case], stack_heads=True))
            return carry

        def class_body(g, carry, c=c, dil=dil, m_len=m_len, nblk=nblk):
            rows = pl.ds(g, m_len, stride=dil)
            q = q_ref[0, rows, :]
            kc = k_ref[0, rows, :].astype(BF16)
            vc = v_ref[0, rows, :].astype(BF16)
            res = []
            for n in range(nblk):
                pos = n * DIL_BQ
                lo, hi = max(pos - DIL_HALF, 0), min(pos + DIL_BQ, m_len - DIL_HALF)
                k = window(lambda p0, size: kc[p0:p0 + size], lo, pos, hi)
                v = window(lambda p0, size: vc[p0:p0 + size], lo, pos, hi)
                edge_case = 0 if n == 0 else 2 if n == nblk - 1 else 1
                res.append(_pair_attention(q[pos:pos + DIL_BQ], k, v, bias_ref[c, 0, edge_case], stack_heads=True))
            merge(rows, *(jnp.concatenate(parts, axis=0) for parts in zip(*res)))
            return carry

        if nblk <= DIL_UNROLL:
            lax.fori_loop(0, dil, class_body, 0, unroll=DIL_UNROLL // nblk)
        else:
            lax.fori_loop(0, SEQ // DIL_BQ, block_body, 0, unroll=DIL_UNROLL)


def _dil_attention(pc, bias):
    b = pc.shape[0]
    npair = DIL_WIDTH // LANES
    assert all(SEQ // dil // DIL_BQ >= 2 for _, dil in DIL_CONFIGS)
    return pl.pallas_call(
        _dil_kernel,
        grid=(b, npair),
        in_specs=[
            pl.BlockSpec((1, SEQ, LANES), lambda bi, p: (bi, 0, p)),
            pl.BlockSpec((1, SEQ, LANES), lambda bi, p: (bi, 0, npair + p)),
            pl.BlockSpec((1, SEQ, LANES), lambda bi, p: (bi, 0, 2 * npair + p)),
            pl.BlockSpec((len(DIL_CONFIGS), 1, 3, 2 * DIL_BQ, DIL_BK), lambda bi, p: (0, p, 0, 0, 0)),
        ],
        out_specs=pl.BlockSpec((1, SEQ, LANES), lambda bi, p: (bi, 0, p)),
        out_shape=jax.ShapeDtypeStruct((b, SEQ, DIL_WIDTH), F32),
        scratch_shapes=[pltpu.VMEM((SEQ, LANES), F32), pltpu.VMEM((SEQ, LANES), F32)],
        compiler_params=_cparams(("parallel", "arbitrary"), 56),
        name="dil_attn",
    )(pc, pc, pc, bias)


def _t5_bucket(rel):
    half = T5_BUCKETS // 2
    exact = half // 2
    n = np.abs(rel)
    far = exact + (np.log(np.maximum(n, 1) / exact) / math.log(T5_MAX_DIST / exact) * (half - exact)).astype(np.int32)
    far = np.minimum(far, half - 1)
    return (np.where(rel > 0, half, 0) + np.where(n < exact, n, far)).astype(np.int32)


def _dil_bias(t5_table):
    rel = np.arange(-DIL_HALF, DIL_HALF + 1)
    period = DIL_BK + 1
    nh = t5_table.shape[1]
    out = []
    for _, dil in DIL_CONFIGS:
        vals = t5_table[_t5_bucket(dil * rel)].T
        v = jnp.concatenate([vals, jnp.full((nh, period - vals.shape[1]), NEG_INF, vals.dtype)], axis=1)
        out.append(jnp.tile(v, (1, DIL_BQ))[:, :DIL_BQ * DIL_BK].reshape(nh, DIL_BQ, DIL_BK))
    t = (jnp.stack(out) * LOG2E).reshape(len(DIL_CONFIGS), nh // 2, 1, 2 * DIL_BQ, DIL_BK)
    kcol = np.arange(DIL_BK)
    edge = np.stack([kcol < DIL_HALF, np.zeros_like(kcol, bool), kcol >= DIL_HALF + DIL_BQ])
    return jnp.where(edge[None, None, :, None, :], NEG_INF, t)


CONV_TM = 1024


def _conv_gate_kernel(p_ref, prev_ref, next_ref, w_ref, b_ref, z_ref, x0_ref):
    i = pl.program_id(1)
    p = p_ref[0]
    tm = p.shape[0]
    first = jnp.where(i > 0, prev_ref[0, 7:8, :], 0.0)
    last = jnp.where(i < pl.num_programs(1) - 1, next_ref[0, 0:1, :], 0.0)
    row = lax.broadcasted_iota(jnp.int32, (tm, 1), 0)
    up = jnp.where(row == 0, first, pltpu.roll(p, 1, axis=0))
    dn = jnp.where(row == tm - 1, last, pltpu.roll(p, tm - 1, axis=0))
    uc = up * w_ref[0:1, :] + p * w_ref[1:2, :] + dn * w_ref[2:3, :] + b_ref[...]
    z_ref[0] = uc[:, HY_WIDTH:2 * HY_WIDTH] * uc[:, :HY_WIDTH]
    x0_ref[0] = uc[:, 2 * HY_WIDTH:]


def _conv_gate(pb, w, bvec):
    b = pb.shape[0]
    wb = 3 * HY_WIDTH
    nt = SEQ // CONV_TM
    r8 = CONV_TM // 8
    return pl.pallas_call(
        _conv_gate_kernel,
        grid=(b, nt),
        in_specs=[
            pl.BlockSpec((1, CONV_TM, wb), lambda bi, i: (bi, i, 0)),
            pl.BlockSpec((1, 8, wb), lambda bi, i: (bi, jnp.maximum(i * r8 - 1, 0), 0)),
            pl.BlockSpec((1, 8, wb), lambda bi, i: (bi, jnp.minimum((i + 1) * r8, SEQ // 8 - 1), 0)),
            pl.BlockSpec((3, wb), lambda bi, i: (0, 0)),
            pl.BlockSpec((1, wb), lambda bi, i: (0, 0)),
        ],
        out_specs=[
            pl.BlockSpec((1, CONV_TM, HY_WIDTH), lambda bi, i: (bi, i, 0)),
            pl.BlockSpec((1, CONV_TM, HY_WIDTH), lambda bi, i: (bi, i, 0)),
        ],
        out_shape=[jax.ShapeDtypeStruct((b, SEQ, HY_WIDTH), F32)] * 2,
        compiler_params=_cparams(("parallel", "parallel"), 48),
        name="hy_conv_gate",
    )(pb, pb, pb, w, bvec)


FILT_TM = 1024
FILT_KPAD = 128
FILT_HALF = SEQ // 2


def _filter_kernel(z_ref, w1_ref, b1_ref, w2_ref, b2_ref, w3_ref, b3_ref, w4_ref, fr_ref, dec_ref, h_ref):
    hp = lax.Precision.HIGHEST
    z = z_ref[...]
    fr = fr_ref[...]
    h = jnp.sin(fr * (jnp.dot(z, w1_ref[...], precision=hp, preferred_element_type=F32) + b1_ref[...]))
    h = jnp.sin(fr * (jnp.dot(h, w2_ref[...], precision=hp, preferred_element_type=F32) + b2_ref[...]))
    h = jnp.sin(fr * (jnp.dot(h, w3_ref[...], precision=hp, preferred_element_type=F32) + b3_ref[...]))
    h = jnp.dot(h, w4_ref[...], precision=hp, preferred_element_type=F32)
    wout = 2 * HY_WIDTH
    dec = jnp.abs(dec_ref[...])
    lo = h[:, :wout] * jnp.exp(-z[:, 0:1] * dec)
    hi = h[:, wout:] * jnp.exp(-z[:, FILT_KPAD:FILT_KPAD + 1] * dec)
    row = pl.program_id(0) * FILT_TM + lax.broadcasted_iota(jnp.int32, (FILT_TM, 1), 0)
    col = lax.broadcasted_iota(jnp.int32, (1, wout), 1)
    h_ref[0] = jnp.where((row == 0) & (col >= HY_WIDTH), 0.0, lo)
    h_ref[1] = hi


def _filter_features():
    t = jnp.linspace(0.0, 1.0, SEQ, dtype=F32)[:, None]
    bands = jnp.linspace(1e-4, HY_BANDS - 1, HY_BANDS, dtype=F32)[None, :]
    ang = (2.0 * math.pi / SEQ) * jnp.arange(SEQ, dtype=F32)[:, None] * bands
    z = jnp.concatenate([t, jnp.cos(ang), -jnp.sin(ang)], axis=-1)
    z = jnp.pad(z, ((0, 0), (0, FILT_KPAD - z.shape[1])))
    return jnp.concatenate([z[:FILT_HALF], z[FILT_HALF:]], axis=1)


def _hyena_filters(zfeat, w1, b1, w2, b2, w3, b3, w4, freq, decay):
    def diag2(w):
        zero = jnp.zeros_like(w)
        return jnp.concatenate([jnp.concatenate([w, zero], axis=1), jnp.concatenate([zero, w], axis=1)], axis=0)

    twice = lambda v: jnp.concatenate([v, v])[None]
    w1p = jnp.pad(w1, ((0, FILT_KPAD - w1.shape[0]), (0, 0)))
    full = lambda a: pl.BlockSpec(a.shape, lambda i: (0,) * a.ndim)
    args = (diag2(w1p), twice(b1), diag2(w2), twice(b2), diag2(w3), twice(b3), diag2(w4), twice(freq), decay[None])
    out = pl.pallas_call(
        _filter_kernel,
        grid=(FILT_HALF // FILT_TM,),
        in_specs=[pl.BlockSpec((FILT_TM, 2 * FILT_KPAD), lambda i: (i, 0))] + [full(a) for a in args],
        out_specs=pl.BlockSpec((2, FILT_TM, 2 * HY_WIDTH), lambda i: (0, i, 0)),
        out_shape=jax.ShapeDtypeStruct((2, FILT_HALF, 2 * HY_WIDTH), F32),
        compiler_params=_cparams(("parallel",), 32),
        name="hy_filter",
    )(zfeat, *args)
    return out.reshape(1, SEQ, 2 * HY_WIDTH)


def _dft_tables():
    a = np.arange(DFT_R)
    w = np.exp(-2j * np.pi * np.outer(a, a) / DFT_R)
    tw = np.exp(-2j * np.pi * np.outer(a, a) / DFT_N)
    f32 = lambda x: jnp.asarray(x.astype(np.float32))
    f1 = np.concatenate([w.real[:, :DFT_HALF], w.imag[:, :DFT_HALF]], axis=0)
    m1 = np.concatenate([w.real[:DFT_HALF], w.imag[:DFT_HALF]], axis=1) / DFT_N
    c = w[None] * tw[:, :DFT_KB].T[:, None, :]
    blocks = lambda a, b, c, d: np.concatenate(
        [np.concatenate([a, b], axis=2), np.concatenate([c, d], axis=2)], axis=1)
    qu = blocks(c.real, -c.imag, c.imag, c.real)
    ct = np.swapaxes(c, 1, 2)
    gu = blocks(ct.real, ct.imag, -ct.imag, ct.real)
    ta = np.exp(-2j * np.pi * DFT_KB * np.outer(np.arange(DFT_R // DFT_KB), a) / DFT_N)
    ta = np.concatenate([ta.real, ta.imag], axis=1)[:, :, None] * np.ones((1, 1, LANES))
    bf = lambda x: f32(x).astype(BF16)
    return bf(f1), bf(qu), bf(gu), bf(m1), f32(ta)


DFT_KB = 8


def _dft_stage1(x_ref, f1_ref, s_scr):
    def body(i, carry):
        n2b = i * DFT_KB
        xs = jnp.concatenate([x_ref[0, pl.ds(n2b + u, DFT_HALF, stride=DFT_R), :] for u in range(DFT_KB)],
                             axis=1).astype(BF16)
        a = jnp.dot(f1_ref[...], xs, preferred_element_type=F32)
        for u in range(DFT_KB):
            s_scr[pl.ds(pl.multiple_of((n2b + u) * DFT_PITCH, 8), 2 * DFT_R), :] = a[:, u * LANES:(u + 1) * LANES]
        return carry

    lax.fori_loop(0, DFT_R // DFT_KB, body, 0)


def _dft_stage2_rows(s_scr, ta_ref, step, u):
    k1 = step * DFT_KB + u
    re = s_scr[pl.ds(k1, DFT_R, stride=DFT_PITCH), :]
    im = s_scr[pl.ds(DFT_R + k1, DFT_R, stride=DFT_PITCH), :]
    tr, ti = ta_ref[step, :DFT_R, :], ta_ref[step, DFT_R:, :]
    return jnp.concatenate([re * tr - im * ti, re * ti + im * tr], axis=0).astype(BF16)


def _dft_fwd_kernel(x_ref, f1_ref, qu_ref, ta_ref, o_ref, s_scr):
    step = pl.program_id(2)
    pl.when(step == 0)(functools.partial(_dft_stage1, x_ref, f1_ref, s_scr))
    for u in range(DFT_KB):
        a = _dft_stage2_rows(s_scr, ta_ref, step, u)
        o_ref[0, u] = jnp.dot(qu_ref[u], a, preferred_element_type=F32).astype(o_ref.dtype)


def _dft_fwd(x, f1, qu, ta, out_dtype):
    nb, _, c = x.shape
    full = lambda a: pl.BlockSpec(a.shape, lambda b, ct, s: (0,) * a.ndim)
    return pl.pallas_call(
        _dft_fwd_kernel,
        grid=(nb, c // LANES, DFT_R // DFT_KB),
        in_specs=[pl.BlockSpec((1, SEQ, LANES), lambda b, ct, s: (b, 0, ct)), full(f1), full(qu), full(ta)],
        out_specs=pl.BlockSpec((1, DFT_KB, 2 * DFT_R, LANES), lambda b, ct, s: (b, s, 0, ct)),
        out_shape=jax.ShapeDtypeStruct((nb, DFT_R, 2 * DFT_R, c), out_dtype),
        scratch_shapes=[pltpu.VMEM((DFT_R * DFT_PITCH, LANES), F32)],
        compiler_params=_cparams(("parallel", "parallel", "arbitrary"), 40),
        name="hy_dft_fwd",
    )(x, f1, qu, ta)


def _dft_stage1_inv(s_scr, m1_ref, o_ref):
    def body(i, carry):
        n2b = i * DFT_KB
        bc = jnp.concatenate(
            [s_scr[pl.ds(pl.multiple_of((n2b + u) * DFT_PITCH, 8), 2 * DFT_R), :] for u in range(DFT_KB)],
            axis=1).astype(BF16)
        y = jnp.dot(m1_ref[...], bc, preferred_element_type=F32)
        for u in range(DFT_KB):
            o_ref[0, pl.ds(n2b + u, DFT_HALF, stride=DFT_R), :] = y[:, u * LANES:(u + 1) * LANES]
        return carry

    lax.fori_loop(0, DFT_R // DFT_KB, body, 0)


def _dft_conv_kernel(x_ref, f1_ref, qu_ref, ta_ref, hf_ref, hb_ref, gu_ref, m1_ref, o_ref, s_scr):
    step = pl.program_id(2)
    pl.when(step == 0)(functools.partial(_dft_stage1, x_ref, f1_ref, s_scr))
    tr, ti = ta_ref[step, :DFT_R, :], ta_ref[step, DFT_R:, :]
    for u in range(DFT_KB):
        k1 = step * DFT_KB + u
        xk = jnp.dot(qu_ref[u], _dft_stage2_rows(s_scr, ta_ref, step, u), preferred_element_type=F32)
        xr, xi = xk[:DFT_R], xk[DFT_R:]
        hr = hf_ref[0, u, :DFT_R, :].astype(F32) + hb_ref[0, u, :DFT_R, :].astype(F32)
        hi = hf_ref[0, u, DFT_R:, :].astype(F32) - hb_ref[0, u, DFT_R:, :].astype(F32)
        y = jnp.concatenate([xr * hr - xi * hi, xr * hi + xi * hr], axis=0).astype(BF16)
        b = jnp.dot(gu_ref[u], y, preferred_element_type=F32)
        br, bi = b[:DFT_R], b[DFT_R:]
        s_scr[pl.ds(k1, DFT_R, stride=DFT_PITCH), :] = br * tr + bi * ti
        s_scr[pl.ds(DFT_R + k1, DFT_R, stride=DFT_PITCH), :] = bi * tr - br * ti
    pl.when(step == pl.num_programs(2) - 1)(functools.partial(_dft_stage1_inv, s_scr, m1_ref, o_ref))


def _dft_conv(x, hs, f1, qu, gu, m1, ta):
    b, _, c = x.shape
    nct = c // LANES
    full = lambda a: pl.BlockSpec(a.shape, lambda bi, ct, s: (0,) * a.ndim)
    return pl.pallas_call(
        _dft_conv_kernel,
        grid=(b, nct, DFT_R // DFT_KB),
        in_specs=[
            pl.BlockSpec((1, SEQ, LANES), lambda bi, ct, s: (bi, 0, ct)),
            full(f1), full(qu), full(ta),
            pl.BlockSpec((1, DFT_KB, 2 * DFT_R, LANES), lambda bi, ct, s: (0, s, 0, ct)),
            pl.BlockSpec((1, DFT_KB, 2 * DFT_R, LANES), lambda bi, ct, s: (0, s, 0, nct + ct)),
            full(gu), full(m1),
        ],
        out_specs=pl.BlockSpec((1, SEQ, LANES), lambda bi, ct, s: (bi, 0, ct)),
        out_shape=jax.ShapeDtypeStruct((b, SEQ, c), F32),
        scratch_shapes=[pltpu.VMEM((DFT_R * DFT_PITCH, LANES), F32)],
        compiler_params=_cparams(("parallel", "parallel", "arbitrary"), 48),
        name="hy_dft_conv",
    )(x, f1, qu, ta, hs, hs, gu, m1)


def _mix_out_body(x, ya, yh, z, x0, yc, skip_ref, gg_ref, w_ref, gpost_ref):
    a0, b0 = NA_WIDTH, NA_WIDTH + HY_WIDTH
    yb = x0 * (yh + skip_ref[...] * z)
    acc = jnp.dot(_rms(ya, gg_ref[:, :a0]).astype(BF16), w_ref[:a0, :], preferred_element_type=F32)
    acc += jnp.dot(_rms(yb, gg_ref[:, a0:b0]).astype(BF16), w_ref[a0:b0, :], preferred_element_type=F32)
    acc += jnp.dot(_rms(yc, gg_ref[:, b0:]).astype(BF16), w_ref[b0:, :], preferred_element_type=F32)
    return x + _rms(acc, gpost_ref[...])


def _xattn_body(x, gpre_ref, wq_ref, kv_ref, wo_ref, gpost_ref):
    h = _rms(x, gpre_ref[...]).astype(BF16)
    q = jnp.dot(h, wq_ref[...], preferred_element_type=F32) * (XA_HEAD_DIM ** -0.5)
    outs = []
    for hd in range(XA_HEADS):
        c0 = hd * XA_HEAD_DIM
        qh = q[:, c0:c0 + XA_HEAD_DIM].astype(BF16)
        kh = kv_ref[0, :, c0:c0 + XA_HEAD_DIM]
        vh = kv_ref[0, :, D_MODEL + c0:D_MODEL + c0 + XA_HEAD_DIM]
        s = lax.dot_general(qh, kh, (((1,), (1,)), ((), ())), preferred_element_type=F32)
        p = jnp.exp(s - jnp.max(s, axis=-1, keepdims=True))
        l = jnp.sum(p, axis=-1, keepdims=True)
        outs.append(jnp.dot(p.astype(BF16), vh, preferred_element_type=F32) / l)
    o = jnp.concatenate(outs, axis=-1).astype(BF16)
    xa = jnp.dot(o, wo_ref[...], preferred_element_type=F32)
    return x + _rms(xa, gpost_ref[...])


POST_TM = 512


def _post_mixer_kernel(x_ref, ya_ref, yh_ref, z_ref, x0_ref, yc_ref, skip_ref, gg_ref, wout_ref, gmix_ref,
                       gxpre_ref, wq_ref, kv_ref, wo_ref, gxpost_ref, gfpre_ref, fwin_ref, fwout_ref, gfpost_ref,
                       o_ref):
    x = _mix_out_body(x_ref[...], ya_ref[...], yh_ref[...], z_ref[...], x0_ref[...], yc_ref[...],
                      skip_ref, gg_ref, wout_ref, gmix_ref)
    x = _xattn_body(x, gxpre_ref, wq_ref, kv_ref, wo_ref, gxpost_ref)
    o_ref[...] = _ffn_body(x, gfpre_ref, fwin_ref, fwout_ref, gfpost_ref)


def _post_mixer(x, layer, ya, yh, z, x0, yc, skip, gg, w_out, g_mix, g_xpre, wq, kv, wo, g_xpost, g_fpre, fw_in,
                fw_out, g_fpost):
    n = x.shape[0]
    per_batch = SEQ // POST_TM
    tok = lambda width: pl.BlockSpec((POST_TM, width), lambda i: (i, 0))
    const = _resident
    kv_spec = pl.BlockSpec((1,) + kv.shape[1:], lambda i: (i // per_batch, 0, 0))
    return pl.pallas_call(
        _post_mixer_kernel,
        grid=(n // POST_TM,),
        in_specs=[tok(D_MODEL), tok(NA_WIDTH), tok(HY_WIDTH), tok(HY_WIDTH), tok(HY_WIDTH), tok(DIL_WIDTH),
                  const(skip), const(gg), const(w_out, (layer,)), const(g_mix),
                  const(g_xpre), const(wq, (layer,)), kv_spec, const(wo, (layer,)), const(g_xpost),
                  const(g_fpre), const(fw_in, (layer, 1)), const(fw_out, (layer, 1)), const(g_fpost)],
        out_specs=tok(D_MODEL),
        out_shape=jax.ShapeDtypeStruct((n, D_MODEL), F32),
        compiler_params=_cparams(("parallel",), 56),
        name="post_mixer",
    )(x, ya, yh, z, x0, yc, skip, gg, w_out, g_mix, g_xpre, wq, kv, wo, g_xpost, g_fpre, fw_in, fw_out, g_fpost)


def kernel(x, mem, norm_g, w_in, w_out, na_rpb, t5_table, hy_conv_w, hy_conv_b, hy_f_w1, hy_f_b1, hy_f_w2, hy_f_b2, hy_f_w3, hy_f_b3, hy_f_w4, hy_freq, hy_decay, hy_skip, xa_wq, xa_wkv, xa_wo, ffn_w_in, ffn_w_out):
    bsz, seq, d = x.shape
    assert (seq, d) == (SEQ, D_MODEL)
    depth = norm_g.shape[0]
    mem_len = mem.shape[1]
    n = bsz * seq
    scale = HEAD_DIM ** -0.5 * LOG2E
    wa, wb, wc = 3 * NA_WIDTH, 3 * HY_WIDTH, 3 * DIL_WIDTH
    colscale = np.ones((1, wa + wb + wc), np.float32)
    colscale[:, :NA_WIDTH] = scale
    colscale[:, wa + wb:wa + wb + DIL_WIDTH] = scale
    colscale = jnp.asarray(colscale)

    zfeat = _filter_features()
    f1, qu, gu, m1, ta = _dft_tables()
    dil_bias = _dil_bias(t5_table)

    xf = x.reshape(n, d)
    memf = mem.reshape(bsz * mem_len, d)
    ffn_w_in, ffn_w_out, w_in, w_out, xa_wq, xa_wkv, xa_wo = (
        w.astype(BF16) for w in (ffn_w_in, ffn_w_out, w_in, w_out, xa_wq, xa_wkv, xa_wo))
    for l in range(depth):
        g = norm_g[l][:, None, :]
        xf, pa, pb, pc = _pre_mixer(xf, l, g[0], ffn_w_in, ffn_w_out, g[1], g[2], w_in, colscale)
        pa = pa.reshape(bsz, seq, wa)
        pb = pb.reshape(bsz, seq, wb)
        pc = pc.reshape(bsz, seq, wc)
        ya = _na_attention(pa, _na_tz(na_rpb[l]))
        z, x0 = _conv_gate(pb, hy_conv_w[l], hy_conv_b[l][None])
        hfilt = _hyena_filters(zfeat, hy_f_w1[l], hy_f_b1[l], hy_f_w2[l], hy_f_b2[l], hy_f_w3[l], hy_f_b3[l],
                               hy_f_w4[l], hy_freq[l], hy_decay[l])
        yh = _dft_conv(z, _dft_fwd(hfilt, f1, qu, ta, BF16), f1, qu, gu, m1, ta)
        yc = _dil_attention(pc, dil_bias)
        kv = _norm_mm(memf, g[6], xa_wkv, l, mem_len, BF16).reshape(bsz, mem_len, 2 * D_MODEL)
        xf = _post_mixer(xf, l, ya.reshape(n, NA_WIDTH), yh.reshape(n, HY_WIDTH), z.reshape(n, HY_WIDTH),
                         x0.reshape(n, HY_WIDTH), yc.reshape(n, DIL_WIDTH), hy_skip[l][None], g[3],
                         w_out, g[4], g[5], xa_wq, kv, xa_wo, g[7], g[8], ffn_w_in, ffn_w_out, g[9])
    return xf.reshape(bsz, seq, d)
```

```python
import functools
import math

import numpy as np
import jax
import jax.numpy as jnp
from jax import lax
from jax.experimental import pallas as pl
from jax.experimental.pallas import tpu as pltpu

F32 = jnp.float32
BF16 = jnp.bfloat16

D_MODEL = 1024
SEQ = 8192
GRID_W = 64
HEAD_DIM = 64
NA_WIDTH = 384
HY_WIDTH = 256
DIL_WIDTH = 384
NA_ROWS = 8
NA_COLS = 16
HY_BANDS = 16
DIL_CONFIGS = ((128, 1), (512, 4), (2048, 16))
T5_BUCKETS = 32
T5_MAX_DIST = 1024
XA_HEADS = 4
XA_HEAD_DIM = 256
FFN_HIDDEN = 2816
RMS_EPS = 1e-6
NEG_INF = -1e30
LOG2E = math.log2(math.e)

LANES = 128
MIB = 1024 * 1024

NA_QROWS = 8
NA_KROWS = 16
NA_BQ = NA_QROWS * GRID_W
NA_GROUPS = 4
NA_GQ = GRID_W // NA_GROUPS
NA_GK = 2 * NA_GQ
DIL_HALF = 64
DIL_BQ = 128
DIL_UNROLL = 8
DIL_BK = DIL_BQ + 2 * DIL_HALF
DFT_R = 128
DFT_HALF = 64
DFT_N = DFT_R * DFT_R
DFT_PITCH = 264


def _cparams(sem, vmem_mib):
    return pltpu.CompilerParams(dimension_semantics=sem, vmem_limit_bytes=vmem_mib * MIB)


def _resident(a, lead=()):
    k = len(lead)
    return pl.BlockSpec((None,) * k + a.shape[k:], lambda i: tuple(lead) + (0,) * (a.ndim - k),
                        pipeline_mode=pl.Buffered(1))


def _rms(x, g):
    return x * lax.rsqrt(jnp.mean(x * x, axis=-1, keepdims=True) + RMS_EPS) * g


FFN_TM = 512
FFN_TH = 256


def _ffn_body(x, gpre_ref, win_ref, wout_ref, gpost_ref):
    h = _rms(x, gpre_ref[...]).astype(BF16)
    acc = jnp.zeros(x.shape, F32)
    for j in range(FFN_HIDDEN // FFN_TH):
        c0 = j * FFN_TH
        gate = jnp.dot(h, win_ref[:, c0:c0 + FFN_TH], preferred_element_type=F32)
        up = jnp.dot(h, win_ref[:, FFN_HIDDEN + c0:FFN_HIDDEN + c0 + FFN_TH], preferred_element_type=F32)
        act = (gate * jax.nn.sigmoid(gate) * up).astype(BF16)
        acc += jnp.dot(act, wout_ref[c0:c0 + FFN_TH, :], preferred_element_type=F32)
    return x + 0.5 * _rms(acc, gpost_ref[...])


def _pre_mixer_kernel(x_ref, gfpre_ref, fwin_ref, fwout_ref, gfpost_ref, gmix_ref, w_ref, cs_ref,
                      o_ref, pa_ref, pb_ref, pc_ref):
    x = _ffn_body(x_ref[...], gfpre_ref, fwin_ref, fwout_ref, gfpost_ref)
    o_ref[...] = x
    h = _rms(x, gmix_ref[...]).astype(BF16)
    c0 = 0
    for ref in (pa_ref, pb_ref, pc_ref):
        c1 = c0 + ref.shape[1]
        ref[...] = jnp.dot(h, w_ref[:, c0:c1], preferred_element_type=F32) * cs_ref[:, c0:c1]
        c0 = c1


def _pre_mixer(x, layer, g_fpre, fw_in, fw_out, g_fpost, g_mix, w, colscale):
    n = x.shape[0]
    widths = (3 * NA_WIDTH, 3 * HY_WIDTH, 3 * DIL_WIDTH)
    tok = lambda width: pl.BlockSpec((FFN_TM, width), lambda i: (i, 0))
    const = _resident
    return pl.pallas_call(
        _pre_mixer_kernel,
        grid=(n // FFN_TM,),
        in_specs=[tok(D_MODEL), const(g_fpre), const(fw_in, (layer, 0)), const(fw_out, (layer, 0)), const(g_fpost),
                  const(g_mix), const(w, (layer,)), const(colscale)],
        out_specs=[tok(D_MODEL)] + [tok(wd) for wd in widths],
        out_shape=[jax.ShapeDtypeStruct((n, wd), F32) for wd in (D_MODEL,) + widths],
        compiler_params=_cparams(("parallel",), 56),
        name="pre_mixer",
    )(x, g_fpre, fw_in, fw_out, g_fpost, g_mix, w, colscale)


def _norm_mm_kernel(x_ref, g_ref, w_ref, o_ref):
    h = _rms(x_ref[...], g_ref[...]).astype(BF16)
    o_ref[...] = jnp.dot(h, w_ref[...], preferred_element_type=F32).astype(o_ref.dtype)


def _norm_mm(x, g, w, layer, tm, out_dtype):
    n, ncol = x.shape[0], w.shape[2]
    return pl.pallas_call(
        _norm_mm_kernel,
        grid=(n // tm,),
        in_specs=[pl.BlockSpec((tm, D_MODEL), lambda i: (i, 0)), _resident(g), _resident(w, (layer,))],
        out_specs=pl.BlockSpec((tm, ncol), lambda i: (i, 0)),
        out_shape=jax.ShapeDtypeStruct((n, ncol), out_dtype),
        compiler_params=_cparams(("parallel",), 32),
        name="norm_mm",
    )(x, g, w)


def _softmax_pv(q, k, v, bias):
    s = lax.dot_general(q, k, (((1,), (1,)), ((), ())), preferred_element_type=F32) + bias
    m = jnp.max(s, axis=-1, keepdims=True)
    p = jnp.exp2(s - m).astype(BF16)
    o = jnp.dot(p, jnp.concatenate([v, jnp.ones_like(v)], axis=1), preferred_element_type=F32)
    return o[:, :LANES], m, o[:, LANES:]


def _pair_attention(q, k, v, bias, stack_heads):
    bq = q.shape[0]
    lane = lax.broadcasted_iota(jnp.int32, (1, LANES), 1)
    first = lane < HEAD_DIM
    q0 = jnp.where(first, q, 0.0).astype(BF16)
    q1 = jnp.where(first, 0.0, q).astype(BF16)
    if stack_heads:
        res = _softmax_pv(jnp.concatenate([q0, q1], axis=0), k, v, bias)
        return tuple(jnp.where(first, a[:bq], a[bq:]) for a in res)
    res0 = _softmax_pv(q0, k, v, bias[:bq])
    res1 = _softmax_pv(q1, k, v, bias[bq:])
    return tuple(jnp.where(first, a, b) for a, b in zip(res0, res1))


def _na_key_col0(g):
    return int(np.clip(g * NA_GQ - NA_COLS // 2, 0, GRID_W - NA_GK))


def _na_fill_bias(tz_ref, bias_scr, j):
    nrows = SEQ // GRID_W
    start_row = int(np.clip(j * NA_QROWS - NA_ROWS // 2, 0, nrows - NA_KROWS))
    neg = jnp.full((NA_GQ, NA_GK), NEG_INF, F32)
    per_tile = LANES // NA_GK
    for g in range(NA_GROUPS):
        for hh in range(2):
            for rl in range(NA_QROWS):
                r = j * NA_QROWS + rl
                rs = int(np.clip(r - NA_ROWS // 2, 0, nrows - NA_ROWS))
                r0 = (hh * NA_QROWS + rl) * NA_GQ
                for kt in range(NA_KROWS // per_tile):
                    tiles = []
                    for kl in range(kt * per_tile, (kt + 1) * per_tile):
                        kr = start_row + kl
                        tiles.append(tz_ref[hh, kr - r + NA_ROWS - 1, g] if rs <= kr < rs + NA_ROWS else neg)
                    bias_scr[g, r0:r0 + NA_GQ, kt * LANES:(kt + 1) * LANES] = jnp.concatenate(tiles, axis=1)


def _na_row_block(q_ref, k_ref, v_ref, o_ref, bias_scr, j, q0):
    nrows = SEQ // GRID_W
    nblk = SEQ // NA_BQ
    table = jnp.where(j == 0, 0, jnp.where(j == nblk - 1, 2, 1))
    start_row = jnp.clip(j * NA_QROWS - NA_ROWS // 2, 0, nrows - NA_KROWS)
    start = start_row * GRID_W
    for g in range(NA_GROUPS):
        kc0 = _na_key_col0(g)
        krows = [pl.ds(pl.multiple_of(start + kl * GRID_W + kc0, 8), NA_GK) for kl in range(NA_KROWS)]
        qrows = [slice(q0 + rl * GRID_W + g * NA_GQ, q0 + rl * GRID_W + (g + 1) * NA_GQ) for rl in range(NA_QROWS)]
        k = jnp.concatenate([k_ref[0, r, :] for r in krows], axis=0).astype(BF16)
        v = jnp.concatenate([v_ref[0, r, :] for r in krows], axis=0).astype(BF16)
        q = jnp.concatenate([q_ref[0, r, :] for r in qrows], axis=0)
        o, _, l = _pair_attention(q, k, v, bias_scr[table, g], stack_heads=True)
        o = o / l
        for rl, r in enumerate(qrows):
            o_ref[0, r, :] = o[rl * NA_GQ:(rl + 1) * NA_GQ]


def _na_kernel(q_ref, k_ref, v_ref, tz_ref, o_ref, bias_scr):
    nblk = SEQ // NA_BQ

    @pl.when((pl.program_id(1) == 0) & (pl.program_id(2) == 0))
    def _():
        for table, j in enumerate((0, 1, nblk - 1)):
            _na_fill_bias(tz_ref, bias_scr.at[table], j)

    step = pl.program_id(2)
    _na_row_block(q_ref, k_ref, v_ref, o_ref, bias_scr, 2 * step, 0)
    _na_row_block(q_ref, k_ref, v_ref, o_ref, bias_scr, 2 * step + 1, NA_BQ)


def _na_attention(pa, tzg):
    b = pa.shape[0]
    npair = NA_WIDTH // LANES
    nstep = SEQ // (2 * NA_BQ)
    return pl.pallas_call(
        _na_kernel,
        grid=(npair, b, nstep),
        in_specs=[
            pl.BlockSpec((1, 2 * NA_BQ, LANES), lambda p, bi, j: (bi, j, p)),
            pl.BlockSpec((1, SEQ, LANES), lambda p, bi, j: (bi, 0, npair + p)),
            pl.BlockSpec((1, SEQ, LANES), lambda p, bi, j: (bi, 0, 2 * npair + p)),
            pl.BlockSpec((2,) + tzg.shape[1:], lambda p, bi, j: (p, 0, 0, 0, 0)),
        ],
        out_specs=pl.BlockSpec((1, 2 * NA_BQ, LANES), lambda p, bi, j: (bi, j, p)),
        out_shape=jax.ShapeDtypeStruct((b, SEQ, NA_WIDTH), F32),
        scratch_shapes=[pltpu.VMEM((3, NA_GROUPS, 2 * NA_QROWS * NA_GQ, NA_KROWS * NA_GK), F32)],
        compiler_params=_cparams(("parallel", "arbitrary", "arbitrary"), 48),
        name="na_attn",
    )(pa, pa, pa, tzg)


def _na_tz(rpb):
    col = np.arange(GRID_W)
    cs = np.clip(col - NA_COLS // 2, 0, GRID_W - NA_COLS)
    col_ok = (col[None, :] >= cs[:, None]) & (col[None, :] < cs[:, None] + NA_COLS)
    period = GRID_W + 1
    half = NA_COLS - 1
    filler = jnp.zeros(rpb.shape[:2] + (period - (2 * half + 1),), rpb.dtype)
    v = jnp.concatenate([rpb[..., half:], filler, rpb[..., :half]], axis=-1)
    t = jnp.tile(v, (1, 1, GRID_W))[..., :GRID_W * GRID_W].reshape(rpb.shape[:2] + (GRID_W, GRID_W))
    t = jnp.where(col_ok[None, None], t * LOG2E, NEG_INF)
    groups = []
    for g in range(NA_GROUPS):
        kc0 = _na_key_col0(g)
        assert col_ok[g * NA_GQ:(g + 1) * NA_GQ, :kc0].sum() == 0 and col_ok[g * NA_GQ:(g + 1) * NA_GQ, kc0 + NA_GK:].sum() == 0
        groups.append(t[:, :, g * NA_GQ:(g + 1) * NA_GQ, kc0:kc0 + NA_GK])
    return jnp.stack(groups, axis=2)


def _dil_kernel(q_ref, k_ref, v_ref, bias_ref, o_ref, m_scr, l_scr):
    order = sorted(range(len(DIL_CONFIGS)), key=lambda c: -DIL_CONFIGS[c][1])
    for c in order:
        dil = DIL_CONFIGS[c][1]
        m_len = SEQ // dil
        nblk = m_len // DIL_BQ

        def merge(qrows, o, m, l, c=c):
            if c == order[0]:
                o_ref[0, qrows, :] = o
                m_scr[qrows, :] = m
                l_scr[qrows, :] = l
                return
            m_old = m_scr[qrows, :]
            m_new = jnp.maximum(m_old, m)
            a_old = jnp.exp2(m_old - m_new)
            a_new = jnp.exp2(m - m_new)
            o_new = o_ref[0, qrows, :] * a_old + o * a_new
            l_new = l_scr[qrows, :] * a_old + l * a_new
            if c == order[-1]:
                o_ref[0, qrows, :] = o_new / l_new
            else:
                o_ref[0, qrows, :] = o_new
                l_scr[qrows, :] = l_new
                m_scr[qrows, :] = m_new

        def window(t, lo, pos, hi):
            return jnp.concatenate([t(lo, DIL_HALF), t(pos, DIL_BQ), t(hi, DIL_HALF)], axis=0)

        def block_body(it, carry, c=c, dil=dil, m_len=m_len, nblk=nblk):
            g = it // nblk
            n = it % nblk
            pos = n * DIL_BQ

            def rows(p0, size):
                return pl.ds(g + p0 * dil, size, stride=dil) if dil > 1 else pl.ds(p0, size)

            lo = jnp.maximum(pos - DIL_HALF, 0)
            hi = jnp.minimum(pos + DIL_BQ, m_len - DIL_HALF)
            k = window(lambda p0, size: k_ref[0, rows(p0, size), :], lo, pos, hi).astype(BF16)
            v = window(lambda p0, size: v_ref[0, rows(p0, size), :], lo, pos, hi).astype(BF16)
            edge_case = jnp.where(n == 0, 0, jnp.where(n == nblk - 1, 2, 1))
            qrows = rows(pos, DIL_BQ)
            merge(qrows, *_pair_attention(q_ref[0, qrows, :], k, v, bias_ref[c, 0, edge_case], stack_heads=True))
            return carry

        def class_body(g, carry, c=c, dil=dil, m_len=m_len, nblk=nblk):
            rows = pl.ds(g, m_len, stride=dil)
            q = q_ref[0, rows, :]
            kc = k_ref[0, rows, :].astype(BF16)
            vc = v_ref[0, rows, :].astype(BF16)
            res = []
            for n in range(nblk):
                pos = n * DIL_BQ
                lo, hi = max(pos - DIL_HALF, 0), min(pos + DIL_BQ, m_len - DIL_HALF)
                k = window(lambda p0, size: kc[p0:p0 + size], lo, pos, hi)
                v = window(lambda p0, size: vc[p0:p0 + size], lo, pos, hi)
                edge_case = 0 if n == 0 else 2 if n == nblk - 1 else 1
                res.append(_pair_attention(q[pos:pos + DIL_BQ], k, v, bias_ref[c, 0, edge_case], stack_heads=True))
            merge(rows, *(jnp.concatenate(parts, axis=0) for parts in zip(*res)))
            return carry

        if nblk <= DIL_UNROLL:
            lax.fori_loop(0, dil, class_body, 0, unroll=DIL_UNROLL // nblk)
        else:
            lax.fori_loop(0, SEQ // DIL_BQ, block_body, 0, unroll=DIL_UNROLL)


def _dil_attention(pc, bias):
    b = pc.shape[0]
    npair = DIL_WIDTH // LANES
    assert all(SEQ // dil // DIL_BQ >= 2 for _, dil in DIL_CONFIGS)
    return pl.pallas_call(
        _dil_kernel,
        grid=(b, npair),
        in_specs=[
            pl.BlockSpec((1, SEQ, LANES), lambda bi, p: (bi, 0, p)),
            pl.BlockSpec((1, SEQ, LANES), lambda bi, p: (bi, 0, npair + p)),
            pl.BlockSpec((1, SEQ, LANES), lambda bi, p: (bi, 0, 2 * npair + p)),
            pl.BlockSpec((len(DIL_CONFIGS), 1, 3, 2 * DIL_BQ, DIL_BK), lambda bi, p: (0, p, 0, 0, 0)),
        ],
        out_specs=pl.BlockSpec((1, SEQ, LANES), lambda bi, p: (bi, 0, p)),
        out_shape=jax.ShapeDtypeStruct((b, SEQ, DIL_WIDTH), F32),
        scratch_shapes=[pltpu.VMEM((SEQ, LANES), F32), pltpu.VMEM((SEQ, LANES), F32)],
        compiler_params=_cparams(("parallel", "arbitrary"), 56),
        name="dil_attn",
    )(pc, pc, pc, bias)


def _t5_bucket(rel):
    half = T5_BUCKETS // 2
    exact = half // 2
    n = np.abs(rel)
    far = exact + (np.log(np.maximum(n, 1) / exact) / math.log(T5_MAX_DIST / exact) * (half - exact)).astype(np.int32)
    far = np.minimum(far, half - 1)
    return (np.where(rel > 0, half, 0) + np.where(n < exact, n, far)).astype(np.int32)


def _dil_bias(t5_table):
    rel = np.arange(-DIL_HALF, DIL_HALF + 1)
    period = DIL_BK + 1
    nh = t5_table.shape[1]
    out = []
    for _, dil in DIL_CONFIGS:
        vals = t5_table[_t5_bucket(dil * rel)].T
        v = jnp.concatenate([vals, jnp.full((nh, period - vals.shape[1]), NEG_INF, vals.dtype)], axis=1)
        out.append(jnp.tile(v, (1, DIL_BQ))[:, :DIL_BQ * DIL_BK].reshape(nh, DIL_BQ, DIL_BK))
    t = (jnp.stack(out) * LOG2E).reshape(len(DIL_CONFIGS), nh // 2, 1, 2 * DIL_BQ, DIL_BK)
    kcol = np.arange(DIL_BK)
    edge = np.stack([kcol < DIL_HALF, np.zeros_like(kcol, bool), kcol >= DIL_HALF + DIL_BQ])
    return jnp.where(edge[None, None, :, None, :], NEG_INF, t)


CONV_TM = 1024


def _conv_gate_kernel(p_ref, prev_ref, next_ref, w_ref, b_ref, z_ref, x0_ref):
    i = pl.program_id(1)
    p = p_ref[0]
    tm = p.shape[0]
    first = jnp.where(i > 0, prev_ref[0, 7:8, :], 0.0)
    last = jnp.where(i < pl.num_programs(1) - 1, next_ref[0, 0:1, :], 0.0)
    row = lax.broadcasted_iota(jnp.int32, (tm, 1), 0)
    up = jnp.where(row == 0, first, pltpu.roll(p, 1, axis=0))
    dn = jnp.where(row == tm - 1, last, pltpu.roll(p, tm - 1, axis=0))
    uc = up * w_ref[0:1, :] + p * w_ref[1:2, :] + dn * w_ref[2:3, :] + b_ref[...]
    z_ref[0] = uc[:, HY_WIDTH:2 * HY_WIDTH] * uc[:, :HY_WIDTH]
    x0_ref[0] = uc[:, 2 * HY_WIDTH:]


def _conv_gate(pb, w, bvec):
    b = pb.shape[0]
    wb = 3 * HY_WIDTH
    nt = SEQ // CONV_TM
    r8 = CONV_TM // 8
    return pl.pallas_call(
        _conv_gate_kernel,
        grid=(b, nt),
        in_specs=[
            pl.BlockSpec((1, CONV_TM, wb), lambda bi, i: (bi, i, 0)),
            pl.BlockSpec((1, 8, wb), lambda bi, i: (bi, jnp.maximum(i * r8 - 1, 0), 0)),
            pl.BlockSpec((1, 8, wb), lambda bi, i: (bi, jnp.minimum((i + 1) * r8, SEQ // 8 - 1), 0)),
            pl.BlockSpec((3, wb), lambda bi, i: (0, 0)),
            pl.BlockSpec((1, wb), lambda bi, i: (0, 0)),
        ],
        out_specs=[
            pl.BlockSpec((1, CONV_TM, HY_WIDTH), lambda bi, i: (bi, i, 0)),
            pl.BlockSpec((1, CONV_TM, HY_WIDTH), lambda bi, i: (bi, i, 0)),
        ],
        out_shape=[jax.ShapeDtypeStruct((b, SEQ, HY_WIDTH), F32)] * 2,
        compiler_params=_cparams(("parallel", "parallel"), 48),
        name="hy_conv_gate",
    )(pb, pb, pb, w, bvec)


FILT_TM = 1024
FILT_KPAD = 128
FILT_HALF = SEQ // 2


def _filter_kernel(z_ref, w1_ref, b1_ref, w2_ref, b2_ref, w3_ref, b3_ref, w4_ref, fr_ref, dec_ref, h_ref):
    hp = lax.Precision.HIGHEST
    z = z_ref[...]
    fr = fr_ref[...]
    h = jnp.sin(fr * (jnp.dot(z, w1_ref[...], precision=hp, preferred_element_type=F32) + b1_ref[...]))
    h = jnp.sin(fr * (jnp.dot(h, w2_ref[...], precision=hp, preferred_element_type=F32) + b2_ref[...]))
    h = jnp.sin(fr * (jnp.dot(h, w3_ref[...], precision=hp, preferred_element_type=F32) + b3_ref[...]))
    h = jnp.dot(h, w4_ref[...], precision=hp, preferred_element_type=F32)
    wout = 2 * HY_WIDTH
    dec = jnp.abs(dec_ref[...])
    lo = h[:, :wout] * jnp.exp(-z[:, 0:1] * dec)
    hi = h[:, wout:] * jnp.exp(-z[:, FILT_KPAD:FILT_KPAD + 1] * dec)
    row = pl.program_id(0) * FILT_TM + lax.broadcasted_iota(jnp.int32, (FILT_TM, 1), 0)
    col = lax.broadcasted_iota(jnp.int32, (1, wout), 1)
    h_ref[0] = jnp.where((row == 0) & (col >= HY_WIDTH), 0.0, lo)
    h_ref[1] = hi


def _filter_features():
    t = jnp.linspace(0.0, 1.0, SEQ, dtype=F32)[:, None]
    bands = jnp.linspace(1e-4, HY_BANDS - 1, HY_BANDS, dtype=F32)[None, :]
    ang = (2.0 * math.pi / SEQ) * jnp.arange(SEQ, dtype=F32)[:, None] * bands
    z = jnp.concatenate([t, jnp.cos(ang), -jnp.sin(ang)], axis=-1)
    z = jnp.pad(z, ((0, 0), (0, FILT_KPAD - z.shape[1])))
    return jnp.concatenate([z[:FILT_HALF], z[FILT_HALF:]], axis=1)


def _hyena_filters(zfeat, w1, b1, w2, b2, w3, b3, w4, freq, decay):
    def diag2(w):
        zero = jnp.zeros_like(w)
        return jnp.concatenate([jnp.concatenate([w, zero], axis=1), jnp.concatenate([zero, w], axis=1)], axis=0)

    twice = lambda v: jnp.concatenate([v, v])[None]
    w1p = jnp.pad(w1, ((0, FILT_KPAD - w1.shape[0]), (0, 0)))
    full = lambda a: pl.BlockSpec(a.shape, lambda i: (0,) * a.ndim)
    args = (diag2(w1p), twice(b1), diag2(w2), twice(b2), diag2(w3), twice(b3), diag2(w4), twice(freq), decay[None])
    out = pl.pallas_call(
        _filter_kernel,
        grid=(FILT_HALF // FILT_TM,),
        in_specs=[pl.BlockSpec((FILT_TM, 2 * FILT_KPAD), lambda i: (i, 0))] + [full(a) for a in args],
        out_specs=pl.BlockSpec((2, FILT_TM, 2 * HY_WIDTH), lambda i: (0, i, 0)),
        out_shape=jax.ShapeDtypeStruct((2, FILT_HALF, 2 * HY_WIDTH), F32),
        compiler_params=_cparams(("parallel",), 32),
        name="hy_filter",
    )(zfeat, *args)
    return out.reshape(1, SEQ, 2 * HY_WIDTH)


def _dft_tables():
    a = np.arange(DFT_R)
    w = np.exp(-2j * np.pi * np.outer(a, a) / DFT_R)
    tw = np.exp(-2j * np.pi * np.outer(a, a) / DFT_N)
    f32 = lambda x: jnp.asarray(x.astype(np.float32))
    f1 = np.concatenate([w.real[:, :DFT_HALF], w.imag[:, :DFT_HALF]], axis=0)
    m1 = np.concatenate([w.real[:DFT_HALF], w.imag[:DFT_HALF]], axis=1) / DFT_N
    c = w[None] * tw[:, :DFT_KB].T[:, None, :]
    blocks = lambda a, b, c, d: np.concatenate(
        [np.concatenate([a, b], axis=2), np.concatenate([c, d], axis=2)], axis=1)
    qu = blocks(c.real, -c.imag, c.imag, c.real)
    ct = np.swapaxes(c, 1, 2)
    gu = blocks(ct.real, ct.imag, -ct.imag, ct.real)
    ta = np.exp(-2j * np.pi * DFT_KB * np.outer(np.arange(DFT_R // DFT_KB), a) / DFT_N)
    ta = np.concatenate([ta.real, ta.imag], axis=1)[:, :, None] * np.ones((1, 1, LANES))
    bf = lambda x: f32(x).astype(BF16)
    return bf(f1), bf(qu), bf(gu), bf(m1), f32(ta)


DFT_KB = 16
DFT_NB = 8


def _dft_stage1(x_ref, f1_ref, s_scr):
    def body(i, carry):
        n2b = i * DFT_NB
        xs = jnp.concatenate([x_ref[0, pl.ds(n2b + u, DFT_HALF, stride=DFT_R), :] for u in range(DFT_NB)],
                             axis=1).astype(BF16)
        a = jnp.dot(f1_ref[...], xs, preferred_element_type=F32)
        for u in range(DFT_NB):
            s_scr[pl.ds(pl.multiple_of((n2b + u) * DFT_PITCH, 8), 2 * DFT_R), :] = a[:, u * LANES:(u + 1) * LANES]
        return carry

    lax.fori_loop(0, DFT_R // DFT_NB, body, 0)


def _dft_stage2_rows(s_scr, ta_ref, step, u):
    k1 = step * DFT_KB + u
    re = s_scr[pl.ds(k1, DFT_R, stride=DFT_PITCH), :]
    im = s_scr[pl.ds(DFT_R + k1, DFT_R, stride=DFT_PITCH), :]
    tr, ti = ta_ref[step, :DFT_R, :], ta_ref[step, DFT_R:, :]
    return jnp.concatenate([re * tr - im * ti, re * ti + im * tr], axis=0).astype(BF16)


def _dft_fwd_kernel(x_ref, f1_ref, qu_ref, ta_ref, o_ref, s_scr):
    step = pl.program_id(2)
    pl.when(step == 0)(functools.partial(_dft_stage1, x_ref, f1_ref, s_scr))
    for u in range(DFT_KB):
        a = _dft_stage2_rows(s_scr, ta_ref, step, u)
        o_ref[0, u] = jnp.dot(qu_ref[u], a, preferred_element_type=F32).astype(o_ref.dtype)


def _dft_fwd(x, f1, qu, ta, out_dtype):
    nb, _, c = x.shape
    full = lambda a: pl.BlockSpec(a.shape, lambda b, ct, s: (0,) * a.ndim)
    return pl.pallas_call(
        _dft_fwd_kernel,
        grid=(nb, c // LANES, DFT_R // DFT_KB),
        in_specs=[pl.BlockSpec((1, SEQ, LANES), lambda b, ct, s: (b, 0, ct)), full(f1), full(qu), full(ta)],
        out_specs=pl.BlockSpec((1, DFT_KB, 2 * DFT_R, LANES), lambda b, ct, s: (b, s, 0, ct)),
        out_shape=jax.ShapeDtypeStruct((nb, DFT_R, 2 * DFT_R, c), out_dtype),
        scratch_shapes=[pltpu.VMEM((DFT_R * DFT_PITCH, LANES), F32)],
        compiler_params=_cparams(("parallel", "parallel", "arbitrary"), 40),
        name="hy_dft_fwd",
    )(x, f1, qu, ta)


def _dft_stage1_inv(s_scr, m1_ref, o_ref):
    def body(i, carry):
        n2b = i * DFT_NB
        bc = jnp.concatenate(
            [s_scr[pl.ds(pl.multiple_of((n2b + u) * DFT_PITCH, 8), 2 * DFT_R), :] for u in range(DFT_NB)],
            axis=1).astype(BF16)
        y = jnp.dot(m1_ref[...], bc, preferred_element_type=F32)
        for u in range(DFT_NB):
            o_ref[0, pl.ds(n2b + u, DFT_HALF, stride=DFT_R), :] = y[:, u * LANES:(u + 1) * LANES]
        return carry

    lax.fori_loop(0, DFT_R // DFT_NB, body, 0)


def _dft_conv_kernel(x_ref, f1_ref, qu_ref, ta_ref, hf_ref, hb_ref, gu_ref, m1_ref, o_ref, s_scr):
    step = pl.program_id(2)
    pl.when(step == 0)(functools.partial(_dft_stage1, x_ref, f1_ref, s_scr))
    tr, ti = ta_ref[step, :DFT_R, :], ta_ref[step, DFT_R:, :]
    for u in range(DFT_KB):
        k1 = step * DFT_KB + u
        xk = jnp.dot(qu_ref[u], _dft_stage2_rows(s_scr, ta_ref, step, u), preferred_element_type=F32)
        xr, xi = xk[:DFT_R], xk[DFT_R:]
        hr = hf_ref[0, u, :DFT_R, :].astype(F32) + hb_ref[0, u, :DFT_R, :].astype(F32)
        hi = hf_ref[0, u, DFT_R:, :].astype(F32) - hb_ref[0, u, DFT_R:, :].astype(F32)
        y = jnp.concatenate([xr * hr - xi * hi, xr * hi + xi * hr], axis=0).astype(BF16)
        b = jnp.dot(gu_ref[u], y, preferred_element_type=F32)
        br, bi = b[:DFT_R], b[DFT_R:]
        s_scr[pl.ds(k1, DFT_R, stride=DFT_PITCH), :] = br * tr + bi * ti
        s_scr[pl.ds(DFT_R + k1, DFT_R, stride=DFT_PITCH), :] = bi * tr - br * ti
    pl.when(step == pl.num_programs(2) - 1)(functools.partial(_dft_stage1_inv, s_scr, m1_ref, o_ref))


def _dft_conv(x, hs, f1, qu, gu, m1, ta):
    b, _, c = x.shape
    nct = c // LANES
    full = lambda a: pl.BlockSpec(a.shape, lambda bi, ct, s: (0,) * a.ndim)
    return pl.pallas_call(
        _dft_conv_kernel,
        grid=(b, nct, DFT_R // DFT_KB),
        in_specs=[
            pl.BlockSpec((1, SEQ, LANES), lambda bi, ct, s: (bi, 0, ct)),
            full(f1), full(qu), full(ta),
            pl.BlockSpec((1, DFT_KB, 2 * DFT_R, LANES), lambda bi, ct, s: (0, s, 0, ct)),
            pl.BlockSpec((1, DFT_KB, 2 * DFT_R, LANES), lambda bi, ct, s: (0, s, 0, nct + ct)),
            full(gu), full(m1),
        ],
        out_specs=pl.BlockSpec((1, SEQ, LANES), lambda bi, ct, s: (bi, 0, ct)),
        out_shape=jax.ShapeDtypeStruct((b, SEQ, c), F32),
        scratch_shapes=[pltpu.VMEM((DFT_R * DFT_PITCH, LANES), F32)],
        compiler_params=_cparams(("parallel", "parallel", "arbitrary"), 48),
        name="hy_dft_conv",
    )(x, f1, qu, ta, hs, hs, gu, m1)


def _mix_out_body(x, ya, yh, z, x0, yc, skip_ref, gg_ref, w_ref, gpost_ref):
    a0, b0 = NA_WIDTH, NA_WIDTH + HY_WIDTH
    yb = x0 * (yh + skip_ref[...] * z)
    acc = jnp.dot(_rms(ya, gg_ref[:, :a0]).astype(BF16), w_ref[:a0, :], preferred_element_type=F32)
    acc += jnp.dot(_rms(yb, gg_ref[:, a0:b0]).astype(BF16), w_ref[a0:b0, :], preferred_element_type=F32)
    acc += jnp.dot(_rms(yc, gg_ref[:, b0:]).astype(BF16), w_ref[b0:, :], preferred_element_type=F32)
    return x + _rms(acc, gpost_ref[...])


def _xattn_body(x, gpre_ref, wq_ref, kv_ref, wo_ref, gpost_ref):
    h = _rms(x, gpre_ref[...]).astype(BF16)
    q = jnp.dot(h, wq_ref[...], preferred_element_type=F32) * (XA_HEAD_DIM ** -0.5)
    outs = []
    for hd in range(XA_HEADS):
        c0 = hd * XA_HEAD_DIM
        qh = q[:, c0:c0 + XA_HEAD_DIM].astype(BF16)
        kh = kv_ref[0, :, c0:c0 + XA_HEAD_DIM]
        vh = kv_ref[0, :, D_MODEL + c0:D_MODEL + c0 + XA_HEAD_DIM]
        s = lax.dot_general(qh, kh, (((1,), (1,)), ((), ())), preferred_element_type=F32)
        p = jnp.exp(s - jnp.max(s, axis=-1, keepdims=True))
        l = jnp.sum(p, axis=-1, keepdims=True)
        outs.append(jnp.dot(p.astype(BF16), vh, preferred_element_type=F32) / l)
    o = jnp.concatenate(outs, axis=-1).astype(BF16)
    xa = jnp.dot(o, wo_ref[...], preferred_element_type=F32)
    return x + _rms(xa, gpost_ref[...])


POST_TM = 512


def _post_mixer_kernel(x_ref, ya_ref, yh_ref, z_ref, x0_ref, yc_ref, skip_ref, gg_ref, wout_ref, gmix_ref,
                       gxpre_ref, wq_ref, kv_ref, wo_ref, gxpost_ref, gfpre_ref, fwin_ref, fwout_ref, gfpost_ref,
                       o_ref):
    x = _mix_out_body(x_ref[...], ya_ref[...], yh_ref[...], z_ref[...], x0_ref[...], yc_ref[...],
                      skip_ref, gg_ref, wout_ref, gmix_ref)
    x = _xattn_body(x, gxpre_ref, wq_ref, kv_ref, wo_ref, gxpost_ref)
    o_ref[...] = _ffn_body(x, gfpre_ref, fwin_ref, fwout_ref, gfpost_ref)


def _post_mixer(x, layer, ya, yh, z, x0, yc, skip, gg, w_out, g_mix, g_xpre, wq, kv, wo, g_xpost, g_fpre, fw_in,
                fw_out, g_fpost):
    n = x.shape[0]
    per_batch = SEQ // POST_TM
    tok = lambda width: pl.BlockSpec((POST_TM, width), lambda i: (i, 0))
    const = _resident
    kv_spec = pl.BlockSpec((1,) + kv.shape[1:], lambda i: (i // per_batch, 0, 0))
    return pl.pallas_call(
        _post_mixer_kernel,
        grid=(n // POST_TM,),
        in_specs=[tok(D_MODEL), tok(NA_WIDTH), tok(HY_WIDTH), tok(HY_WIDTH), tok(HY_WIDTH), tok(DIL_WIDTH),
                  const(skip), const(gg), const(w_out, (layer,)), const(g_mix),
                  const(g_xpre), const(wq, (layer,)), kv_spec, const(wo, (layer,)), const(g_xpost),
                  const(g_fpre), const(fw_in, (layer, 1)), const(fw_out, (layer, 1)), const(g_fpost)],
        out_specs=tok(D_MODEL),
        out_shape=jax.ShapeDtypeStruct((n, D_MODEL), F32),
        compiler_params=_cparams(("parallel",), 56),
        name="post_mixer",
    )(x, ya, yh, z, x0, yc, skip, gg, w_out, g_mix, g_xpre, wq, kv, wo, g_xpost, g_fpre, fw_in, fw_out, g_fpost)


def kernel(x, mem, norm_g, w_in, w_out, na_rpb, t5_table, hy_conv_w, hy_conv_b, hy_f_w1, hy_f_b1, hy_f_w2, hy_f_b2, hy_f_w3, hy_f_b3, hy_f_w4, hy_freq, hy_decay, hy_skip, xa_wq, xa_wkv, xa_wo, ffn_w_in, ffn_w_out):
    bsz, seq, d = x.shape
    assert (seq, d) == (SEQ, D_MODEL)
    depth = norm_g.shape[0]
    mem_len = mem.shape[1]
    n = bsz * seq
    scale = HEAD_DIM ** -0.5 * LOG2E
    wa, wb, wc = 3 * NA_WIDTH, 3 * HY_WIDTH, 3 * DIL_WIDTH
    colscale = np.ones((1, wa + wb + wc), np.float32)
    colscale[:, :NA_WIDTH] = scale
    colscale[:, wa + wb:wa + wb + DIL_WIDTH] = scale
    colscale = jnp.asarray(colscale)

    zfeat = _filter_features()
    f1, qu, gu, m1, ta = _dft_tables()
    dil_bias = _dil_bias(t5_table)

    xf = x.reshape(n, d)
    memf = mem.reshape(bsz * mem_len, d)
    ffn_w_in, ffn_w_out, w_in, w_out, xa_wq, xa_wkv, xa_wo = (
        w.astype(BF16) for w in (ffn_w_in, ffn_w_out, w_in, w_out, xa_wq, xa_wkv, xa_wo))
    for l in range(depth):
        g = norm_g[l][:, None, :]
        xf, pa, pb, pc = _pre_mixer(xf, l, g[0], ffn_w_in, ffn_w_out, g[1], g[2], w_in, colscale)
        pa = pa.reshape(bsz, seq, wa)
        pb = pb.reshape(bsz, seq, wb)
        pc = pc.reshape(bsz, seq, wc)
        ya = _na_attention(pa, _na_tz(na_rpb[l]))
        z, x0 = _conv_gate(pb, hy_conv_w[l], hy_conv_b[l][None])
        hfilt = _hyena_filters(zfeat, hy_f_w1[l], hy_f_b1[l], hy_f_w2[l], hy_f_b2[l], hy_f_w3[l], hy_f_b3[l],
                               hy_f_w4[l], hy_freq[l], hy_decay[l])
        yh = _dft_conv(z, _dft_fwd(hfilt, f1, qu, ta, BF16), f1, qu, gu, m1, ta)
        yc = _dil_attention(pc, dil_bias)
        kv = _norm_mm(memf, g[6], xa_wkv, l, mem_len, BF16).reshape(bsz, mem_len, 2 * D_MODEL)
        xf = _post_mixer(xf, l, ya.reshape(n, NA_WIDTH), yh.reshape(n, HY_WIDTH), z.reshape(n, HY_WIDTH),
                         x0.reshape(n, HY_WIDTH), yc.reshape(n, DIL_WIDTH), hy_skip[l][None], g[3],
                         w_out, g[4], g[5], xa_wq, kv, xa_wo, g[7], g[8], ffn_w_in, ffn_w_out, g[9])
    return xf.reshape(bsz, seq, d)
```

```python
import functools
import math

import numpy as np
import jax
import jax.numpy as jnp
from jax import lax
from jax.experimental import pallas as pl
from jax.experimental.pallas import tpu as pltpu

F32 = jnp.float32
BF16 = jnp.bfloat16

D_MODEL = 1024
SEQ = 8192
GRID_W = 64
HEAD_DIM = 64
NA_WIDTH = 384
HY_WIDTH = 256
DIL_WIDTH = 384
NA_ROWS = 8
NA_COLS = 16
HY_BANDS = 16
DIL_CONFIGS = ((128, 1), (512, 4), (2048, 16))
T5_BUCKETS = 32
T5_MAX_DIST = 1024
XA_HEADS = 4
XA_HEAD_DIM = 256
FFN_HIDDEN = 2816
RMS_EPS = 1e-6
NEG_INF = -1e30
LOG2E = math.log2(math.e)

LANES = 128
MIB = 1024 * 1024

NA_QROWS = 8
NA_KROWS = 16
NA_BQ = NA_QROWS * GRID_W
NA_GROUPS = 4
NA_GQ = GRID_W // NA_GROUPS
NA_GK = 2 * NA_GQ
DIL_HALF = 64
DIL_BQ = 128
DIL_UNROLL = 8
DIL_BK = DIL_BQ + 2 * DIL_HALF
DFT_R = 128
DFT_HALF = 64
DFT_N = DFT_R * DFT_R
DFT_PITCH = 264


def _cparams(sem, vmem_mib):
    return pltpu.CompilerParams(dimension_semantics=sem, vmem_limit_bytes=vmem_mib * MIB)


def _resident(a, lead=()):
    k = len(lead)
    return pl.BlockSpec((None,) * k + a.shape[k:], lambda i: tuple(lead) + (0,) * (a.ndim - k),
                        pipeline_mode=pl.Buffered(1))


def _rms(x, g):
    return x * lax.rsqrt(jnp.mean(x * x, axis=-1, keepdims=True) + RMS_EPS) * g


FFN_TM = 512
FFN_TH = 256


def _ffn_body(x, gpre_ref, win_ref, wout_ref, gpost_ref):
    h = _rms(x, gpre_ref[...]).astype(BF16)
    acc = jnp.zeros(x.shape, F32)
    for j in range(FFN_HIDDEN // FFN_TH):
        c0 = j * FFN_TH
        gate = jnp.dot(h, win_ref[:, c0:c0 + FFN_TH], preferred_element_type=F32)
        up = jnp.dot(h, win_ref[:, FFN_HIDDEN + c0:FFN_HIDDEN + c0 + FFN_TH], preferred_element_type=F32)
        act = (gate * jax.nn.sigmoid(gate) * up).astype(BF16)
        acc += jnp.dot(act, wout_ref[c0:c0 + FFN_TH, :], preferred_element_type=F32)
    return x + 0.5 * _rms(acc, gpost_ref[...])


def _pre_mixer_kernel(x_ref, gfpre_ref, fwin_ref, fwout_ref, gfpost_ref, gmix_ref, w_ref, cs_ref,
                      o_ref, pa_ref, pb_ref, pc_ref):
    x = _ffn_body(x_ref[...], gfpre_ref, fwin_ref, fwout_ref, gfpost_ref)
    o_ref[...] = x
    h = _rms(x, gmix_ref[...]).astype(BF16)
    c0 = 0
    for ref in (pa_ref, pb_ref, pc_ref):
        c1 = c0 + ref.shape[1]
        ref[...] = jnp.dot(h, w_ref[:, c0:c1], preferred_element_type=F32) * cs_ref[:, c0:c1]
        c0 = c1


def _pre_mixer(x, layer, g_fpre, fw_in, fw_out, g_fpost, g_mix, w, colscale):
    n = x.shape[0]
    widths = (3 * NA_WIDTH, 3 * HY_WIDTH, 3 * DIL_WIDTH)
    tok = lambda width: pl.BlockSpec((FFN_TM, width), lambda i: (i, 0))
    const = _resident
    return pl.pallas_call(
        _pre_mixer_kernel,
        grid=(n // FFN_TM,),
        in_specs=[tok(D_MODEL), const(g_fpre), const(fw_in, (layer, 0)), const(fw_out, (layer, 0)), const(g_fpost),
                  const(g_mix), const(w, (layer,)), const(colscale)],
        out_specs=[tok(D_MODEL)] + [tok(wd) for wd in widths],
        out_shape=[jax.ShapeDtypeStruct((n, wd), F32) for wd in (D_MODEL,) + widths],
        compiler_params=_cparams(("parallel",), 56),
        name="pre_mixer",
    )(x, g_fpre, fw_in, fw_out, g_fpost, g_mix, w, colscale)


def _norm_mm_kernel(x_ref, g_ref, w_ref, o_ref):
    h = _rms(x_ref[...], g_ref[...]).astype(BF16)
    o_ref[...] = jnp.dot(h, w_ref[...], preferred_element_type=F32).astype(o_ref.dtype)


def _norm_mm(x, g, w, layer, tm, out_dtype):
    n, ncol = x.shape[0], w.shape[2]
    return pl.pallas_call(
        _norm_mm_kernel,
        grid=(n // tm,),
        in_specs=[pl.BlockSpec((tm, D_MODEL), lambda i: (i, 0)), _resident(g), _resident(w, (layer,))],
        out_specs=pl.BlockSpec((tm, ncol), lambda i: (i, 0)),
        out_shape=jax.ShapeDtypeStruct((n, ncol), out_dtype),
        compiler_params=_cparams(("parallel",), 32),
        name="norm_mm",
    )(x, g, w)


def _softmax_pv(q, k, v, bias):
    s = lax.dot_general(q, k, (((1,), (1,)), ((), ())), preferred_element_type=F32) + bias
    m = jnp.max(s, axis=-1, keepdims=True)
    p = jnp.exp2(s - m).astype(BF16)
    o = jnp.dot(p, jnp.concatenate([v, jnp.ones_like(v)], axis=1), preferred_element_type=F32)
    return o[:, :LANES], m, o[:, LANES:]


def _pair_attention(q, k, v, bias, stack_heads):
    bq = q.shape[0]
    lane = lax.broadcasted_iota(jnp.int32, (1, LANES), 1)
    first = lane < HEAD_DIM
    q0 = jnp.where(first, q, 0.0).astype(BF16)
    q1 = jnp.where(first, 0.0, q).astype(BF16)
    if stack_heads:
        res = _softmax_pv(jnp.concatenate([q0, q1], axis=0), k, v, bias)
        return tuple(jnp.where(first, a[:bq], a[bq:]) for a in res)
    res0 = _softmax_pv(q0, k, v, bias[:bq])
    res1 = _softmax_pv(q1, k, v, bias[bq:])
    return tuple(jnp.where(first, a, b) for a, b in zip(res0, res1))


def _na_key_col0(g):
    return int(np.clip(g * NA_GQ - NA_COLS // 2, 0, GRID_W - NA_GK))


def _na_fill_bias(tz_ref, bias_scr, j):
    nrows = SEQ // GRID_W
    start_row = int(np.clip(j * NA_QROWS - NA_ROWS // 2, 0, nrows - NA_KROWS))
    neg = jnp.full((NA_GQ, NA_GK), NEG_INF, F32)
    per_tile = LANES // NA_GK
    for g in range(NA_GROUPS):
        for hh in range(2):
            for rl in range(NA_QROWS):
                r = j * NA_QROWS + rl
                rs = int(np.clip(r - NA_ROWS // 2, 0, nrows - NA_ROWS))
                r0 = (hh * NA_QROWS + rl) * NA_GQ
                for kt in range(NA_KROWS // per_tile):
                    tiles = []
                    for kl in range(kt * per_tile, (kt + 1) * per_tile):
                        kr = start_row + kl
                        tiles.append(tz_ref[hh, kr - r + NA_ROWS - 1, g] if rs <= kr < rs + NA_ROWS else neg)
                    bias_scr[g, r0:r0 + NA_GQ, kt * LANES:(kt + 1) * LANES] = jnp.concatenate(tiles, axis=1)


def _na_row_block(q_ref, k_ref, v_ref, o_ref, bias_scr, j, q0):
    nrows = SEQ // GRID_W
    nblk = SEQ // NA_BQ
    table = jnp.where(j == 0, 0, jnp.where(j == nblk - 1, 2, 1))
    start_row = jnp.clip(j * NA_QROWS - NA_ROWS // 2, 0, nrows - NA_KROWS)
    start = start_row * GRID_W
    for g in range(NA_GROUPS):
        kc0 = _na_key_col0(g)
        krows = [pl.ds(pl.multiple_of(start + kl * GRID_W + kc0, 8), NA_GK) for kl in range(NA_KROWS)]
        qrows = [slice(q0 + rl * GRID_W + g * NA_GQ, q0 + rl * GRID_W + (g + 1) * NA_GQ) for rl in range(NA_QROWS)]
        k = jnp.concatenate([k_ref[0, r, :] for r in krows], axis=0).astype(BF16)
        v = jnp.concatenate([v_ref[0, r, :] for r in krows], axis=0).astype(BF16)
        q = jnp.concatenate([q_ref[0, r, :] for r in qrows], axis=0)
        o, _, l = _pair_attention(q, k, v, bias_scr[table, g], stack_heads=True)
        o = o / l
        for rl, r in enumerate(qrows):
            o_ref[0, r, :] = o[rl * NA_GQ:(rl + 1) * NA_GQ]


def _na_kernel(q_ref, k_ref, v_ref, tz_ref, o_ref, bias_scr):
    nblk = SEQ // NA_BQ

    @pl.when((pl.program_id(1) == 0) & (pl.program_id(2) == 0))
    def _():
        for table, j in enumerate((0, 1, nblk - 1)):
            _na_fill_bias(tz_ref, bias_scr.at[table], j)

    step = pl.program_id(2)
    _na_row_block(q_ref, k_ref, v_ref, o_ref, bias_scr, 2 * step, 0)
    _na_row_block(q_ref, k_ref, v_ref, o_ref, bias_scr, 2 * step + 1, NA_BQ)


def _na_attention(pa, tzg):
    b = pa.shape[0]
    npair = NA_WIDTH // LANES
    nstep = SEQ // (2 * NA_BQ)
    return pl.pallas_call(
        _na_kernel,
        grid=(npair, b, nstep),
        in_specs=[
            pl.BlockSpec((1, 2 * NA_BQ, LANES), lambda p, bi, j: (bi, j, p)),
            pl.BlockSpec((1, SEQ, LANES), lambda p, bi, j: (bi, 0, npair + p)),
            pl.BlockSpec((1, SEQ, LANES), lambda p, bi, j: (bi, 0, 2 * npair + p)),
            pl.BlockSpec((2,) + tzg.shape[1:], lambda p, bi, j: (p, 0, 0, 0, 0)),
        ],
        out_specs=pl.BlockSpec((1, 2 * NA_BQ, LANES), lambda p, bi, j: (bi, j, p)),
        out_shape=jax.ShapeDtypeStruct((b, SEQ, NA_WIDTH), F32),
        scratch_shapes=[pltpu.VMEM((3, NA_GROUPS, 2 * NA_QROWS * NA_GQ, NA_KROWS * NA_GK), F32)],
        compiler_params=_cparams(("parallel", "arbitrary", "arbitrary"), 48),
        name="na_attn",
    )(pa, pa, pa, tzg)


def _na_tz(rpb):
    col = np.arange(GRID_W)
    cs = np.clip(col - NA_COLS // 2, 0, GRID_W - NA_COLS)
    col_ok = (col[None, :] >= cs[:, None]) & (col[None, :] < cs[:, None] + NA_COLS)
    period = GRID_W + 1
    half = NA_COLS - 1
    filler = jnp.zeros(rpb.shape[:2] + (period - (2 * half + 1),), rpb.dtype)
    v = jnp.concatenate([rpb[..., half:], filler, rpb[..., :half]], axis=-1)
    t = jnp.tile(v, (1, 1, GRID_W))[..., :GRID_W * GRID_W].reshape(rpb.shape[:2] + (GRID_W, GRID_W))
    t = jnp.where(col_ok[None, None], t * LOG2E, NEG_INF)
    groups = []
    for g in range(NA_GROUPS):
        kc0 = _na_key_col0(g)
        assert col_ok[g * NA_GQ:(g + 1) * NA_GQ, :kc0].sum() == 0 and col_ok[g * NA_GQ:(g + 1) * NA_GQ, kc0 + NA_GK:].sum() == 0
        groups.append(t[:, :, g * NA_GQ:(g + 1) * NA_GQ, kc0:kc0 + NA_GK])
    return jnp.stack(groups, axis=2)


def _dil_kernel(q_ref, k_ref, v_ref, bias_ref, o_ref, m_scr, l_scr):
    order = sorted(range(len(DIL_CONFIGS)), key=lambda c: -DIL_CONFIGS[c][1])
    for c in order:
        dil = DIL_CONFIGS[c][1]
        m_len = SEQ // dil
        nblk = m_len // DIL_BQ

        def merge(qrows, o, m, l, c=c):
            if c == order[0]:
                o_ref[0, qrows, :] = o
                m_scr[qrows, :] = m
                l_scr[qrows, :] = l
                return
            m_old = m_scr[qrows, :]
            m_new = jnp.maximum(m_old, m)
            a_old = jnp.exp2(m_old - m_new)
            a_new = jnp.exp2(m - m_new)
            o_new = o_ref[0, qrows, :] * a_old + o * a_new
            l_new = l_scr[qrows, :] * a_old + l * a_new
            if c == order[-1]:
                o_ref[0, qrows, :] = o_new / l_new
            else:
                o_ref[0, qrows, :] = o_new
                l_scr[qrows, :] = l_new
                m_scr[qrows, :] = m_new

        def window(t, lo, pos, hi):
            return jnp.concatenate([t(lo, DIL_HALF), t(pos, DIL_BQ), t(hi, DIL_HALF)], axis=0)

        def block_body(it, carry, c=c, dil=dil, m_len=m_len, nblk=nblk):
            g = it // nblk
            n = it % nblk
            pos = n * DIL_BQ

            def rows(p0, size):
                return pl.ds(g + p0 * dil, size, stride=dil) if dil > 1 else pl.ds(p0, size)

            lo = jnp.maximum(pos - DIL_HALF, 0)
            hi = jnp.minimum(pos + DIL_BQ, m_len - DIL_HALF)
            k = window(lambda p0, size: k_ref[0, rows(p0, size), :], lo, pos, hi).astype(BF16)
            v = window(lambda p0, size: v_ref[0, rows(p0, size), :], lo, pos, hi).astype(BF16)
            edge_case = jnp.where(n == 0, 0, jnp.where(n == nblk - 1, 2, 1))
            qrows = rows(pos, DIL_BQ)
            merge(qrows, *_pair_attention(q_ref[0, qrows, :], k, v, bias_ref[c, 0, edge_case], stack_heads=True))
            return carry

        def class_body(g, carry, c=c, dil=dil, m_len=m_len, nblk=nblk):
            rows = pl.ds(g, m_len, stride=dil)
            q = q_ref[0, rows, :]
            kc = k_ref[0, rows, :].astype(BF16)
            vc = v_ref[0, rows, :].astype(BF16)
            res = []
            for n in range(nblk):
                pos = n * DIL_BQ
                lo, hi = max(pos - DIL_HALF, 0), min(pos + DIL_BQ, m_len - DIL_HALF)
                k = window(lambda p0, size: kc[p0:p0 + size], lo, pos, hi)
                v = window(lambda p0, size: vc[p0:p0 + size], lo, pos, hi)
                edge_case = 0 if n == 0 else 2 if n == nblk - 1 else 1
                res.append(_pair_attention(q[pos:pos + DIL_BQ], k, v, bias_ref[c, 0, edge_case], stack_heads=True))
            merge(rows, *(jnp.concatenate(parts, axis=0) for parts in zip(*res)))
            return carry

        if nblk <= DIL_UNROLL:
            lax.fori_loop(0, dil, class_body, 0, unroll=DIL_UNROLL // nblk)
        else:
            lax.fori_loop(0, SEQ // DIL_BQ, block_body, 0, unroll=DIL_UNROLL)


def _dil_attention(pc, bias):
    b = pc.shape[0]
    npair = DIL_WIDTH // LANES
    assert all(SEQ // dil // DIL_BQ >= 2 for _, dil in DIL_CONFIGS)
    return pl.pallas_call(
        _dil_kernel,
        grid=(b, npair),
        in_specs=[
            pl.BlockSpec((1, SEQ, LANES), lambda bi, p: (bi, 0, p)),
            pl.BlockSpec((1, SEQ, LANES), lambda bi, p: (bi, 0, npair + p)),
            pl.BlockSpec((1, SEQ, LANES), lambda bi, p: (bi, 0, 2 * npair + p)),
            pl.BlockSpec((len(DIL_CONFIGS), 1, 3, 2 * DIL_BQ, DIL_BK), lambda bi, p: (0, p, 0, 0, 0)),
        ],
        out_specs=pl.BlockSpec((1, SEQ, LANES), lambda bi, p: (bi, 0, p)),
        out_shape=jax.ShapeDtypeStruct((b, SEQ, DIL_WIDTH), F32),
        scratch_shapes=[pltpu.VMEM((SEQ, LANES), F32), pltpu.VMEM((SEQ, LANES), F32)],
        compiler_params=_cparams(("parallel", "arbitrary"), 56),
        name="dil_attn",
    )(pc, pc, pc, bias)


def _t5_bucket(rel):
    half = T5_BUCKETS // 2
    exact = half // 2
    n = np.abs(rel)
    far = exact + (np.log(np.maximum(n, 1) / exact) / math.log(T5_MAX_DIST / exact) * (half - exact)).astype(np.int32)
    far = np.minimum(far, half - 1)
    return (np.where(rel > 0, half, 0) + np.where(n < exact, n, far)).astype(np.int32)


def _dil_bias(t5_table):
    rel = np.arange(-DIL_HALF, DIL_HALF + 1)
    period = DIL_BK + 1
    nh = t5_table.shape[1]
    out = []
    for _, dil in DIL_CONFIGS:
        vals = t5_table[_t5_bucket(dil * rel)].T
        v = jnp.concatenate([vals, jnp.full((nh, period - vals.shape[1]), NEG_INF, vals.dtype)], axis=1)
        out.append(jnp.tile(v, (1, DIL_BQ))[:, :DIL_BQ * DIL_BK].reshape(nh, DIL_BQ, DIL_BK))
    t = (jnp.stack(out) * LOG2E).reshape(len(DIL_CONFIGS), nh // 2, 1, 2 * DIL_BQ, DIL_BK)
    kcol = np.arange(DIL_BK)
    edge = np.stack([kcol < DIL_HALF, np.zeros_like(kcol, bool), kcol >= DIL_HALF + DIL_BQ])
    return jnp.where(edge[None, None, :, None, :], NEG_INF, t)


CONV_TM = 1024


def _conv_gate_kernel(p_ref, prev_ref, next_ref, w_ref, b_ref, z_ref, x0_ref):
    i = pl.program_id(1)
    p = p_ref[0]
    tm = p.shape[0]
    first = jnp.where(i > 0, prev_ref[0, 7:8, :], 0.0)
    last = jnp.where(i < pl.num_programs(1) - 1, next_ref[0, 0:1, :], 0.0)
    row = lax.broadcasted_iota(jnp.int32, (tm, 1), 0)
    up = jnp.where(row == 0, first, pltpu.roll(p, 1, axis=0))
    dn = jnp.where(row == tm - 1, last, pltpu.roll(p, tm - 1, axis=0))
    uc = up * w_ref[0:1, :] + p * w_ref[1:2, :] + dn * w_ref[2:3, :] + b_ref[...]
    z_ref[0] = uc[:, HY_WIDTH:2 * HY_WIDTH] * uc[:, :HY_WIDTH]
    x0_ref[0] = uc[:, 2 * HY_WIDTH:]


def _conv_gate(pb, w, bvec):
    b = pb.shape[0]
    wb = 3 * HY_WIDTH
    nt = SEQ // CONV_TM
    r8 = CONV_TM // 8
    return pl.pallas_call(
        _conv_gate_kernel,
        grid=(b, nt),
        in_specs=[
            pl.BlockSpec((1, CONV_TM, wb), lambda bi, i: (bi, i, 0)),
            pl.BlockSpec((1, 8, wb), lambda bi, i: (bi, jnp.maximum(i * r8 - 1, 0), 0)),
            pl.BlockSpec((1, 8, wb), lambda bi, i: (bi, jnp.minimum((i + 1) * r8, SEQ // 8 - 1), 0)),
            pl.BlockSpec((3, wb), lambda bi, i: (0, 0)),
            pl.BlockSpec((1, wb), lambda bi, i: (0, 0)),
        ],
        out_specs=[
            pl.BlockSpec((1, CONV_TM, HY_WIDTH), lambda bi, i: (bi, i, 0)),
            pl.BlockSpec((1, CONV_TM, HY_WIDTH), lambda bi, i: (bi, i, 0)),
        ],
        out_shape=[jax.ShapeDtypeStruct((b, SEQ, HY_WIDTH), F32)] * 2,
        compiler_params=_cparams(("parallel", "parallel"), 48),
        name="hy_conv_gate",
    )(pb, pb, pb, w, bvec)


FILT_TM = 1024
FILT_KPAD = 128
FILT_HALF = SEQ // 2


def _filter_kernel(z_ref, w1_ref, b1_ref, w2_ref, b2_ref, w3_ref, b3_ref, w4_ref, fr_ref, dec_ref, h_ref):
    hp = lax.Precision.HIGHEST
    z = z_ref[...]
    fr = fr_ref[...]
    h = jnp.sin(fr * (jnp.dot(z, w1_ref[...], precision=hp, preferred_element_type=F32) + b1_ref[...]))
    h = jnp.sin(fr * (jnp.dot(h, w2_ref[...], precision=hp, preferred_element_type=F32) + b2_ref[...]))
    h = jnp.sin(fr * (jnp.dot(h, w3_ref[...], precision=hp, preferred_element_type=F32) + b3_ref[...]))
    h = jnp.dot(h.astype(BF16), w4_ref[...], preferred_element_type=F32)
    wout = 2 * HY_WIDTH
    dec = jnp.abs(dec_ref[...])
    lo = h[:, :wout] * jnp.exp(-z[:, 0:1] * dec)
    hi = h[:, wout:] * jnp.exp(-z[:, FILT_KPAD:FILT_KPAD + 1] * dec)
    row = pl.program_id(0) * FILT_TM + lax.broadcasted_iota(jnp.int32, (FILT_TM, 1), 0)
    col = lax.broadcasted_iota(jnp.int32, (1, wout), 1)
    h_ref[0] = jnp.where((row == 0) & (col >= HY_WIDTH), 0.0, lo)
    h_ref[1] = hi


def _filter_features():
    t = jnp.linspace(0.0, 1.0, SEQ, dtype=F32)[:, None]
    bands = jnp.linspace(1e-4, HY_BANDS - 1, HY_BANDS, dtype=F32)[None, :]
    ang = (2.0 * math.pi / SEQ) * jnp.arange(SEQ, dtype=F32)[:, None] * bands
    z = jnp.concatenate([t, jnp.cos(ang), -jnp.sin(ang)], axis=-1)
    z = jnp.pad(z, ((0, 0), (0, FILT_KPAD - z.shape[1])))
    return jnp.concatenate([z[:FILT_HALF], z[FILT_HALF:]], axis=1)


def _hyena_filters(zfeat, w1, b1, w2, b2, w3, b3, w4, freq, decay):
    def diag2(w):
        zero = jnp.zeros_like(w)
        return jnp.concatenate([jnp.concatenate([w, zero], axis=1), jnp.concatenate([zero, w], axis=1)], axis=0)

    twice = lambda v: jnp.concatenate([v, v])[None]
    w1p = jnp.pad(w1, ((0, FILT_KPAD - w1.shape[0]), (0, 0)))
    full = lambda a: pl.BlockSpec(a.shape, lambda i: (0,) * a.ndim)
    args = (diag2(w1p), twice(b1), diag2(w2), twice(b2), diag2(w3), twice(b3), diag2(w4).astype(BF16), twice(freq),
            decay[None])
    out = pl.pallas_call(
        _filter_kernel,
        grid=(FILT_HALF // FILT_TM,),
        in_specs=[pl.BlockSpec((FILT_TM, 2 * FILT_KPAD), lambda i: (i, 0))] + [full(a) for a in args],
        out_specs=pl.BlockSpec((2, FILT_TM, 2 * HY_WIDTH), lambda i: (0, i, 0)),
        out_shape=jax.ShapeDtypeStruct((2, FILT_HALF, 2 * HY_WIDTH), F32),
        compiler_params=_cparams(("parallel",), 32),
        name="hy_filter",
    )(zfeat, *args)
    return out.reshape(1, SEQ, 2 * HY_WIDTH)


def _dft_tables():
    a = np.arange(DFT_R)
    w = np.exp(-2j * np.pi * np.outer(a, a) / DFT_R)
    tw = np.exp(-2j * np.pi * np.outer(a, a) / DFT_N)
    f32 = lambda x: jnp.asarray(x.astype(np.float32))
    f1 = np.concatenate([w.real[:, :DFT_HALF], w.imag[:, :DFT_HALF]], axis=0)
    m1 = np.concatenate([w.real[:DFT_HALF], w.imag[:DFT_HALF]], axis=1) / DFT_N
    c = w[None] * tw[:, :DFT_KB].T[:, None, :]
    blocks = lambda a, b, c, d: np.concatenate(
        [np.concatenate([a, b], axis=2), np.concatenate([c, d], axis=2)], axis=1)
    qu = blocks(c.real, -c.imag, c.imag, c.real)
    ct = np.swapaxes(c, 1, 2)
    gu = blocks(ct.real, ct.imag, -ct.imag, ct.real)
    ta = np.exp(-2j * np.pi * DFT_KB * np.outer(np.arange(DFT_R // DFT_KB), a) / DFT_N)
    ta = np.concatenate([ta.real, ta.imag], axis=1)[:, :, None] * np.ones((1, 1, LANES))
    bf = lambda x: f32(x).astype(BF16)
    return bf(f1), bf(qu), bf(gu), bf(m1), f32(ta)


DFT_KB = 32
DFT_NB = 8


def _dft_stage1(x_ref, f1_ref, s_scr):
    def body(i, carry):
        n2b = i * DFT_NB
        xs = jnp.concatenate([x_ref[0, pl.ds(n2b + u, DFT_HALF, stride=DFT_R), :] for u in range(DFT_NB)],
                             axis=1).astype(BF16)
        a = jnp.dot(f1_ref[...], xs, preferred_element_type=F32)
        for u in range(DFT_NB):
            s_scr[pl.ds(pl.multiple_of((n2b + u) * DFT_PITCH, 8), 2 * DFT_R), :] = a[:, u * LANES:(u + 1) * LANES]
        return carry

    lax.fori_loop(0, DFT_R // DFT_NB, body, 0)


def _dft_stage2_rows(s_scr, ta_ref, step, u):
    k1 = step * DFT_KB + u
    re = s_scr[pl.ds(k1, DFT_R, stride=DFT_PITCH), :]
    im = s_scr[pl.ds(DFT_R + k1, DFT_R, stride=DFT_PITCH), :]
    tr, ti = ta_ref[step, :DFT_R, :], ta_ref[step, DFT_R:, :]
    return jnp.concatenate([re * tr - im * ti, re * ti + im * tr], axis=0).astype(BF16)


def _dft_fwd_kernel(x_ref, f1_ref, qu_ref, ta_ref, o_ref, s_scr):
    step = pl.program_id(2)
    pl.when(step == 0)(functools.partial(_dft_stage1, x_ref, f1_ref, s_scr))
    for u in range(DFT_KB):
        a = _dft_stage2_rows(s_scr, ta_ref, step, u)
        o_ref[0, u] = jnp.dot(qu_ref[u], a, preferred_element_type=F32).astype(o_ref.dtype)


def _dft_fwd(x, f1, qu, ta, out_dtype):
    nb, _, c = x.shape
    full = lambda a: pl.BlockSpec(a.shape, lambda b, ct, s: (0,) * a.ndim, pipeline_mode=pl.Buffered(1))
    return pl.pallas_call(
        _dft_fwd_kernel,
        grid=(nb, c // LANES, DFT_R // DFT_KB),
        in_specs=[pl.BlockSpec((1, SEQ, LANES), lambda b, ct, s: (b, 0, ct)), full(f1), full(qu), full(ta)],
        out_specs=pl.BlockSpec((1, DFT_KB, 2 * DFT_R, LANES), lambda b, ct, s: (b, s, 0, ct)),
        out_shape=jax.ShapeDtypeStruct((nb, DFT_R, 2 * DFT_R, c), out_dtype),
        scratch_shapes=[pltpu.VMEM((DFT_R * DFT_PITCH, LANES), F32)],
        compiler_params=_cparams(("parallel", "parallel", "arbitrary"), 40),
        name="hy_dft_fwd",
    )(x, f1, qu, ta)


def _dft_stage1_inv(s_scr, m1_ref, o_ref):
    def body(i, carry):
        n2b = i * DFT_NB
        bc = jnp.concatenate(
            [s_scr[pl.ds(pl.multiple_of((n2b + u) * DFT_PITCH, 8), 2 * DFT_R), :] for u in range(DFT_NB)],
            axis=1).astype(BF16)
        y = jnp.dot(m1_ref[...], bc, preferred_element_type=F32)
        for u in range(DFT_NB):
            o_ref[0, pl.ds(n2b + u, DFT_HALF, stride=DFT_R), :] = y[:, u * LANES:(u + 1) * LANES]
        return carry

    lax.fori_loop(0, DFT_R // DFT_NB, body, 0)


def _dft_conv_kernel(x_ref, f1_ref, qu_ref, ta_ref, hf_ref, hb_ref, gu_ref, m1_ref, o_ref, s_scr):
    step = pl.program_id(2)
    pl.when(step == 0)(functools.partial(_dft_stage1, x_ref, f1_ref, s_scr))
    tr, ti = ta_ref[step, :DFT_R, :], ta_ref[step, DFT_R:, :]
    for u in range(DFT_KB):
        k1 = step * DFT_KB + u
        xk = jnp.dot(qu_ref[u], _dft_stage2_rows(s_scr, ta_ref, step, u), preferred_element_type=F32)
        xr, xi = xk[:DFT_R], xk[DFT_R:]
        hr = hf_ref[0, u, :DFT_R, :].astype(F32) + hb_ref[0, u, :DFT_R, :].astype(F32)
        hi = hf_ref[0, u, DFT_R:, :].astype(F32) - hb_ref[0, u, DFT_R:, :].astype(F32)
        y = jnp.concatenate([xr * hr - xi * hi, xr * hi + xi * hr], axis=0).astype(BF16)
        b = jnp.dot(gu_ref[u], y, preferred_element_type=F32)
        br, bi = b[:DFT_R], b[DFT_R:]
        s_scr[pl.ds(k1, DFT_R, stride=DFT_PITCH), :] = br * tr + bi * ti
        s_scr[pl.ds(DFT_R + k1, DFT_R, stride=DFT_PITCH), :] = bi * tr - br * ti
    pl.when(step == pl.num_programs(2) - 1)(functools.partial(_dft_stage1_inv, s_scr, m1_ref, o_ref))


def _dft_conv(x, hs, f1, qu, gu, m1, ta):
    b, _, c = x.shape
    nct = c // LANES
    full = lambda a: pl.BlockSpec(a.shape, lambda bi, ct, s: (0,) * a.ndim, pipeline_mode=pl.Buffered(1))
    return pl.pallas_call(
        _dft_conv_kernel,
        grid=(b, nct, DFT_R // DFT_KB),
        in_specs=[
            pl.BlockSpec((1, SEQ, LANES), lambda bi, ct, s: (bi, 0, ct)),
            full(f1), full(qu), full(ta),
            pl.BlockSpec((1, DFT_KB, 2 * DFT_R, LANES), lambda bi, ct, s: (0, s, 0, ct)),
            pl.BlockSpec((1, DFT_KB, 2 * DFT_R, LANES), lambda bi, ct, s: (0, s, 0, nct + ct)),
            full(gu), full(m1),
        ],
        out_specs=pl.BlockSpec((1, SEQ, LANES), lambda bi, ct, s: (bi, 0, ct)),
        out_shape=jax.ShapeDtypeStruct((b, SEQ, c), F32),
        scratch_shapes=[pltpu.VMEM((DFT_R * DFT_PITCH, LANES), F32)],
        compiler_params=_cparams(("parallel", "parallel", "arbitrary"), 48),
        name="hy_dft_conv",
    )(x, f1, qu, ta, hs, hs, gu, m1)


def _mix_out_body(x, ya, yh, z, x0, yc, skip_ref, gg_ref, w_ref, gpost_ref):
    a0, b0 = NA_WIDTH, NA_WIDTH + HY_WIDTH
    yb = x0 * (yh + skip_ref[...] * z)
    acc = jnp.dot(_rms(ya, gg_ref[:, :a0]).astype(BF16), w_ref[:a0, :], preferred_element_type=F32)
    acc += jnp.dot(_rms(yb, gg_ref[:, a0:b0]).astype(BF16), w_ref[a0:b0, :], preferred_element_type=F32)
    acc += jnp.dot(_rms(yc, gg_ref[:, b0:]).astype(BF16), w_ref[b0:, :], preferred_element_type=F32)
    return x + _rms(acc, gpost_ref[...])


def _xattn_body(x, gpre_ref, wq_ref, kv_ref, wo_ref, gpost_ref):
    h = _rms(x, gpre_ref[...]).astype(BF16)
    q = jnp.dot(h, wq_ref[...], preferred_element_type=F32) * (XA_HEAD_DIM ** -0.5)
    outs = []
    for hd in range(XA_HEADS):
        c0 = hd * XA_HEAD_DIM
        qh = q[:, c0:c0 + XA_HEAD_DIM].astype(BF16)
        kh = kv_ref[0, :, c0:c0 + XA_HEAD_DIM]
        vh = kv_ref[0, :, D_MODEL + c0:D_MODEL + c0 + XA_HEAD_DIM]
        s = lax.dot_general(qh, kh, (((1,), (1,)), ((), ())), preferred_element_type=F32)
        p = jnp.exp(s - jnp.max(s, axis=-1, keepdims=True))
        l = jnp.sum(p, axis=-1, keepdims=True)
        outs.append(jnp.dot(p.astype(BF16), vh, preferred_element_type=F32) / l)
    o = jnp.concatenate(outs, axis=-1).astype(BF16)
    xa = jnp.dot(o, wo_ref[...], preferred_element_type=F32)
    return x + _rms(xa, gpost_ref[...])


POST_TM = 512


def _post_mixer_kernel(x_ref, ya_ref, yh_ref, z_ref, x0_ref, yc_ref, skip_ref, gg_ref, wout_ref, gmix_ref,
                       gxpre_ref, wq_ref, kv_ref, wo_ref, gxpost_ref, gfpre_ref, fwin_ref, fwout_ref, gfpost_ref,
                       o_ref):
    x = _mix_out_body(x_ref[...], ya_ref[...], yh_ref[...], z_ref[...], x0_ref[...], yc_ref[...],
                      skip_ref, gg_ref, wout_ref, gmix_ref)
    x = _xattn_body(x, gxpre_ref, wq_ref, kv_ref, wo_ref, gxpost_ref)
    o_ref[...] = _ffn_body(x, gfpre_ref, fwin_ref, fwout_ref, gfpost_ref)


def _post_mixer(x, layer, ya, yh, z, x0, yc, skip, gg, w_out, g_mix, g_xpre, wq, kv, wo, g_xpost, g_fpre, fw_in,
                fw_out, g_fpost):
    n = x.shape[0]
    per_batch = SEQ // POST_TM
    tok = lambda width: pl.BlockSpec((POST_TM, width), lambda i: (i, 0))
    const = _resident
    kv_spec = pl.BlockSpec((1,) + kv.shape[1:], lambda i: (i // per_batch, 0, 0))
    return pl.pallas_call(
        _post_mixer_kernel,
        grid=(n // POST_TM,),
        in_specs=[tok(D_MODEL), tok(NA_WIDTH), tok(HY_WIDTH), tok(HY_WIDTH), tok(HY_WIDTH), tok(DIL_WIDTH),
                  const(skip), const(gg), const(w_out, (layer,)), const(g_mix),
                  const(g_xpre), const(wq, (layer,)), kv_spec, const(wo, (layer,)), const(g_xpost),
                  const(g_fpre), const(fw_in, (layer, 1)), const(fw_out, (layer, 1)), const(g_fpost)],
        out_specs=tok(D_MODEL),
        out_shape=jax.ShapeDtypeStruct((n, D_MODEL), F32),
        compiler_params=_cparams(("parallel",), 56),
        name="post_mixer",
    )(x, ya, yh, z, x0, yc, skip, gg, w_out, g_mix, g_xpre, wq, kv, wo, g_xpost, g_fpre, fw_in, fw_out, g_fpost)


def kernel(x, mem, norm_g, w_in, w_out, na_rpb, t5_table, hy_conv_w, hy_conv_b, hy_f_w1, hy_f_b1, hy_f_w2, hy_f_b2, hy_f_w3, hy_f_b3, hy_f_w4, hy_freq, hy_decay, hy_skip, xa_wq, xa_wkv, xa_wo, ffn_w_in, ffn_w_out):
    bsz, seq, d = x.shape
    assert (seq, d) == (SEQ, D_MODEL)
    depth = norm_g.shape[0]
    mem_len = mem.shape[1]
    n = bsz * seq
    scale = HEAD_DIM ** -0.5 * LOG2E
    wa, wb, wc = 3 * NA_WIDTH, 3 * HY_WIDTH, 3 * DIL_WIDTH
    colscale = np.ones((1, wa + wb + wc), np.float32)
    colscale[:, :NA_WIDTH] = scale
    colscale[:, wa + wb:wa + wb + DIL_WIDTH] = scale
    colscale = jnp.asarray(colscale)

    zfeat = _filter_features()
    f1, qu, gu, m1, ta = _dft_tables()
    dil_bias = _dil_bias(t5_table)

    xf = x.reshape(n, d)
    memf = mem.reshape(bsz * mem_len, d)
    ffn_w_in, ffn_w_out, w_in, w_out, xa_wq, xa_wkv, xa_wo = (
        w.astype(BF16) for w in (ffn_w_in, ffn_w_out, w_in, w_out, xa_wq, xa_wkv, xa_wo))
    for l in range(depth):
        g = norm_g[l][:, None, :]
        xf, pa, pb, pc = _pre_mixer(xf, l, g[0], ffn_w_in, ffn_w_out, g[1], g[2], w_in, colscale)
        pa = pa.reshape(bsz, seq, wa)
        pb = pb.reshape(bsz, seq, wb)
        pc = pc.reshape(bsz, seq, wc)
        ya = _na_attention(pa, _na_tz(na_rpb[l]))
        z, x0 = _conv_gate(pb, hy_conv_w[l], hy_conv_b[l][None])
        hfilt = _hyena_filters(zfeat, hy_f_w1[l], hy_f_b1[l], hy_f_w2[l], hy_f_b2[l], hy_f_w3[l], hy_f_b3[l],
                               hy_f_w4[l], hy_freq[l], hy_decay[l])
        yh = _dft_conv(z, _dft_fwd(hfilt, f1, qu, ta, BF16), f1, qu, gu, m1, ta)
        yc = _dil_attention(pc, dil_bias)
        kv = _norm_mm(memf, g[6], xa_wkv, l, mem_len, BF16).reshape(bsz, mem_len, 2 * D_MODEL)
        xf = _post_mixer(xf, l, ya.reshape(n, NA_WIDTH), yh.reshape(n, HY_WIDTH), z.reshape(n, HY_WIDTH),
                         x0.reshape(n, HY_WIDTH), yc.reshape(n, DIL_WIDTH), hy_skip[l][None], g[3],
                         w_out, g[4], g[5], xa_wq, kv, xa_wo, g[7], g[8], ffn_w_in, ffn_w_out, g[9])
    return xf.reshape(bsz, seq, d)
```

```python
import functools
import math

import numpy as np
import jax
import jax.numpy as jnp
from jax import lax
from jax.experimental import pallas as pl
from jax.experimental.pallas import tpu as pltpu

F32 = jnp.float32
BF16 = jnp.bfloat16

D_MODEL = 1024
SEQ = 8192
GRID_W = 64
HEAD_DIM = 64
NA_WIDTH = 384
HY_WIDTH = 256
DIL_WIDTH = 384
NA_ROWS = 8
NA_COLS = 16
HY_BANDS = 16
DIL_CONFIGS = ((128, 1), (512, 4), (2048, 16))
T5_BUCKETS = 32
T5_MAX_DIST = 1024
XA_HEADS = 4
XA_HEAD_DIM = 256
FFN_HIDDEN = 2816
RMS_EPS = 1e-6
NEG_INF = -1e30
LOG2E = math.log2(math.e)

LANES = 128
MIB = 1024 * 1024

NA_QROWS = 8
NA_KROWS = 16
NA_BQ = NA_QROWS * GRID_W
NA_GROUPS = 4
NA_GQ = GRID_W // NA_GROUPS
NA_GK = 2 * NA_GQ
NA_STEP_BLOCKS = 8
DIL_HALF = 64
DIL_BQ = 128
DIL_UNROLL = 32
DIL_BK = DIL_BQ + 2 * DIL_HALF
DFT_R = 128
DFT_HALF = 64
DFT_N = DFT_R * DFT_R
DFT_PITCH = 264


def _cparams(sem, vmem_mib):
    return pltpu.CompilerParams(dimension_semantics=sem, vmem_limit_bytes=vmem_mib * MIB)


def _resident(a, lead=()):
    k = len(lead)
    return pl.BlockSpec((None,) * k + a.shape[k:], lambda i: tuple(lead) + (0,) * (a.ndim - k),
                        pipeline_mode=pl.Buffered(1))


def _rms(x, g):
    return x * lax.rsqrt(jnp.mean(x * x, axis=-1, keepdims=True) + RMS_EPS) * g


FFN_TM = 512
FFN_TH = 256


def _ffn_body(x, gpre_ref, win_ref, wout_ref, gpost_ref):
    h = _rms(x, gpre_ref[...]).astype(BF16)
    acc = jnp.zeros(x.shape, F32)
    for j in range(FFN_HIDDEN // FFN_TH):
        c0 = j * FFN_TH
        gate = jnp.dot(h, win_ref[:, c0:c0 + FFN_TH], preferred_element_type=F32)
        up = jnp.dot(h, win_ref[:, FFN_HIDDEN + c0:FFN_HIDDEN + c0 + FFN_TH], preferred_element_type=F32)
        act = (gate * jax.nn.sigmoid(gate) * up).astype(BF16)
        acc += jnp.dot(act, wout_ref[c0:c0 + FFN_TH, :], preferred_element_type=F32)
    return x + 0.5 * _rms(acc, gpost_ref[...])


def _pre_mixer_kernel(x_ref, gfpre_ref, fwin_ref, fwout_ref, gfpost_ref, gmix_ref, w_ref, cs_ref,
                      o_ref, pa_ref, pb_ref, pc_ref):
    x = _ffn_body(x_ref[...], gfpre_ref, fwin_ref, fwout_ref, gfpost_ref)
    o_ref[...] = x
    h = _rms(x, gmix_ref[...]).astype(BF16)
    c0 = 0
    for ref in (pa_ref, pb_ref, pc_ref):
        c1 = c0 + ref.shape[1]
        ref[...] = jnp.dot(h, w_ref[:, c0:c1], preferred_element_type=F32) * cs_ref[:, c0:c1]
        c0 = c1


def _pre_mixer(x, layer, g_fpre, fw_in, fw_out, g_fpost, g_mix, w, colscale):
    n = x.shape[0]
    widths = (3 * NA_WIDTH, 3 * HY_WIDTH, 3 * DIL_WIDTH)
    tok = lambda width: pl.BlockSpec((FFN_TM, width), lambda i: (i, 0))
    const = _resident
    return pl.pallas_call(
        _pre_mixer_kernel,
        grid=(n // FFN_TM,),
        in_specs=[tok(D_MODEL), const(g_fpre), const(fw_in, (layer, 0)), const(fw_out, (layer, 0)), const(g_fpost),
                  const(g_mix), const(w, (layer,)), const(colscale)],
        out_specs=[tok(D_MODEL)] + [tok(wd) for wd in widths],
        out_shape=[jax.ShapeDtypeStruct((n, wd), F32) for wd in (D_MODEL,) + widths],
        compiler_params=_cparams(("parallel",), 56),
        name="pre_mixer",
    )(x, g_fpre, fw_in, fw_out, g_fpost, g_mix, w, colscale)


def _norm_mm_kernel(x_ref, g_ref, w_ref, o_ref):
    h = _rms(x_ref[...], g_ref[...]).astype(BF16)
    o_ref[...] = jnp.dot(h, w_ref[...], preferred_element_type=F32).astype(o_ref.dtype)


def _norm_mm(x, g, w, layer, tm, out_dtype):
    n, ncol = x.shape[0], w.shape[2]
    return pl.pallas_call(
        _norm_mm_kernel,
        grid=(n // tm,),
        in_specs=[pl.BlockSpec((tm, D_MODEL), lambda i: (i, 0)), _resident(g), _resident(w, (layer,))],
        out_specs=pl.BlockSpec((tm, ncol), lambda i: (i, 0)),
        out_shape=jax.ShapeDtypeStruct((n, ncol), out_dtype),
        compiler_params=_cparams(("parallel",), 32),
        name="norm_mm",
    )(x, g, w)


def _softmax_pv(q, k, v, bias):
    s = lax.dot_general(q, k, (((1,), (1,)), ((), ())), preferred_element_type=F32) + bias
    m = jnp.max(s, axis=-1, keepdims=True)
    p = jnp.exp2(s - m).astype(BF16)
    o = jnp.dot(p, jnp.concatenate([v, jnp.ones_like(v)], axis=1), preferred_element_type=F32)
    return o[:, :LANES], m, o[:, LANES:]


def _pair_attention(q, k, v, bias, stack_heads):
    bq = q.shape[0]
    lane = lax.broadcasted_iota(jnp.int32, (1, LANES), 1)
    first = lane < HEAD_DIM
    q0 = jnp.where(first, q, 0.0).astype(BF16)
    q1 = jnp.where(first, 0.0, q).astype(BF16)
    if stack_heads:
        res = _softmax_pv(jnp.concatenate([q0, q1], axis=0), k, v, bias)
        return tuple(jnp.where(first, a[:bq], a[bq:]) for a in res)
    res0 = _softmax_pv(q0, k, v, bias[:bq])
    res1 = _softmax_pv(q1, k, v, bias[bq:])
    return tuple(jnp.where(first, a, b) for a, b in zip(res0, res1))


def _na_key_col0(g):
    return int(np.clip(g * NA_GQ - NA_COLS // 2, 0, GRID_W - NA_GK))


def _na_fill_bias(tz_ref, bias_scr, j):
    nrows = SEQ // GRID_W
    start_row = int(np.clip(j * NA_QROWS - NA_ROWS // 2, 0, nrows - NA_KROWS))
    neg = jnp.full((NA_GQ, NA_GK), NEG_INF, F32)
    per_tile = LANES // NA_GK
    for g in range(NA_GROUPS):
        for hh in range(2):
            for rl in range(NA_QROWS):
                r = j * NA_QROWS + rl
                rs = int(np.clip(r - NA_ROWS // 2, 0, nrows - NA_ROWS))
                r0 = (hh * NA_QROWS + rl) * NA_GQ
                for kt in range(NA_KROWS // per_tile):
                    tiles = []
                    for kl in range(kt * per_tile, (kt + 1) * per_tile):
                        kr = start_row + kl
                        tiles.append(tz_ref[hh, kr - r + NA_ROWS - 1, g] if rs <= kr < rs + NA_ROWS else neg)
                    bias_scr[g, r0:r0 + NA_GQ, kt * LANES:(kt + 1) * LANES] = jnp.concatenate(tiles, axis=1)


def _na_row_block(q_ref, k_ref, v_ref, o_ref, bias_scr, j, q0):
    nrows = SEQ // GRID_W
    nblk = SEQ // NA_BQ
    table = jnp.where(j == 0, 0, jnp.where(j == nblk - 1, 2, 1))
    start_row = jnp.clip(j * NA_QROWS - NA_ROWS // 2, 0, nrows - NA_KROWS)
    start = start_row * GRID_W
    for g in range(NA_GROUPS):
        kc0 = _na_key_col0(g)
        krows = [pl.ds(pl.multiple_of(start + kl * GRID_W + kc0, 8), NA_GK) for kl in range(NA_KROWS)]
        qrows = [slice(q0 + rl * GRID_W + g * NA_GQ, q0 + rl * GRID_W + (g + 1) * NA_GQ) for rl in range(NA_QROWS)]
        k = jnp.concatenate([k_ref[0, r, :] for r in krows], axis=0).astype(BF16)
        v = jnp.concatenate([v_ref[0, r, :] for r in krows], axis=0).astype(BF16)
        q = jnp.concatenate([q_ref[0, r, :] for r in qrows], axis=0)
        o, _, l = _pair_attention(q, k, v, bias_scr[table, g], stack_heads=True)
        o = o / l
        for rl, r in enumerate(qrows):
            o_ref[0, r, :] = o[rl * NA_GQ:(rl + 1) * NA_GQ]


def _na_kernel(q_ref, k_ref, v_ref, tz_ref, o_ref, bias_scr):
    nblk = SEQ // NA_BQ

    @pl.when((pl.program_id(1) == 0) & (pl.program_id(2) == 0))
    def _():
        for table, j in enumerate((0, 1, nblk - 1)):
            _na_fill_bias(tz_ref, bias_scr.at[table], j)

    step = pl.program_id(2)
    for t in range(NA_STEP_BLOCKS):
        _na_row_block(q_ref, k_ref, v_ref, o_ref, bias_scr, NA_STEP_BLOCKS * step + t, t * NA_BQ)


def _na_attention(pa, tzg):
    b = pa.shape[0]
    npair = NA_WIDTH // LANES
    nstep = SEQ // (NA_STEP_BLOCKS * NA_BQ)
    return pl.pallas_call(
        _na_kernel,
        grid=(npair, b, nstep),
        in_specs=[
            pl.BlockSpec((1, NA_STEP_BLOCKS * NA_BQ, LANES), lambda p, bi, j: (bi, j, p)),
            pl.BlockSpec((1, SEQ, LANES), lambda p, bi, j: (bi, 0, npair + p)),
            pl.BlockSpec((1, SEQ, LANES), lambda p, bi, j: (bi, 0, 2 * npair + p)),
            pl.BlockSpec((2,) + tzg.shape[1:], lambda p, bi, j: (p, 0, 0, 0, 0)),
        ],
        out_specs=pl.BlockSpec((1, NA_STEP_BLOCKS * NA_BQ, LANES), lambda p, bi, j: (bi, j, p)),
        out_shape=jax.ShapeDtypeStruct((b, SEQ, NA_WIDTH), F32),
        scratch_shapes=[pltpu.VMEM((3, NA_GROUPS, 2 * NA_QROWS * NA_GQ, NA_KROWS * NA_GK), F32)],
        compiler_params=_cparams(("parallel", "arbitrary", "arbitrary"), 48),
        name="na_attn",
    )(pa, pa, pa, tzg)


def _na_tz(rpb):
    col = np.arange(GRID_W)
    cs = np.clip(col - NA_COLS // 2, 0, GRID_W - NA_COLS)
    col_ok = (col[None, :] >= cs[:, None]) & (col[None, :] < cs[:, None] + NA_COLS)
    period = GRID_W + 1
    half = NA_COLS - 1
    filler = jnp.zeros(rpb.shape[:2] + (period - (2 * half + 1),), rpb.dtype)
    v = jnp.concatenate([rpb[..., half:], filler, rpb[..., :half]], axis=-1)
    t = jnp.tile(v, (1, 1, GRID_W))[..., :GRID_W * GRID_W].reshape(rpb.shape[:2] + (GRID_W, GRID_W))
    t = jnp.where(col_ok[None, None], t * LOG2E, NEG_INF)
    groups = []
    for g in range(NA_GROUPS):
        kc0 = _na_key_col0(g)
        assert col_ok[g * NA_GQ:(g + 1) * NA_GQ, :kc0].sum() == 0 and col_ok[g * NA_GQ:(g + 1) * NA_GQ, kc0 + NA_GK:].sum() == 0
        groups.append(t[:, :, g * NA_GQ:(g + 1) * NA_GQ, kc0:kc0 + NA_GK])
    return jnp.stack(groups, axis=2)


def _dil_kernel(q_ref, k_ref, v_ref, bias_ref, o_ref, m_scr, l_scr):
    order = sorted(range(len(DIL_CONFIGS)), key=lambda c: -DIL_CONFIGS[c][1])
    for c in order:
        dil = DIL_CONFIGS[c][1]
        m_len = SEQ // dil
        nblk = m_len // DIL_BQ

        def merge(qrows, o, m, l, c=c):
            if c == order[0]:
                o_ref[0, qrows, :] = o
                m_scr[qrows, :] = m
                l_scr[qrows, :] = l
                return
            m_old = m_scr[qrows, :]
            m_new = jnp.maximum(m_old, m)
            a_old = jnp.exp2(m_old - m_new)
            a_new = jnp.exp2(m - m_new)
            o_new = o_ref[0, qrows, :] * a_old + o * a_new
            l_new = l_scr[qrows, :] * a_old + l * a_new
            if c == order[-1]:
                o_ref[0, qrows, :] = o_new / l_new
            else:
                o_ref[0, qrows, :] = o_new
                l_scr[qrows, :] = l_new
                m_scr[qrows, :] = m_new

        def window(t, lo, pos, hi):
            return jnp.concatenate([t(lo, DIL_HALF), t(pos, DIL_BQ), t(hi, DIL_HALF)], axis=0)

        def block_body(it, carry, c=c, dil=dil, m_len=m_len, nblk=nblk):
            g = it // nblk
            n = it % nblk
            pos = n * DIL_BQ

            def rows(p0, size):
                return pl.ds(g + p0 * dil, size, stride=dil) if dil > 1 else pl.ds(p0, size)

            lo = jnp.maximum(pos - DIL_HALF, 0)
            hi = jnp.minimum(pos + DIL_BQ, m_len - DIL_HALF)
            k = window(lambda p0, size: k_ref[0, rows(p0, size), :], lo, pos, hi).astype(BF16)
            v = window(lambda p0, size: v_ref[0, rows(p0, size), :], lo, pos, hi).astype(BF16)
            edge_case = jnp.where(n == 0, 0, jnp.where(n == nblk - 1, 2, 1))
            qrows = rows(pos, DIL_BQ)
            merge(qrows, *_pair_attention(q_ref[0, qrows, :], k, v, bias_ref[c, 0, edge_case], stack_heads=True))
            return carry

        def class_body(g, carry, c=c, dil=dil, m_len=m_len, nblk=nblk):
            rows = pl.ds(g, m_len, stride=dil)
            q = q_ref[0, rows, :]
            kc = k_ref[0, rows, :].astype(BF16)
            vc = v_ref[0, rows, :].astype(BF16)
            res = []
            for n in range(nblk):
                pos = n * DIL_BQ
                lo, hi = max(pos - DIL_HALF, 0), min(pos + DIL_BQ, m_len - DIL_HALF)
                k = window(lambda p0, size: kc[p0:p0 + size], lo, pos, hi)
                v = window(lambda p0, size: vc[p0:p0 + size], lo, pos, hi)
                edge_case = 0 if n == 0 else 2 if n == nblk - 1 else 1
                res.append(_pair_attention(q[pos:pos + DIL_BQ], k, v, bias_ref[c, 0, edge_case], stack_heads=True))
            merge(rows, *(jnp.concatenate(parts, axis=0) for parts in zip(*res)))
            return carry

        if nblk <= DIL_UNROLL:
            lax.fori_loop(0, dil, class_body, 0, unroll=DIL_UNROLL // nblk)
        else:
            lax.fori_loop(0, SEQ // DIL_BQ, block_body, 0, unroll=DIL_UNROLL)


def _dil_attention(pc, bias):
    b = pc.shape[0]
    npair = DIL_WIDTH // LANES
    assert all(SEQ // dil // DIL_BQ >= 2 for _, dil in DIL_CONFIGS)
    return pl.pallas_call(
        _dil_kernel,
        grid=(b, npair),
        in_specs=[
            pl.BlockSpec((1, SEQ, LANES), lambda bi, p: (bi, 0, p)),
            pl.BlockSpec((1, SEQ, LANES), lambda bi, p: (bi, 0, npair + p)),
            pl.BlockSpec((1, SEQ, LANES), lambda bi, p: (bi, 0, 2 * npair + p)),
            pl.BlockSpec((len(DIL_CONFIGS), 1, 3, 2 * DIL_BQ, DIL_BK), lambda bi, p: (0, p, 0, 0, 0)),
        ],
        out_specs=pl.BlockSpec((1, SEQ, LANES), lambda bi, p: (bi, 0, p)),
        out_shape=jax.ShapeDtypeStruct((b, SEQ, DIL_WIDTH), F32),
        scratch_shapes=[pltpu.VMEM((SEQ, LANES), F32), pltpu.VMEM((SEQ, LANES), F32)],
        compiler_params=_cparams(("parallel", "arbitrary"), 56),
        name="dil_attn",
    )(pc, pc, pc, bias)


def _t5_bucket(rel):
    half = T5_BUCKETS // 2
    exact = half // 2
    n = np.abs(rel)
    far = exact + (np.log(np.maximum(n, 1) / exact) / math.log(T5_MAX_DIST / exact) * (half - exact)).astype(np.int32)
    far = np.minimum(far, half - 1)
    return (np.where(rel > 0, half, 0) + np.where(n < exact, n, far)).astype(np.int32)


def _dil_bias(t5_table):
    rel = np.arange(-DIL_HALF, DIL_HALF + 1)
    period = DIL_BK + 1
    nh = t5_table.shape[1]
    out = []
    for _, dil in DIL_CONFIGS:
        vals = t5_table[_t5_bucket(dil * rel)].T
        v = jnp.concatenate([vals, jnp.full((nh, period - vals.shape[1]), NEG_INF, vals.dtype)], axis=1)
        out.append(jnp.tile(v, (1, DIL_BQ))[:, :DIL_BQ * DIL_BK].reshape(nh, DIL_BQ, DIL_BK))
    t = (jnp.stack(out) * LOG2E).reshape(len(DIL_CONFIGS), nh // 2, 1, 2 * DIL_BQ, DIL_BK)
    kcol = np.arange(DIL_BK)
    edge = np.stack([kcol < DIL_HALF, np.zeros_like(kcol, bool), kcol >= DIL_HALF + DIL_BQ])
    return jnp.where(edge[None, None, :, None, :], NEG_INF, t)


CONV_TM = 1024


def _conv_gate_kernel(p_ref, prev_ref, next_ref, w_ref, b_ref, z_ref, x0_ref):
    i = pl.program_id(1)
    p = p_ref[0]
    tm = p.shape[0]
    first = jnp.where(i > 0, prev_ref[0, 7:8, :], 0.0)
    last = jnp.where(i < pl.num_programs(1) - 1, next_ref[0, 0:1, :], 0.0)
    row = lax.broadcasted_iota(jnp.int32, (tm, 1), 0)
    up = jnp.where(row == 0, first, pltpu.roll(p, 1, axis=0))
    dn = jnp.where(row == tm - 1, last, pltpu.roll(p, tm - 1, axis=0))
    uc = up * w_ref[0:1, :] + p * w_ref[1:2, :] + dn * w_ref[2:3, :] + b_ref[...]
    z_ref[0] = uc[:, HY_WIDTH:2 * HY_WIDTH] * uc[:, :HY_WIDTH]
    x0_ref[0] = uc[:, 2 * HY_WIDTH:]


def _conv_gate(pb, w, bvec):
    b = pb.shape[0]
    wb = 3 * HY_WIDTH
    nt = SEQ // CONV_TM
    r8 = CONV_TM // 8
    return pl.pallas_call(
        _conv_gate_kernel,
        grid=(b, nt),
        in_specs=[
            pl.BlockSpec((1, CONV_TM, wb), lambda bi, i: (bi, i, 0)),
            pl.BlockSpec((1, 8, wb), lambda bi, i: (bi, jnp.maximum(i * r8 - 1, 0), 0)),
            pl.BlockSpec((1, 8, wb), lambda bi, i: (bi, jnp.minimum((i + 1) * r8, SEQ // 8 - 1), 0)),
            pl.BlockSpec((3, wb), lambda bi, i: (0, 0)),
            pl.BlockSpec((1, wb), lambda bi, i: (0, 0)),
        ],
        out_specs=[
            pl.BlockSpec((1, CONV_TM, HY_WIDTH), lambda bi, i: (bi, i, 0)),
            pl.BlockSpec((1, CONV_TM, HY_WIDTH), lambda bi, i: (bi, i, 0)),
        ],
        out_shape=[jax.ShapeDtypeStruct((b, SEQ, HY_WIDTH), F32)] * 2,
        compiler_params=_cparams(("parallel", "parallel"), 48),
        name="hy_conv_gate",
    )(pb, pb, pb, w, bvec)


FILT_TM = 1024
FILT_KPAD = 128
FILT_HALF = SEQ // 2


def _filter_kernel(z_ref, w1_ref, b1_ref, w2_ref, b2_ref, w3_ref, b3_ref, w4_ref, fr_ref, dec_ref, h_ref):
    hp = lax.Precision.HIGHEST
    z = z_ref[...]
    fr = fr_ref[...]
    h = jnp.sin(fr * (jnp.dot(z, w1_ref[...], precision=hp, preferred_element_type=F32) + b1_ref[...]))
    h = jnp.sin(fr * (jnp.dot(h, w2_ref[...], precision=hp, preferred_element_type=F32) + b2_ref[...]))
    h = jnp.sin(fr * (jnp.dot(h, w3_ref[...], precision=hp, preferred_element_type=F32) + b3_ref[...]))
    h = jnp.dot(h.astype(BF16), w4_ref[...], preferred_element_type=F32)
    wout = 2 * HY_WIDTH
    dec = jnp.abs(dec_ref[...])
    lo = h[:, :wout] * jnp.exp(-z[:, 0:1] * dec)
    hi = h[:, wout:] * jnp.exp(-z[:, FILT_KPAD:FILT_KPAD + 1] * dec)
    row = pl.program_id(0) * FILT_TM + lax.broadcasted_iota(jnp.int32, (FILT_TM, 1), 0)
    col = lax.broadcasted_iota(jnp.int32, (1, wout), 1)
    h_ref[0] = jnp.where((row == 0) & (col >= HY_WIDTH), 0.0, lo)
    h_ref[1] = hi


def _filter_features():
    t = jnp.linspace(0.0, 1.0, SEQ, dtype=F32)[:, None]
    bands = jnp.linspace(1e-4, HY_BANDS - 1, HY_BANDS, dtype=F32)[None, :]
    ang = (2.0 * math.pi / SEQ) * jnp.arange(SEQ, dtype=F32)[:, None] * bands
    z = jnp.concatenate([t, jnp.cos(ang), -jnp.sin(ang)], axis=-1)
    z = jnp.pad(z, ((0, 0), (0, FILT_KPAD - z.shape[1])))
    return jnp.concatenate([z[:FILT_HALF], z[FILT_HALF:]], axis=1)


def _hyena_filters(zfeat, w1, b1, w2, b2, w3, b3, w4, freq, decay):
    def diag2(w):
        zero = jnp.zeros_like(w)
        return jnp.concatenate([jnp.concatenate([w, zero], axis=1), jnp.concatenate([zero, w], axis=1)], axis=0)

    twice = lambda v: jnp.concatenate([v, v])[None]
    w1p = jnp.pad(w1, ((0, FILT_KPAD - w1.shape[0]), (0, 0)))
    full = lambda a: pl.BlockSpec(a.shape, lambda i: (0,) * a.ndim)
    args = (diag2(w1p), twice(b1), diag2(w2), twice(b2), diag2(w3), twice(b3), diag2(w4).astype(BF16), twice(freq),
            decay[None])
    out = pl.pallas_call(
        _filter_kernel,
        grid=(FILT_HALF // FILT_TM,),
        in_specs=[pl.BlockSpec((FILT_TM, 2 * FILT_KPAD), lambda i: (i, 0))] + [full(a) for a in args],
        out_specs=pl.BlockSpec((2, FILT_TM, 2 * HY_WIDTH), lambda i: (0, i, 0)),
        out_shape=jax.ShapeDtypeStruct((2, FILT_HALF, 2 * HY_WIDTH), F32),
        compiler_params=_cparams(("parallel",), 32),
        name="hy_filter",
    )(zfeat, *args)
    return out.reshape(1, SEQ, 2 * HY_WIDTH)


def _dft_tables():
    a = np.arange(DFT_R)
    w = np.exp(-2j * np.pi * np.outer(a, a) / DFT_R)
    tw = np.exp(-2j * np.pi * np.outer(a, a) / DFT_N)
    f32 = lambda x: jnp.asarray(x.astype(np.float32))
    f1 = np.concatenate([w.real[:, :DFT_HALF], w.imag[:, :DFT_HALF]], axis=0)
    m1 = np.concatenate([w.real[:DFT_HALF], w.imag[:DFT_HALF]], axis=1) / DFT_N
    c = w[None] * tw[:, :DFT_KB].T[:, None, :]
    blocks = lambda a, b, c, d: np.concatenate(
        [np.concatenate([a, b], axis=2), np.concatenate([c, d], axis=2)], axis=1)
    qu = blocks(c.real, -c.imag, c.imag, c.real)
    ct = np.swapaxes(c, 1, 2)
    gu = blocks(ct.real, ct.imag, -ct.imag, ct.real)
    ta = np.exp(-2j * np.pi * DFT_KB * np.outer(np.arange(DFT_R // DFT_KB), a) / DFT_N)
    ta = np.concatenate([ta.real, ta.imag], axis=1)[:, :, None] * np.ones((1, 1, LANES))
    bf = lambda x: f32(x).astype(BF16)
    return bf(f1), bf(qu), bf(gu), bf(m1), f32(ta)


DFT_KB = 32
DFT_NB = 8


def _dft_stage1(x_ref, f1_ref, s_scr):
    def body(i, carry):
        n2b = i * DFT_NB
        xs = jnp.concatenate([x_ref[0, pl.ds(n2b + u, DFT_HALF, stride=DFT_R), :] for u in range(DFT_NB)],
                             axis=1).astype(BF16)
        a = jnp.dot(f1_ref[...], xs, preferred_element_type=F32)
        for u in range(DFT_NB):
            s_scr[pl.ds(pl.multiple_of((n2b + u) * DFT_PITCH, 8), 2 * DFT_R), :] = a[:, u * LANES:(u + 1) * LANES]
        return carry

    lax.fori_loop(0, DFT_R // DFT_NB, body, 0)


def _dft_stage2_rows(s_scr, ta_ref, step, u):
    k1 = step * DFT_KB + u
    re = s_scr[pl.ds(k1, DFT_R, stride=DFT_PITCH), :]
    im = s_scr[pl.ds(DFT_R + k1, DFT_R, stride=DFT_PITCH), :]
    tr, ti = ta_ref[step, :DFT_R, :], ta_ref[step, DFT_R:, :]
    return jnp.concatenate([re * tr - im * ti, re * ti + im * tr], axis=0).astype(BF16)


def _dft_fwd_kernel(x_ref, f1_ref, qu_ref, ta_ref, o_ref, s_scr):
    step = pl.program_id(2)
    pl.when(step == 0)(functools.partial(_dft_stage1, x_ref, f1_ref, s_scr))
    for u in range(DFT_KB):
        a = _dft_stage2_rows(s_scr, ta_ref, step, u)
        o_ref[0, u] = jnp.dot(qu_ref[u], a, preferred_element_type=F32).astype(o_ref.dtype)


def _dft_fwd(x, f1, qu, ta, out_dtype):
    nb, _, c = x.shape
    full = lambda a: pl.BlockSpec(a.shape, lambda b, ct, s: (0,) * a.ndim, pipeline_mode=pl.Buffered(1))
    return pl.pallas_call(
        _dft_fwd_kernel,
        grid=(nb, c // LANES, DFT_R // DFT_KB),
        in_specs=[pl.BlockSpec((1, SEQ, LANES), lambda b, ct, s: (b, 0, ct)), full(f1), full(qu), full(ta)],
        out_specs=pl.BlockSpec((1, DFT_KB, 2 * DFT_R, LANES), lambda b, ct, s: (b, s, 0, ct)),
        out_shape=jax.ShapeDtypeStruct((nb, DFT_R, 2 * DFT_R, c), out_dtype),
        scratch_shapes=[pltpu.VMEM((DFT_R * DFT_PITCH, LANES), F32)],
        compiler_params=_cparams(("parallel", "parallel", "arbitrary"), 40),
        name="hy_dft_fwd",
    )(x, f1, qu, ta)


def _dft_stage1_inv(s_scr, m1_ref, o_ref):
    def body(i, carry):
        n2b = i * DFT_NB
        bc = jnp.concatenate(
            [s_scr[pl.ds(pl.multiple_of((n2b + u) * DFT_PITCH, 8), 2 * DFT_R), :] for u in range(DFT_NB)],
            axis=1).astype(BF16)
        y = jnp.dot(m1_ref[...], bc, preferred_element_type=F32)
        for u in range(DFT_NB):
            o_ref[0, pl.ds(n2b + u, DFT_HALF, stride=DFT_R), :] = y[:, u * LANES:(u + 1) * LANES]
        return carry

    lax.fori_loop(0, DFT_R // DFT_NB, body, 0)


def _dft_conv_kernel(x_ref, f1_ref, qu_ref, ta_ref, hf_ref, hb_ref, gu_ref, m1_ref, o_ref, s_scr):
    step = pl.program_id(2)
    pl.when(step == 0)(functools.partial(_dft_stage1, x_ref, f1_ref, s_scr))
    tr, ti = ta_ref[step, :DFT_R, :], ta_ref[step, DFT_R:, :]
    for u in range(DFT_KB):
        k1 = step * DFT_KB + u
        xk = jnp.dot(qu_ref[u], _dft_stage2_rows(s_scr, ta_ref, step, u), preferred_element_type=F32)
        xr, xi = xk[:DFT_R], xk[DFT_R:]
        hr = hf_ref[0, u, :DFT_R, :].astype(F32) + hb_ref[0, u, :DFT_R, :].astype(F32)
        hi = hf_ref[0, u, DFT_R:, :].astype(F32) - hb_ref[0, u, DFT_R:, :].astype(F32)
        y = jnp.concatenate([xr * hr - xi * hi, xr * hi + xi * hr], axis=0).astype(BF16)
        b = jnp.dot(gu_ref[u], y, preferred_element_type=F32)
        br, bi = b[:DFT_R], b[DFT_R:]
        s_scr[pl.ds(k1, DFT_R, stride=DFT_PITCH), :] = br * tr + bi * ti
        s_scr[pl.ds(DFT_R + k1, DFT_R, stride=DFT_PITCH), :] = bi * tr - br * ti
    pl.when(step == pl.num_programs(2) - 1)(functools.partial(_dft_stage1_inv, s_scr, m1_ref, o_ref))


def _dft_conv(x, hs, f1, qu, gu, m1, ta):
    b, _, c = x.shape
    nct = c // LANES
    full = lambda a: pl.BlockSpec(a.shape, lambda bi, ct, s: (0,) * a.ndim, pipeline_mode=pl.Buffered(1))
    return pl.pallas_call(
        _dft_conv_kernel,
        grid=(b, nct, DFT_R // DFT_KB),
        in_specs=[
            pl.BlockSpec((1, SEQ, LANES), lambda bi, ct, s: (bi, 0, ct)),
            full(f1), full(qu), full(ta),
            pl.BlockSpec((1, DFT_KB, 2 * DFT_R, LANES), lambda bi, ct, s: (0, s, 0, ct)),
            pl.BlockSpec((1, DFT_KB, 2 * DFT_R, LANES), lambda bi, ct, s: (0, s, 0, nct + ct)),
            full(gu), full(m1),
        ],
        out_specs=pl.BlockSpec((1, SEQ, LANES), lambda bi, ct, s: (bi, 0, ct)),
        out_shape=jax.ShapeDtypeStruct((b, SEQ, c), F32),
        scratch_shapes=[pltpu.VMEM((DFT_R * DFT_PITCH, LANES), F32)],
        compiler_params=_cparams(("parallel", "parallel", "arbitrary"), 48),
        name="hy_dft_conv",
    )(x, f1, qu, ta, hs, hs, gu, m1)


def _mix_out_body(x, ya, yh, z, x0, yc, skip_ref, gg_ref, w_ref, gpost_ref):
    a0, b0 = NA_WIDTH, NA_WIDTH + HY_WIDTH
    yb = x0 * (yh + skip_ref[...] * z)
    acc = jnp.dot(_rms(ya, gg_ref[:, :a0]).astype(BF16), w_ref[:a0, :], preferred_element_type=F32)
    acc += jnp.dot(_rms(yb, gg_ref[:, a0:b0]).astype(BF16), w_ref[a0:b0, :], preferred_element_type=F32)
    acc += jnp.dot(_rms(yc, gg_ref[:, b0:]).astype(BF16), w_ref[b0:, :], preferred_element_type=F32)
    return x + _rms(acc, gpost_ref[...])


def _xattn_body(x, gpre_ref, wq_ref, kv_ref, wo_ref, gpost_ref):
    h = _rms(x, gpre_ref[...]).astype(BF16)
    q = jnp.dot(h, wq_ref[...], preferred_element_type=F32) * (XA_HEAD_DIM ** -0.5)
    outs = []
    for hd in range(XA_HEADS):
        c0 = hd * XA_HEAD_DIM
        qh = q[:, c0:c0 + XA_HEAD_DIM].astype(BF16)
        kh = kv_ref[0, :, c0:c0 + XA_HEAD_DIM]
        vh = kv_ref[0, :, D_MODEL + c0:D_MODEL + c0 + XA_HEAD_DIM]
        s = lax.dot_general(qh, kh, (((1,), (1,)), ((), ())), preferred_element_type=F32)
        p = jnp.exp(s - jnp.max(s, axis=-1, keepdims=True))
        l = jnp.sum(p, axis=-1, keepdims=True)
        outs.append(jnp.dot(p.astype(BF16), vh, preferred_element_type=F32) / l)
    o = jnp.concatenate(outs, axis=-1).astype(BF16)
    xa = jnp.dot(o, wo_ref[...], preferred_element_type=F32)
    return x + _rms(xa, gpost_ref[...])


POST_TM = 512


def _post_mixer_kernel(x_ref, ya_ref, yh_ref, z_ref, x0_ref, yc_ref, skip_ref, gg_ref, wout_ref, gmix_ref,
                       gxpre_ref, wq_ref, kv_ref, wo_ref, gxpost_ref, gfpre_ref, fwin_ref, fwout_ref, gfpost_ref,
                       o_ref):
    x = _mix_out_body(x_ref[...], ya_ref[...], yh_ref[...], z_ref[...], x0_ref[...], yc_ref[...],
                      skip_ref, gg_ref, wout_ref, gmix_ref)
    x = _xattn_body(x, gxpre_ref, wq_ref, kv_ref, wo_ref, gxpost_ref)
    o_ref[...] = _ffn_body(x, gfpre_ref, fwin_ref, fwout_ref, gfpost_ref)


def _post_mixer(x, layer, ya, yh, z, x0, yc, skip, gg, w_out, g_mix, g_xpre, wq, kv, wo, g_xpost, g_fpre, fw_in,
                fw_out, g_fpost):
    n = x.shape[0]
    per_batch = SEQ // POST_TM
    tok = lambda width: pl.BlockSpec((POST_TM, width), lambda i: (i, 0))
    const = _resident
    kv_spec = pl.BlockSpec((1,) + kv.shape[1:], lambda i: (i // per_batch, 0, 0))
    return pl.pallas_call(
        _post_mixer_kernel,
        grid=(n // POST_TM,),
        in_specs=[tok(D_MODEL), tok(NA_WIDTH), tok(HY_WIDTH), tok(HY_WIDTH), tok(HY_WIDTH), tok(DIL_WIDTH),
                  const(skip), const(gg), const(w_out, (layer,)), const(g_mix),
                  const(g_xpre), const(wq, (layer,)), kv_spec, const(wo, (layer,)), const(g_xpost),
                  const(g_fpre), const(fw_in, (layer, 1)), const(fw_out, (layer, 1)), const(g_fpost)],
        out_specs=tok(D_MODEL),
        out_shape=jax.ShapeDtypeStruct((n, D_MODEL), F32),
        compiler_params=_cparams(("parallel",), 56),
        name="post_mixer",
    )(x, ya, yh, z, x0, yc, skip, gg, w_out, g_mix, g_xpre, wq, kv, wo, g_xpost, g_fpre, fw_in, fw_out, g_fpost)


def kernel(x, mem, norm_g, w_in, w_out, na_rpb, t5_table, hy_conv_w, hy_conv_b, hy_f_w1, hy_f_b1, hy_f_w2, hy_f_b2, hy_f_w3, hy_f_b3, hy_f_w4, hy_freq, hy_decay, hy_skip, xa_wq, xa_wkv, xa_wo, ffn_w_in, ffn_w_out):
    bsz, seq, d = x.shape
    assert (seq, d) == (SEQ, D_MODEL)
    depth = norm_g.shape[0]
    mem_len = mem.shape[1]
    n = bsz * seq
    scale = HEAD_DIM ** -0.5 * LOG2E
    wa, wb, wc = 3 * NA_WIDTH, 3 * HY_WIDTH, 3 * DIL_WIDTH
    colscale = np.ones((1, wa + wb + wc), np.float32)
    colscale[:, :NA_WIDTH] = scale
    colscale[:, wa + wb:wa + wb + DIL_WIDTH] = scale
    colscale = jnp.asarray(colscale)

    zfeat = _filter_features()
    f1, qu, gu, m1, ta = _dft_tables()
    dil_bias = _dil_bias(t5_table)

    xf = x.reshape(n, d)
    memf = mem.reshape(bsz * mem_len, d)
    ffn_w_in, ffn_w_out, w_in, w_out, xa_wq, xa_wkv, xa_wo = (
        w.astype(BF16) for w in (ffn_w_in, ffn_w_out, w_in, w_out, xa_wq, xa_wkv, xa_wo))
    for l in range(depth):
        g = norm_g[l][:, None, :]
        xf, pa, pb, pc = _pre_mixer(xf, l, g[0], ffn_w_in, ffn_w_out, g[1], g[2], w_in, colscale)
        pa = pa.reshape(bsz, seq, wa)
        pb = pb.reshape(bsz, seq, wb)
        pc = pc.reshape(bsz, seq, wc)
        ya = _na_attention(pa, _na_tz(na_rpb[l]))
        z, x0 = _conv_gate(pb, hy_conv_w[l], hy_conv_b[l][None])
        hfilt = _hyena_filters(zfeat, hy_f_w1[l], hy_f_b1[l], hy_f_w2[l], hy_f_b2[l], hy_f_w3[l], hy_f_b3[l],
                               hy_f_w4[l], hy_freq[l], hy_decay[l])
        yh = _dft_conv(z, _dft_fwd(hfilt, f1, qu, ta, BF16), f1, qu, gu, m1, ta)
        yc = _dil_attention(pc, dil_bias)
        kv = _norm_mm(memf, g[6], xa_wkv, l, mem_len, BF16).reshape(bsz, mem_len, 2 * D_MODEL)
        xf = _post_mixer(xf, l, ya.reshape(n, NA_WIDTH), yh.reshape(n, HY_WIDTH), z.reshape(n, HY_WIDTH),
                         x0.reshape(n, HY_WIDTH), yc.reshape(n, DIL_WIDTH), hy_skip[l][None], g[3],
                         w_out, g[4], g[5], xa_wq, kv, xa_wo, g[7], g[8], ffn_w_in, ffn_w_out, g[9])
    return xf.reshape(bsz, seq, d)
```

```python
import functools
import math

import numpy as np
import jax
import jax.numpy as jnp
from jax import lax
from jax.experimental import pallas as pl
from jax.experimental.pallas import tpu as pltpu

F32 = jnp.float32
BF16 = jnp.bfloat16

D_MODEL = 1024
SEQ = 8192
GRID_W = 64
HEAD_DIM = 64
NA_WIDTH = 384
HY_WIDTH = 256
DIL_WIDTH = 384
NA_ROWS = 8
NA_COLS = 16
HY_BANDS = 16
DIL_CONFIGS = ((128, 1), (512, 4), (2048, 16))
T5_BUCKETS = 32
T5_MAX_DIST = 1024
XA_HEADS = 4
XA_HEAD_DIM = 256
FFN_HIDDEN = 2816
RMS_EPS = 1e-6
NEG_INF = -1e30
LOG2E = math.log2(math.e)

LANES = 128
MIB = 1024 * 1024

NA_QROWS = 8
NA_KROWS = 16
NA_BQ = NA_QROWS * GRID_W
NA_GROUPS = 4
NA_GQ = GRID_W // NA_GROUPS
NA_GK = 2 * NA_GQ
NA_STEP_BLOCKS = 8
DIL_HALF = 64
DIL_BQ = 128
DIL_UNROLL = 64
DIL_BK = DIL_BQ + 2 * DIL_HALF
DFT_R = 128
DFT_HALF = 64
DFT_N = DFT_R * DFT_R
DFT_PITCH = 264


def _cparams(sem, vmem_mib):
    return pltpu.CompilerParams(dimension_semantics=sem, vmem_limit_bytes=vmem_mib * MIB)


def _resident(a, lead=()):
    k = len(lead)
    return pl.BlockSpec((None,) * k + a.shape[k:], lambda i: tuple(lead) + (0,) * (a.ndim - k),
                        pipeline_mode=pl.Buffered(1))


def _rms(x, g):
    return x * lax.rsqrt(jnp.mean(x * x, axis=-1, keepdims=True) + RMS_EPS) * g


FFN_TM = 512
FFN_TH = 256


def _ffn_body(x, gpre_ref, win_ref, wout_ref, gpost_ref):
    h = _rms(x, gpre_ref[...]).astype(BF16)
    acc = jnp.zeros(x.shape, F32)
    for j in range(FFN_HIDDEN // FFN_TH):
        c0 = j * FFN_TH
        gate = jnp.dot(h, win_ref[:, c0:c0 + FFN_TH], preferred_element_type=F32)
        up = jnp.dot(h, win_ref[:, FFN_HIDDEN + c0:FFN_HIDDEN + c0 + FFN_TH], preferred_element_type=F32)
        act = (gate * jax.nn.sigmoid(gate) * up).astype(BF16)
        acc += jnp.dot(act, wout_ref[c0:c0 + FFN_TH, :], preferred_element_type=F32)
    return x + 0.5 * _rms(acc, gpost_ref[...])


def _pre_mixer_kernel(x_ref, gfpre_ref, fwin_ref, fwout_ref, gfpost_ref, gmix_ref, w_ref, cs_ref,
                      o_ref, pa_ref, pb_ref, pc_ref):
    x = _ffn_body(x_ref[...], gfpre_ref, fwin_ref, fwout_ref, gfpost_ref)
    o_ref[...] = x
    h = _rms(x, gmix_ref[...]).astype(BF16)
    c0 = 0
    for ref in (pa_ref, pb_ref, pc_ref):
        c1 = c0 + ref.shape[1]
        ref[...] = jnp.dot(h, w_ref[:, c0:c1], preferred_element_type=F32) * cs_ref[:, c0:c1]
        c0 = c1


def _pre_mixer(x, layer, g_fpre, fw_in, fw_out, g_fpost, g_mix, w, colscale):
    n = x.shape[0]
    widths = (3 * NA_WIDTH, 3 * HY_WIDTH, 3 * DIL_WIDTH)
    tok = lambda width: pl.BlockSpec((FFN_TM, width), lambda i: (i, 0))
    const = _resident
    return pl.pallas_call(
        _pre_mixer_kernel,
        grid=(n // FFN_TM,),
        in_specs=[tok(D_MODEL), const(g_fpre), const(fw_in, (layer, 0)), const(fw_out, (layer, 0)), const(g_fpost),
                  const(g_mix), const(w, (layer,)), const(colscale)],
        out_specs=[tok(D_MODEL)] + [tok(wd) for wd in widths],
        out_shape=[jax.ShapeDtypeStruct((n, wd), F32) for wd in (D_MODEL,) + widths],
        compiler_params=_cparams(("parallel",), 56),
        name="pre_mixer",
    )(x, g_fpre, fw_in, fw_out, g_fpost, g_mix, w, colscale)


def _norm_mm_kernel(x_ref, g_ref, w_ref, o_ref):
    h = _rms(x_ref[...], g_ref[...]).astype(BF16)
    o_ref[...] = jnp.dot(h, w_ref[...], preferred_element_type=F32).astype(o_ref.dtype)


def _norm_mm(x, g, w, layer, tm, out_dtype):
    n, ncol = x.shape[0], w.shape[2]
    return pl.pallas_call(
        _norm_mm_kernel,
        grid=(n // tm,),
        in_specs=[pl.BlockSpec((tm, D_MODEL), lambda i: (i, 0)), _resident(g), _resident(w, (layer,))],
        out_specs=pl.BlockSpec((tm, ncol), lambda i: (i, 0)),
        out_shape=jax.ShapeDtypeStruct((n, ncol), out_dtype),
        compiler_params=_cparams(("parallel",), 32),
        name="norm_mm",
    )(x, g, w)


def _softmax_pv(q, k, v, bias):
    s = lax.dot_general(q, k, (((1,), (1,)), ((), ())), preferred_element_type=F32) + bias
    m = jnp.max(s, axis=-1, keepdims=True)
    p = jnp.exp2(s - m).astype(BF16)
    o = jnp.dot(p, jnp.concatenate([v, jnp.ones_like(v)], axis=1), preferred_element_type=F32)
    return o[:, :LANES], m, o[:, LANES:]


def _pair_attention(q, k, v, bias, stack_heads):
    bq = q.shape[0]
    lane = lax.broadcasted_iota(jnp.int32, (1, LANES), 1)
    first = lane < HEAD_DIM
    q0 = jnp.where(first, q, 0.0).astype(BF16)
    q1 = jnp.where(first, 0.0, q).astype(BF16)
    if stack_heads:
        res = _softmax_pv(jnp.concatenate([q0, q1], axis=0), k, v, bias)
        return tuple(jnp.where(first, a[:bq], a[bq:]) for a in res)
    res0 = _softmax_pv(q0, k, v, bias[:bq])
    res1 = _softmax_pv(q1, k, v, bias[bq:])
    return tuple(jnp.where(first, a, b) for a, b in zip(res0, res1))


def _na_key_col0(g):
    return int(np.clip(g * NA_GQ - NA_COLS // 2, 0, GRID_W - NA_GK))


def _na_fill_bias(tz_ref, bias_scr, j):
    nrows = SEQ // GRID_W
    start_row = int(np.clip(j * NA_QROWS - NA_ROWS // 2, 0, nrows - NA_KROWS))
    neg = jnp.full((NA_GQ, NA_GK), NEG_INF, F32)
    per_tile = LANES // NA_GK
    for g in range(NA_GROUPS):
        for hh in range(2):
            for rl in range(NA_QROWS):
                r = j * NA_QROWS + rl
                rs = int(np.clip(r - NA_ROWS // 2, 0, nrows - NA_ROWS))
                r0 = (hh * NA_QROWS + rl) * NA_GQ
                for kt in range(NA_KROWS // per_tile):
                    tiles = []
                    for kl in range(kt * per_tile, (kt + 1) * per_tile):
                        kr = start_row + kl
                        tiles.append(tz_ref[hh, kr - r + NA_ROWS - 1, g] if rs <= kr < rs + NA_ROWS else neg)
                    bias_scr[g, r0:r0 + NA_GQ, kt * LANES:(kt + 1) * LANES] = jnp.concatenate(tiles, axis=1)


def _na_row_block(q_ref, k_ref, v_ref, o_ref, bias_scr, j, q0):
    nrows = SEQ // GRID_W
    nblk = SEQ // NA_BQ
    table = jnp.where(j == 0, 0, jnp.where(j == nblk - 1, 2, 1))
    start_row = jnp.clip(j * NA_QROWS - NA_ROWS // 2, 0, nrows - NA_KROWS)
    start = start_row * GRID_W
    for g in range(NA_GROUPS):
        kc0 = _na_key_col0(g)
        krows = [pl.ds(pl.multiple_of(start + kl * GRID_W + kc0, 8), NA_GK) for kl in range(NA_KROWS)]
        qrows = [slice(q0 + rl * GRID_W + g * NA_GQ, q0 + rl * GRID_W + (g + 1) * NA_GQ) for rl in range(NA_QROWS)]
        k = jnp.concatenate([k_ref[0, r, :] for r in krows], axis=0).astype(BF16)
        v = jnp.concatenate([v_ref[0, r, :] for r in krows], axis=0).astype(BF16)
        q = jnp.concatenate([q_ref[0, r, :] for r in qrows], axis=0)
        o, _, l = _pair_attention(q, k, v, bias_scr[table, g], stack_heads=True)
        o = o / l
        for rl, r in enumerate(qrows):
            o_ref[0, r, :] = o[rl * NA_GQ:(rl + 1) * NA_GQ]


def _na_kernel(q_ref, k_ref, v_ref, tz_ref, o_ref, bias_scr):
    nblk = SEQ // NA_BQ

    @pl.when((pl.program_id(1) == 0) & (pl.program_id(2) == 0))
    def _():
        for table, j in enumerate((0, 1, nblk - 1)):
            _na_fill_bias(tz_ref, bias_scr.at[table], j)

    step = pl.program_id(2)
    for t in range(NA_STEP_BLOCKS):
        _na_row_block(q_ref, k_ref, v_ref, o_ref, bias_scr, NA_STEP_BLOCKS * step + t, t * NA_BQ)


def _na_attention(pa, tzg):
    b = pa.shape[0]
    npair = NA_WIDTH // LANES
    nstep = SEQ // (NA_STEP_BLOCKS * NA_BQ)
    return pl.pallas_call(
        _na_kernel,
        grid=(npair, b, nstep),
        in_specs=[
            pl.BlockSpec((1, NA_STEP_BLOCKS * NA_BQ, LANES), lambda p, bi, j: (bi, j, p)),
            pl.BlockSpec((1, SEQ, LANES), lambda p, bi, j: (bi, 0, npair + p)),
            pl.BlockSpec((1, SEQ, LANES), lambda p, bi, j: (bi, 0, 2 * npair + p)),
            pl.BlockSpec((2,) + tzg.shape[1:], lambda p, bi, j: (p, 0, 0, 0, 0)),
        ],
        out_specs=pl.BlockSpec((1, NA_STEP_BLOCKS * NA_BQ, LANES), lambda p, bi, j: (bi, j, p)),
        out_shape=jax.ShapeDtypeStruct((b, SEQ, NA_WIDTH), F32),
        scratch_shapes=[pltpu.VMEM((3, NA_GROUPS, 2 * NA_QROWS * NA_GQ, NA_KROWS * NA_GK), F32)],
        compiler_params=_cparams(("parallel", "arbitrary", "arbitrary"), 48),
        name="na_attn",
    )(pa, pa, pa, tzg)


def _na_tz(rpb):
    col = np.arange(GRID_W)
    cs = np.clip(col - NA_COLS // 2, 0, GRID_W - NA_COLS)
    col_ok = (col[None, :] >= cs[:, None]) & (col[None, :] < cs[:, None] + NA_COLS)
    period = GRID_W + 1
    half = NA_COLS - 1
    filler = jnp.zeros(rpb.shape[:2] + (period - (2 * half + 1),), rpb.dtype)
    v = jnp.concatenate([rpb[..., half:], filler, rpb[..., :half]], axis=-1)
    t = jnp.tile(v, (1, 1, GRID_W))[..., :GRID_W * GRID_W].reshape(rpb.shape[:2] + (GRID_W, GRID_W))
    t = jnp.where(col_ok[None, None], t * LOG2E, NEG_INF)
    groups = []
    for g in range(NA_GROUPS):
        kc0 = _na_key_col0(g)
        assert col_ok[g * NA_GQ:(g + 1) * NA_GQ, :kc0].sum() == 0 and col_ok[g * NA_GQ:(g + 1) * NA_GQ, kc0 + NA_GK:].sum() == 0
        groups.append(t[:, :, g * NA_GQ:(g + 1) * NA_GQ, kc0:kc0 + NA_GK])
    return jnp.stack(groups, axis=2)


def _dil_kernel(q_ref, k_ref, v_ref, bias_ref, o_ref, m_scr, l_scr):
    order = sorted(range(len(DIL_CONFIGS)), key=lambda c: -DIL_CONFIGS[c][1])
    for c in order:
        dil = DIL_CONFIGS[c][1]
        m_len = SEQ // dil
        nblk = m_len // DIL_BQ

        def merge(qrows, o, m, l, c=c):
            if c == order[0]:
                o_ref[0, qrows, :] = o
                m_scr[qrows, :] = m
                l_scr[qrows, :] = l
                return
            m_old = m_scr[qrows, :]
            m_new = jnp.maximum(m_old, m)
            a_old = jnp.exp2(m_old - m_new)
            a_new = jnp.exp2(m - m_new)
            o_new = o_ref[0, qrows, :] * a_old + o * a_new
            l_new = l_scr[qrows, :] * a_old + l * a_new
            if c == order[-1]:
                o_ref[0, qrows, :] = o_new / l_new
            else:
                o_ref[0, qrows, :] = o_new
                l_scr[qrows, :] = l_new
                m_scr[qrows, :] = m_new

        def window(t, lo, pos, hi):
            return jnp.concatenate([t(lo, DIL_HALF), t(pos, DIL_BQ), t(hi, DIL_HALF)], axis=0)

        def block_body(it, carry, c=c, dil=dil, m_len=m_len, nblk=nblk):
            g = it // nblk
            n = it % nblk
            pos = n * DIL_BQ

            def rows(p0, size):
                return pl.ds(g + p0 * dil, size, stride=dil) if dil > 1 else pl.ds(p0, size)

            lo = jnp.maximum(pos - DIL_HALF, 0)
            hi = jnp.minimum(pos + DIL_BQ, m_len - DIL_HALF)
            k = window(lambda p0, size: k_ref[0, rows(p0, size), :], lo, pos, hi).astype(BF16)
            v = window(lambda p0, size: v_ref[0, rows(p0, size), :], lo, pos, hi).astype(BF16)
            edge_case = jnp.where(n == 0, 0, jnp.where(n == nblk - 1, 2, 1))
            qrows = rows(pos, DIL_BQ)
            merge(qrows, *_pair_attention(q_ref[0, qrows, :], k, v, bias_ref[c, 0, edge_case], stack_heads=True))
            return carry

        def class_body(g, carry, c=c, dil=dil, m_len=m_len, nblk=nblk):
            rows = pl.ds(g, m_len, stride=dil)
            q = q_ref[0, rows, :]
            kc = k_ref[0, rows, :].astype(BF16)
            vc = v_ref[0, rows, :].astype(BF16)
            res = []
            for n in range(nblk):
                pos = n * DIL_BQ
                lo, hi = max(pos - DIL_HALF, 0), min(pos + DIL_BQ, m_len - DIL_HALF)
                k = window(lambda p0, size: kc[p0:p0 + size], lo, pos, hi)
                v = window(lambda p0, size: vc[p0:p0 + size], lo, pos, hi)
                edge_case = 0 if n == 0 else 2 if n == nblk - 1 else 1
                res.append(_pair_attention(q[pos:pos + DIL_BQ], k, v, bias_ref[c, 0, edge_case], stack_heads=True))
            merge(rows, *(jnp.concatenate(parts, axis=0) for parts in zip(*res)))
            return carry

        if nblk <= DIL_UNROLL:
            lax.fori_loop(0, dil, class_body, 0, unroll=DIL_UNROLL // nblk)
        else:
            lax.fori_loop(0, SEQ // DIL_BQ, block_body, 0, unroll=DIL_UNROLL)


def _dil_attention(pc, bias):
    b = pc.shape[0]
    npair = DIL_WIDTH // LANES
    assert all(SEQ // dil // DIL_BQ >= 2 for _, dil in DIL_CONFIGS)
    return pl.pallas_call(
        _dil_kernel,
        grid=(b, npair),
        in_specs=[
            pl.BlockSpec((1, SEQ, LANES), lambda bi, p: (bi, 0, p)),
            pl.BlockSpec((1, SEQ, LANES), lambda bi, p: (bi, 0, npair + p)),
            pl.BlockSpec((1, SEQ, LANES), lambda bi, p: (bi, 0, 2 * npair + p)),
            pl.BlockSpec((len(DIL_CONFIGS), 1, 3, 2 * DIL_BQ, DIL_BK), lambda bi, p: (0, p, 0, 0, 0)),
        ],
        out_specs=pl.BlockSpec((1, SEQ, LANES), lambda bi, p: (bi, 0, p)),
        out_shape=jax.ShapeDtypeStruct((b, SEQ, DIL_WIDTH), F32),
        scratch_shapes=[pltpu.VMEM((SEQ, LANES), F32), pltpu.VMEM((SEQ, LANES), F32)],
        compiler_params=_cparams(("parallel", "arbitrary"), 56),
        name="dil_attn",
    )(pc, pc, pc, bias)


def _t5_bucket(rel):
    half = T5_BUCKETS // 2
    exact = half // 2
    n = np.abs(rel)
    far = exact + (np.log(np.maximum(n, 1) / exact) / math.log(T5_MAX_DIST / exact) * (half - exact)).astype(np.int32)
    far = np.minimum(far, half - 1)
    return (np.where(rel > 0, half, 0) + np.where(n < exact, n, far)).astype(np.int32)


def _dil_bias(t5_table):
    rel = np.arange(-DIL_HALF, DIL_HALF + 1)
    period = DIL_BK + 1
    nh = t5_table.shape[1]
    out = []
    for _, dil in DIL_CONFIGS:
        vals = t5_table[_t5_bucket(dil * rel)].T
        v = jnp.concatenate([vals, jnp.full((nh, period - vals.shape[1]), NEG_INF, vals.dtype)], axis=1)
        out.append(jnp.tile(v, (1, DIL_BQ))[:, :DIL_BQ * DIL_BK].reshape(nh, DIL_BQ, DIL_BK))
    t = (jnp.stack(out) * LOG2E).reshape(len(DIL_CONFIGS), nh // 2, 1, 2 * DIL_BQ, DIL_BK)
    kcol = np.arange(DIL_BK)
    edge = np.stack([kcol < DIL_HALF, np.zeros_like(kcol, bool), kcol >= DIL_HALF + DIL_BQ])
    return jnp.where(edge[None, None, :, None, :], NEG_INF, t)


CONV_TM = 1024


def _conv_gate_kernel(p_ref, prev_ref, next_ref, w_ref, b_ref, z_ref, x0_ref):
    i = pl.program_id(1)
    p = p_ref[0]
    tm = p.shape[0]
    first = jnp.where(i > 0, prev_ref[0, 7:8, :], 0.0)
    last = jnp.where(i < pl.num_programs(1) - 1, next_ref[0, 0:1, :], 0.0)
    row = lax.broadcasted_iota(jnp.int32, (tm, 1), 0)
    up = jnp.where(row == 0, first, pltpu.roll(p, 1, axis=0))
    dn = jnp.where(row == tm - 1, last, pltpu.roll(p, tm - 1, axis=0))
    uc = up * w_ref[0:1, :] + p * w_ref[1:2, :] + dn * w_ref[2:3, :] + b_ref[...]
    z_ref[0] = uc[:, HY_WIDTH:2 * HY_WIDTH] * uc[:, :HY_WIDTH]
    x0_ref[0] = uc[:, 2 * HY_WIDTH:]


def _conv_gate(pb, w, bvec):
    b = pb.shape[0]
    wb = 3 * HY_WIDTH
    nt = SEQ // CONV_TM
    r8 = CONV_TM // 8
    return pl.pallas_call(
        _conv_gate_kernel,
        grid=(b, nt),
        in_specs=[
            pl.BlockSpec((1, CONV_TM, wb), lambda bi, i: (bi, i, 0)),
            pl.BlockSpec((1, 8, wb), lambda bi, i: (bi, jnp.maximum(i * r8 - 1, 0), 0)),
            pl.BlockSpec((1, 8, wb), lambda bi, i: (bi, jnp.minimum((i + 1) * r8, SEQ // 8 - 1), 0)),
            pl.BlockSpec((3, wb), lambda bi, i: (0, 0)),
            pl.BlockSpec((1, wb), lambda bi, i: (0, 0)),
        ],
        out_specs=[
            pl.BlockSpec((1, CONV_TM, HY_WIDTH), lambda bi, i: (bi, i, 0)),
            pl.BlockSpec((1, CONV_TM, HY_WIDTH), lambda bi, i: (bi, i, 0)),
        ],
        out_shape=[jax.ShapeDtypeStruct((b, SEQ, HY_WIDTH), F32)] * 2,
        compiler_params=_cparams(("parallel", "parallel"), 48),
        name="hy_conv_gate",
    )(pb, pb, pb, w, bvec)


FILT_TM = 1024
FILT_KPAD = 128
FILT_HALF = SEQ // 2


def _filter_kernel(z_ref, w1_ref, b1_ref, w2_ref, b2_ref, w3_ref, b3_ref, w4_ref, fr_ref, dec_ref, h_ref):
    hp = lax.Precision.HIGHEST
    z = z_ref[...]
    fr = fr_ref[...]
    h = jnp.sin(fr * (jnp.dot(z, w1_ref[...], precision=hp, preferred_element_type=F32) + b1_ref[...]))
    h = jnp.sin(fr * (jnp.dot(h, w2_ref[...], precision=hp, preferred_element_type=F32) + b2_ref[...]))
    h = jnp.sin(fr * (jnp.dot(h, w3_ref[...], precision=hp, preferred_element_type=F32) + b3_ref[...]))
    h = jnp.dot(h.astype(BF16), w4_ref[...], preferred_element_type=F32)
    wout = 2 * HY_WIDTH
    dec = jnp.abs(dec_ref[...])
    lo = h[:, :wout] * jnp.exp(-z[:, 0:1] * dec)
    hi = h[:, wout:] * jnp.exp(-z[:, FILT_KPAD:FILT_KPAD + 1] * dec)
    row = pl.program_id(0) * FILT_TM + lax.broadcasted_iota(jnp.int32, (FILT_TM, 1), 0)
    col = lax.broadcasted_iota(jnp.int32, (1, wout), 1)
    h_ref[0] = jnp.where((row == 0) & (col >= HY_WIDTH), 0.0, lo)
    h_ref[1] = hi


def _filter_features():
    t = jnp.linspace(0.0, 1.0, SEQ, dtype=F32)[:, None]
    bands = jnp.linspace(1e-4, HY_BANDS - 1, HY_BANDS, dtype=F32)[None, :]
    ang = (2.0 * math.pi / SEQ) * jnp.arange(SEQ, dtype=F32)[:, None] * bands
    z = jnp.concatenate([t, jnp.cos(ang), -jnp.sin(ang)], axis=-1)
    z = jnp.pad(z, ((0, 0), (0, FILT_KPAD - z.shape[1])))
    return jnp.concatenate([z[:FILT_HALF], z[FILT_HALF:]], axis=1)


def _hyena_filters(zfeat, w1, b1, w2, b2, w3, b3, w4, freq, decay):
    def diag2(w):
        zero = jnp.zeros_like(w)
        return jnp.concatenate([jnp.concatenate([w, zero], axis=1), jnp.concatenate([zero, w], axis=1)], axis=0)

    twice = lambda v: jnp.concatenate([v, v])[None]
    w1p = jnp.pad(w1, ((0, FILT_KPAD - w1.shape[0]), (0, 0)))
    full = lambda a: pl.BlockSpec(a.shape, lambda i: (0,) * a.ndim)
    args = (diag2(w1p), twice(b1), diag2(w2), twice(b2), diag2(w3), twice(b3), diag2(w4).astype(BF16), twice(freq),
            decay[None])
    out = pl.pallas_call(
        _filter_kernel,
        grid=(FILT_HALF // FILT_TM,),
        in_specs=[pl.BlockSpec((FILT_TM, 2 * FILT_KPAD), lambda i: (i, 0))] + [full(a) for a in args],
        out_specs=pl.BlockSpec((2, FILT_TM, 2 * HY_WIDTH), lambda i: (0, i, 0)),
        out_shape=jax.ShapeDtypeStruct((2, FILT_HALF, 2 * HY_WIDTH), F32),
        compiler_params=_cparams(("parallel",), 32),
        name="hy_filter",
    )(zfeat, *args)
    return out.reshape(1, SEQ, 2 * HY_WIDTH)


def _dft_tables():
    a = np.arange(DFT_R)
    w = np.exp(-2j * np.pi * np.outer(a, a) / DFT_R)
    tw = np.exp(-2j * np.pi * np.outer(a, a) / DFT_N)
    f32 = lambda x: jnp.asarray(x.astype(np.float32))
    f1 = np.concatenate([w.real[:, :DFT_HALF], w.imag[:, :DFT_HALF]], axis=0)
    m1 = np.concatenate([w.real[:DFT_HALF], w.imag[:DFT_HALF]], axis=1) / DFT_N
    c = w[None] * tw[:, :DFT_KB].T[:, None, :]
    blocks = lambda a, b, c, d: np.concatenate(
        [np.concatenate([a, b], axis=2), np.concatenate([c, d], axis=2)], axis=1)
    qu = blocks(c.real, -c.imag, c.imag, c.real)
    ct = np.swapaxes(c, 1, 2)
    gu = blocks(ct.real, ct.imag, -ct.imag, ct.real)
    ta = np.exp(-2j * np.pi * DFT_KB * np.outer(np.arange(DFT_R // DFT_KB), a) / DFT_N)
    ta = np.concatenate([ta.real, ta.imag], axis=1)[:, :, None] * np.ones((1, 1, LANES))
    bf = lambda x: f32(x).astype(BF16)
    return bf(f1), bf(qu), bf(gu), bf(m1), f32(ta)


DFT_KB = 32
DFT_NB = 8


def _dft_stage1(x_ref, f1_ref, s_scr):
    def body(i, carry):
        n2b = i * DFT_NB
        xs = jnp.concatenate([x_ref[0, pl.ds(n2b + u, DFT_HALF, stride=DFT_R), :] for u in range(DFT_NB)],
                             axis=1).astype(BF16)
        a = jnp.dot(f1_ref[...], xs, preferred_element_type=F32)
        for u in range(DFT_NB):
            s_scr[pl.ds(pl.multiple_of((n2b + u) * DFT_PITCH, 8), 2 * DFT_R), :] = a[:, u * LANES:(u + 1) * LANES]
        return carry

    lax.fori_loop(0, DFT_R // DFT_NB, body, 0, unroll=4)


def _dft_stage2_rows(s_scr, ta_ref, step, u):
    k1 = step * DFT_KB + u
    re = s_scr[pl.ds(k1, DFT_R, stride=DFT_PITCH), :]
    im = s_scr[pl.ds(DFT_R + k1, DFT_R, stride=DFT_PITCH), :]
    tr, ti = ta_ref[step, :DFT_R, :], ta_ref[step, DFT_R:, :]
    return jnp.concatenate([re * tr - im * ti, re * ti + im * tr], axis=0).astype(BF16)


def _dft_fwd_kernel(x_ref, f1_ref, qu_ref, ta_ref, o_ref, s_scr):
    step = pl.program_id(2)
    pl.when(step == 0)(functools.partial(_dft_stage1, x_ref, f1_ref, s_scr))
    for u in range(DFT_KB):
        a = _dft_stage2_rows(s_scr, ta_ref, step, u)
        o_ref[0, u] = jnp.dot(qu_ref[u], a, preferred_element_type=F32).astype(o_ref.dtype)


def _dft_fwd(x, f1, qu, ta, out_dtype):
    nb, _, c = x.shape
    full = lambda a: pl.BlockSpec(a.shape, lambda b, ct, s: (0,) * a.ndim, pipeline_mode=pl.Buffered(1))
    return pl.pallas_call(
        _dft_fwd_kernel,
        grid=(nb, c // LANES, DFT_R // DFT_KB),
        in_specs=[pl.BlockSpec((1, SEQ, LANES), lambda b, ct, s: (b, 0, ct)), full(f1), full(qu), full(ta)],
        out_specs=pl.BlockSpec((1, DFT_KB, 2 * DFT_R, LANES), lambda b, ct, s: (b, s, 0, ct)),
        out_shape=jax.ShapeDtypeStruct((nb, DFT_R, 2 * DFT_R, c), out_dtype),
        scratch_shapes=[pltpu.VMEM((DFT_R * DFT_PITCH, LANES), F32)],
        compiler_params=_cparams(("parallel", "parallel", "arbitrary"), 40),
        name="hy_dft_fwd",
    )(x, f1, qu, ta)


def _dft_stage1_inv(s_scr, m1_ref, o_ref):
    def body(i, carry):
        n2b = i * DFT_NB
        bc = jnp.concatenate(
            [s_scr[pl.ds(pl.multiple_of((n2b + u) * DFT_PITCH, 8), 2 * DFT_R), :] for u in range(DFT_NB)],
            axis=1).astype(BF16)
        y = jnp.dot(m1_ref[...], bc, preferred_element_type=F32)
        for u in range(DFT_NB):
            o_ref[0, pl.ds(n2b + u, DFT_HALF, stride=DFT_R), :] = y[:, u * LANES:(u + 1) * LANES]
        return carry

    lax.fori_loop(0, DFT_R // DFT_NB, body, 0, unroll=4)


def _dft_conv_kernel(x_ref, f1_ref, qu_ref, ta_ref, hf_ref, hb_ref, gu_ref, m1_ref, o_ref, s_scr):
    step = pl.program_id(2)
    pl.when(step == 0)(functools.partial(_dft_stage1, x_ref, f1_ref, s_scr))
    tr, ti = ta_ref[step, :DFT_R, :], ta_ref[step, DFT_R:, :]
    for u in range(DFT_KB):
        k1 = step * DFT_KB + u
        xk = jnp.dot(qu_ref[u], _dft_stage2_rows(s_scr, ta_ref, step, u), preferred_element_type=F32)
        xr, xi = xk[:DFT_R], xk[DFT_R:]
        hr = hf_ref[0, u, :DFT_R, :].astype(F32) + hb_ref[0, u, :DFT_R, :].astype(F32)
        hi = hf_ref[0, u, DFT_R:, :].astype(F32) - hb_ref[0, u, DFT_R:, :].astype(F32)
        y = jnp.concatenate([xr * hr - xi * hi, xr * hi + xi * hr], axis=0).astype(BF16)
        b = jnp.dot(gu_ref[u], y, preferred_element_type=F32)
        br, bi = b[:DFT_R], b[DFT_R:]
        s_scr[pl.ds(k1, DFT_R, stride=DFT_PITCH), :] = br * tr + bi * ti
        s_scr[pl.ds(DFT_R + k1, DFT_R, stride=DFT_PITCH), :] = bi * tr - br * ti
    pl.when(step == pl.num_programs(2) - 1)(functools.partial(_dft_stage1_inv, s_scr, m1_ref, o_ref))


def _dft_conv(x, hs, f1, qu, gu, m1, ta):
    b, _, c = x.shape
    nct = c // LANES
    full = lambda a: pl.BlockSpec(a.shape, lambda bi, ct, s: (0,) * a.ndim, pipeline_mode=pl.Buffered(1))
    return pl.pallas_call(
        _dft_conv_kernel,
        grid=(b, nct, DFT_R // DFT_KB),
        in_specs=[
            pl.BlockSpec((1, SEQ, LANES), lambda bi, ct, s: (bi, 0, ct)),
            full(f1), full(qu), full(ta),
            pl.BlockSpec((1, DFT_KB, 2 * DFT_R, LANES), lambda bi, ct, s: (0, s, 0, ct)),
            pl.BlockSpec((1, DFT_KB, 2 * DFT_R, LANES), lambda bi, ct, s: (0, s, 0, nct + ct)),
            full(gu), full(m1),
        ],
        out_specs=pl.BlockSpec((1, SEQ, LANES), lambda bi, ct, s: (bi, 0, ct)),
        out_shape=jax.ShapeDtypeStruct((b, SEQ, c), F32),
        scratch_shapes=[pltpu.VMEM((DFT_R * DFT_PITCH, LANES), F32)],
        compiler_params=_cparams(("parallel", "parallel", "arbitrary"), 48),
        name="hy_dft_conv",
    )(x, f1, qu, ta, hs, hs, gu, m1)


def _mix_out_body(x, ya, yh, z, x0, yc, skip_ref, gg_ref, w_ref, gpost_ref):
    a0, b0 = NA_WIDTH, NA_WIDTH + HY_WIDTH
    yb = x0 * (yh + skip_ref[...] * z)
    acc = jnp.dot(_rms(ya, gg_ref[:, :a0]).astype(BF16), w_ref[:a0, :], preferred_element_type=F32)
    acc += jnp.dot(_rms(yb, gg_ref[:, a0:b0]).astype(BF16), w_ref[a0:b0, :], preferred_element_type=F32)
    acc += jnp.dot(_rms(yc, gg_ref[:, b0:]).astype(BF16), w_ref[b0:, :], preferred_element_type=F32)
    return x + _rms(acc, gpost_ref[...])


def _xattn_body(x, gpre_ref, wq_ref, kv_ref, wo_ref, gpost_ref):
    h = _rms(x, gpre_ref[...]).astype(BF16)
    q = jnp.dot(h, wq_ref[...], preferred_element_type=F32) * (XA_HEAD_DIM ** -0.5)
    outs = []
    for hd in range(XA_HEADS):
        c0 = hd * XA_HEAD_DIM
        qh = q[:, c0:c0 + XA_HEAD_DIM].astype(BF16)
        kh = kv_ref[0, :, c0:c0 + XA_HEAD_DIM]
        vh = kv_ref[0, :, D_MODEL + c0:D_MODEL + c0 + XA_HEAD_DIM]
        s = lax.dot_general(qh, kh, (((1,), (1,)), ((), ())), preferred_element_type=F32)
        p = jnp.exp(s - jnp.max(s, axis=-1, keepdims=True))
        l = jnp.sum(p, axis=-1, keepdims=True)
        outs.append(jnp.dot(p.astype(BF16), vh, preferred_element_type=F32) / l)
    o = jnp.concatenate(outs, axis=-1).astype(BF16)
    xa = jnp.dot(o, wo_ref[...], preferred_element_type=F32)
    return x + _rms(xa, gpost_ref[...])


POST_TM = 512


def _post_mixer_kernel(x_ref, ya_ref, yh_ref, z_ref, x0_ref, yc_ref, skip_ref, gg_ref, wout_ref, gmix_ref,
                       gxpre_ref, wq_ref, kv_ref, wo_ref, gxpost_ref, gfpre_ref, fwin_ref, fwout_ref, gfpost_ref,
                       o_ref):
    x = _mix_out_body(x_ref[...], ya_ref[...], yh_ref[...], z_ref[...], x0_ref[...], yc_ref[...],
                      skip_ref, gg_ref, wout_ref, gmix_ref)
    x = _xattn_body(x, gxpre_ref, wq_ref, kv_ref, wo_ref, gxpost_ref)
    o_ref[...] = _ffn_body(x, gfpre_ref, fwin_ref, fwout_ref, gfpost_ref)


def _post_mixer(x, layer, ya, yh, z, x0, yc, skip, gg, w_out, g_mix, g_xpre, wq, kv, wo, g_xpost, g_fpre, fw_in,
                fw_out, g_fpost):
    n = x.shape[0]
    per_batch = SEQ // POST_TM
    tok = lambda width: pl.BlockSpec((POST_TM, width), lambda i: (i, 0))
    const = _resident
    kv_spec = pl.BlockSpec((1,) + kv.shape[1:], lambda i: (i // per_batch, 0, 0))
    return pl.pallas_call(
        _post_mixer_kernel,
        grid=(n // POST_TM,),
        in_specs=[tok(D_MODEL), tok(NA_WIDTH), tok(HY_WIDTH), tok(HY_WIDTH), tok(HY_WIDTH), tok(DIL_WIDTH),
                  const(skip), const(gg), const(w_out, (layer,)), const(g_mix),
                  const(g_xpre), const(wq, (layer,)), kv_spec, const(wo, (layer,)), const(g_xpost),
                  const(g_fpre), const(fw_in, (layer, 1)), const(fw_out, (layer, 1)), const(g_fpost)],
        out_specs=tok(D_MODEL),
        out_shape=jax.ShapeDtypeStruct((n, D_MODEL), F32),
        compiler_params=_cparams(("parallel",), 56),
        name="post_mixer",
    )(x, ya, yh, z, x0, yc, skip, gg, w_out, g_mix, g_xpre, wq, kv, wo, g_xpost, g_fpre, fw_in, fw_out, g_fpost)


def kernel(x, mem, norm_g, w_in, w_out, na_rpb, t5_table, hy_conv_w, hy_conv_b, hy_f_w1, hy_f_b1, hy_f_w2, hy_f_b2, hy_f_w3, hy_f_b3, hy_f_w4, hy_freq, hy_decay, hy_skip, xa_wq, xa_wkv, xa_wo, ffn_w_in, ffn_w_out):
    bsz, seq, d = x.shape
    assert (seq, d) == (SEQ, D_MODEL)
    depth = norm_g.shape[0]
    mem_len = mem.shape[1]
    n = bsz * seq
    scale = HEAD_DIM ** -0.5 * LOG2E
    wa, wb, wc = 3 * NA_WIDTH, 3 * HY_WIDTH, 3 * DIL_WIDTH
    colscale = np.ones((1, wa + wb + wc), np.float32)
    colscale[:, :NA_WIDTH] = scale
    colscale[:, wa + wb:wa + wb + DIL_WIDTH] = scale
    colscale = jnp.asarray(colscale)

    zfeat = _filter_features()
    f1, qu, gu, m1, ta = _dft_tables()
    dil_bias = _dil_bias(t5_table)

    xf = x.reshape(n, d)
    memf = mem.reshape(bsz * mem_len, d)
    ffn_w_in, ffn_w_out, w_in, w_out, xa_wq, xa_wkv, xa_wo = (
        w.astype(BF16) for w in (ffn_w_in, ffn_w_out, w_in, w_out, xa_wq, xa_wkv, xa_wo))
    for l in range(depth):
        g = norm_g[l][:, None, :]
        xf, pa, pb, pc = _pre_mixer(xf, l, g[0], ffn_w_in, ffn_w_out, g[1], g[2], w_in, colscale)
        pa = pa.reshape(bsz, seq, wa)
        pb = pb.reshape(bsz, seq, wb)
        pc = pc.reshape(bsz, seq, wc)
        ya = _na_attention(pa, _na_tz(na_rpb[l]))
        z, x0 = _conv_gate(pb, hy_conv_w[l], hy_conv_b[l][None])
        hfilt = _hyena_filters(zfeat, hy_f_w1[l], hy_f_b1[l], hy_f_w2[l], hy_f_b2[l], hy_f_w3[l], hy_f_b3[l],
                               hy_f_w4[l], hy_freq[l], hy_decay[l])
        yh = _dft_conv(z, _dft_fwd(hfilt, f1, qu, ta, BF16), f1, qu, gu, m1, ta)
        yc = _dil_attention(pc, dil_bias)
        kv = _norm_mm(memf, g[6], xa_wkv, l, mem_len, BF16).reshape(bsz, mem_len, 2 * D_MODEL)
        xf = _post_mixer(xf, l, ya.reshape(n, NA_WIDTH), yh.reshape(n, HY_WIDTH), z.reshape(n, HY_WIDTH),
                         x0.reshape(n, HY_WIDTH), yc.reshape(n, DIL_WIDTH), hy_skip[l][None], g[3],
                         w_out, g[4], g[5], xa_wq, kv, xa_wo, g[7], g[8], ffn_w_in, ffn_w_out, g[9])
    return xf.reshape(bsz, seq, d)
```
